```python
import math
import jax, jax.numpy as jnp
from jax import lax
import numpy as np

D_MODEL = 1024
BATCH = 8
SEQ = 4096
DEPTH = 4
DEC_BATCH = 16
DEC_SEQ = 2048
PAST_LEN = 128

N_MIXERS = 3
N_RWKV = (DEPTH + 2) // 3
N_HYENA = (DEPTH + 1) // 3
N_CONV = DEPTH // 3
WIDTH = D_MODEL
RMS_EPS = 1e-6
LN_EPS = 1e-5
RWKV_HEAD_SIZE = 64
RWKV_HEADS = WIDTH // RWKV_HEAD_SIZE
DECAY_LORA = 64
ICLR_LORA = 64
N_SHIFT_GROUPS = 6
RWKV_COLS = (WIDTH, WIDTH, WIDTH, WIDTH, 2 * DECAY_LORA, 2 * ICLR_LORA)
RWKV_SPLITS = [WIDTH, 2 * WIDTH, 3 * WIDTH, 4 * WIDTH, 4 * WIDTH + 2 * DECAY_LORA]
GN_EPS = 64e-5
HYENA_ORDER = 2
HYENA_SHORT = 3
POS_EMB = 33
POS_BANDS = (POS_EMB - 1) // 2
FILTER_HIDDEN = 64
FILTER_MID_LAYERS = 2
FAST_DECAY_PCT = 0.3
SLOW_DECAY_PCT = 1.5
DECAY_TARGET = 1e-2
CONV_WIDTH = 31

kernel_name = 'hybrid_rwkv7_hyena_conformer_encoder'


def rmsnorm(x, g):
    xf = x.astype(jnp.float32)
    y = xf * lax.rsqrt(jnp.mean(xf * xf, axis=-1, keepdims=True) + RMS_EPS)
    return (y * g).astype(x.dtype)


def layernorm(x, g, b):
    xf = x.astype(jnp.float32)
    mean = jnp.mean(xf, axis=-1, keepdims=True)
    var = jnp.mean(jnp.square(xf - mean), axis=-1, keepdims=True)
    return ((xf - mean) * lax.rsqrt(var + LN_EPS) * g + b).astype(x.dtype)


def depthwise_conv(x, w, b):
    K = w.shape[0]
    y = lax.conv_general_dilated(x, w[:, None, :].astype(x.dtype), window_strides=(1,),
                                 padding=[((K - 1) // 2, (K - 1) // 2)],
                                 dimension_numbers=('NWC', 'WIO', 'NWC'),
                                 feature_group_count=x.shape[-1])
    return y + b


def rwkv7_mixer(h, w_in, mu, w0, w2, a0, a2, k_k, k_a, r_k, gn_w, gn_b, w_out):
    B, L, D = h.shape
    E, H, N = WIDTH, RWKV_HEADS, RWKV_HEAD_SIZE
    f32 = jnp.float32
    pad = jnp.zeros_like(h[:, :1])
    xx = 0.5 * (jnp.concatenate([pad, h[:, :-1]], axis=1) + jnp.concatenate([h[:, 1:], pad], axis=1)) - h
    mu_cols = jnp.concatenate([jnp.broadcast_to(mu[i][:, None], (D, n)) for i, n in enumerate(RWKV_COLS)], axis=1)
    proj = h @ w_in + xx @ (w_in * mu_cols)
    r, k, v, g, wl, al = jnp.split(proj, RWKV_SPLITS, axis=-1)
    r, k, v = (t.astype(f32) for t in (r, k, v))
    w_raw = w0[:, None, None, :] + jnp.einsum('bldr,dre->dble', jnp.tanh(wl.reshape(B, L, 2, DECAY_LORA)), w2)
    w_log = -jax.nn.softplus(-w_raw.astype(f32)) - 0.5
    decay = jnp.exp(-jnp.exp(w_log))
    a = jax.nn.sigmoid((a0[:, None, None, :] + jnp.einsum('bldr,dre->dble', al.reshape(B, L, 2, ICLR_LORA), a2)).astype(f32))
    kk = (k * k_k).reshape(B, L, H, N)
    kk = (kk / jnp.maximum(jnp.sqrt(jnp.sum(kk * kk, axis=-1, keepdims=True)), 1e-12)).reshape(B, L, E)
    k_d = k[None] * (1.0 + (a - 1.0) * k_a)
    b_d = kk[None] * a

    def time_major(z_f, z_b):
        z = jnp.stack([z_f, jnp.flip(z_b, axis=1)], axis=0).reshape(2, B, L, H, N)
        return jnp.transpose(z, (2, 0, 1, 3, 4))

    xs = (time_major(r, r), time_major(decay[0], decay[1]), time_major(k_d[0], k_d[1]),
          time_major(v, v), time_major(-kk, -kk), time_major(b_d[0], b_d[1]))

    def step(S, inp):
        r_t, w_t, k_t, v_t, a_t, b_t = inp
        sa = jnp.einsum('dbhij,dbhj->dbhi', S, a_t)
        S = S * w_t[..., None, :] + sa[..., None] * b_t[..., None, :] + v_t[..., None] * k_t[..., None, :]
        return S, jnp.einsum('dbhij,dbhj->dbhi', S, r_t)

    S0 = jnp.zeros((2, B, H, N, N), f32)
    _, ys = lax.scan(step, S0, xs)
    y = jnp.transpose(ys[:, 0] + jnp.flip(ys[:, 1], axis=0), (1, 0, 2, 3))
    mean = jnp.mean(y, axis=-1, keepdims=True)
    var = jnp.mean(jnp.square(y - mean), axis=-1, keepdims=True)
    y = ((y - mean) * lax.rsqrt(var + GN_EPS)).reshape(B, L, E) * gn_w + gn_b
    bonus = jnp.einsum('blhn,dblhn,hn->blh', r.reshape(B, L, H, N), k_d.reshape(2, B, L, H, N), r_k.astype(f32))
    y = y + (bonus[..., None] * v.reshape(B, L, H, N)).reshape(B, L, E)
    return (y.astype(h.dtype) * jax.nn.silu(g)) @ w_out


def hyena_filters(L, f_w1, f_b1, f_w_mid, f_b_mid, f_freq, f_w_out, f_decay):
    f32 = jnp.float32
    pos = jnp.arange(L, dtype=f32)
    t = pos / max(L - 1, 1)
    bands = jnp.linspace(1e-4, POS_BANDS - 1, POS_BANDS, dtype=f32)
    ang = (2.0 * math.pi / L) * pos[:, None] * bands[None, :]
    z = jnp.concatenate([t[:, None], jnp.cos(ang), -jnp.sin(ang)], axis=-1)
    z = jnp.sin(f_freq[0] * (z @ f_w1 + f_b1))
    for i in range(FILTER_MID_LAYERS):
        z = jnp.sin(f_freq[i + 1] * (z @ f_w_mid[i] + f_b_mid[i]))
    hf = (z @ f_w_out).astype(f32).reshape(L, HYENA_ORDER, 2, WIDTH)
    window = jnp.exp(-t[:, None, None, None] * jnp.abs(f_decay.astype(f32)).reshape(HYENA_ORDER, 2, WIDTH))
    hf = hf * window
    fwd, bwd = hf[:, :, 0], hf[:, :, 1]
    k2 = jnp.concatenate([fwd, jnp.zeros_like(fwd[:1]), jnp.flip(bwd[1:], axis=0)], axis=0)
    k2 = k2 / jnp.sum(jnp.abs(k2), axis=0, keepdims=True)
    return jnp.fft.rfft(k2, axis=0)


def long_conv(u, K, bias):
    L = u.shape[1]
    U = jnp.fft.rfft(u, n=2 * L, axis=1)
    y = jnp.fft.irfft(U * K[None], n=2 * L, axis=1)[:, :L]
    return y + u * bias


def hyena_mixer(h, w_in, short_w, short_b, f_w1, f_b1, f_w_mid, f_b_mid, f_freq, f_w_out, f_decay, bias, w_out):
    B, L, _ = h.shape
    E = WIDTH
    f32 = jnp.float32
    proj = h @ w_in
    u3 = depthwise_conv(proj[..., :3 * E], short_w, short_b)
    g = proj[..., 3 * E:]
    x1, x2, v = jnp.split(u3.astype(f32), 3, axis=-1)
    K = hyena_filters(L, f_w1, f_b1, f_w_mid, f_b_mid, f_freq, f_w_out, f_decay)
    bias = bias.astype(f32)
    z = x1 * long_conv(v, K[:, 0], bias[0])
    z = x2 * long_conv(z, K[:, 1], bias[1])
    return (z.astype(h.dtype) * jax.nn.silu(g)) @ w_out


def conformer_conv_mixer(h, w_in, dw_w, dw_b, ln_g, ln_b, w_out):
    a, b, g = jnp.split(h @ w_in, 3, axis=-1)
    u = a * jax.nn.sigmoid(b)
    u = depthwise_conv(u, dw_w, dw_b)
    u = jax.nn.silu(layernorm(u, ln_g, ln_b))
    return (u * jax.nn.silu(g)) @ w_out


def trunk(x, c, norm_g, mod_w, mod_b, rw, hy, cf, final_g):
    for i in range(DEPTH):
        mod = jax.nn.silu(c) @ mod_w[i] + mod_b[i]
        shift, scale, gate = jnp.split(mod[:, None, :], 3, axis=-1)
        hn = rmsnorm(x, norm_g[i]) * (1 + scale) + shift
        kind, j = i % N_MIXERS, i // N_MIXERS
        if kind == 0:
            o = rwkv7_mixer(hn, *[p[j] for p in rw])
        elif kind == 1:
            o = hyena_mixer(hn, *[p[j] for p in hy])
        else:
            o = conformer_conv_mixer(hn, *[p[j] for p in cf])
        x = x + (gate * o).astype(x.dtype)
    return rmsnorm(x, final_g)


def setup_inputs(seed: int = 0) -> dict:
    key = jax.random.key(seed)
    ks = iter(jax.random.split(key, 48))
    f32 = jnp.float32
    D, E, H, N = D_MODEL, WIDTH, RWKV_HEADS, RWKV_HEAD_SIZE

    def nrm(shape, scale):
        return scale * jax.random.normal(next(ks), shape, f32)

    decay_lo = abs(math.log(DECAY_TARGET) / SLOW_DECAY_PCT)
    decay_hi = abs(math.log(DECAY_TARGET) / FAST_DECAY_PCT)
    return {
        'x_prompt': nrm((BATCH, SEQ, D), 1.0),
        'x_sample': nrm((DEC_BATCH, DEC_SEQ, D), 1.0),
        'c_prompt': nrm((BATCH, D), 1.0),
        'c_sample': nrm((DEC_BATCH, D), 1.0),
        'norm_g': 1.0 + nrm((DEPTH, D), 0.02),
        'mod_w': nrm((DEPTH, D, 3 * D), 0.3 * D ** -0.5),
        'mod_b': nrm((DEPTH, 3 * D), 0.02),
        'rw_w_in': nrm((N_RWKV, D, sum(RWKV_COLS)), D ** -0.5),
        'rw_mu': jax.random.uniform(next(ks), (N_RWKV, N_SHIFT_GROUPS, D), f32),
        'rw_w0': jnp.broadcast_to(jnp.linspace(-6.5, -1.5, E, dtype=f32), (N_RWKV, 2, E)) + nrm((N_RWKV, 2, E), 0.1),
        'rw_w2': nrm((N_RWKV, 2, DECAY_LORA, E), 0.5 * DECAY_LORA ** -0.5),
        'rw_a0': nrm((N_RWKV, 2, E), 0.1),
        'rw_a2': nrm((N_RWKV, 2, ICLR_LORA, E), 0.5 * ICLR_LORA ** -0.5),
        'rw_k_k': 0.85 + nrm((N_RWKV, E), 0.02),
        'rw_k_a': 1.0 + nrm((N_RWKV, E), 0.02),
        'rw_r_k': nrm((N_RWKV, H, N), 0.1),
        'rw_gn_w': 1.0 + nrm((N_RWKV, E), 0.02),
        'rw_gn_b': nrm((N_RWKV, E), 0.02),
        'rw_w_out': nrm((N_RWKV, E, D), E ** -0.5),
        'hy_w_in': nrm((N_HYENA, D, 4 * E), D ** -0.5),
        'hy_short_w': nrm((N_HYENA, HYENA_SHORT, 3 * E), HYENA_SHORT ** -0.5),
        'hy_short_b': nrm((N_HYENA, 3 * E), 0.02),
        'hy_f_w1': nrm((N_HYENA, POS_EMB, FILTER_HIDDEN), POS_EMB ** -0.5),
        'hy_f_b1': nrm((N_HYENA, FILTER_HIDDEN), 0.02),
        'hy_f_w_mid': nrm((N_HYENA, FILTER_MID_LAYERS, FILTER_HIDDEN, FILTER_HIDDEN), FILTER_HIDDEN ** -0.5),
        'hy_f_b_mid': nrm((N_HYENA, FILTER_MID_LAYERS, FILTER_HIDDEN), 0.02),
        'hy_f_freq': 1.0 + nrm((N_HYENA, FILTER_MID_LAYERS + 1, FILTER_HIDDEN), 0.02),
        'hy_f_w_out': nrm((N_HYENA, FILTER_HIDDEN, HYENA_ORDER * 2 * E), FILTER_HIDDEN ** -0.5),
        'hy_f_decay': jnp.linspace(decay_lo, decay_hi, E, dtype=f32) * (1.0 + nrm((N_HYENA, HYENA_ORDER * 2, E), 0.05)),
        'hy_bias': nrm((N_HYENA, HYENA_ORDER, E), 1.0),
        'hy_w_out': nrm((N_HYENA, E, D), E ** -0.5),
        'cf_w_in': nrm((N_CONV, D, 3 * E), D ** -0.5),
        'cf_dw_w': nrm((N_CONV, CONV_WIDTH, E), CONV_WIDTH ** -0.5),
        'cf_dw_b': nrm((N_CONV, E), 0.02),
        'cf_ln_g': 1.0 + nrm((N_CONV, E), 0.02),
        'cf_ln_b': nrm((N_CONV, E), 0.02),
        'cf_w_out': nrm((N_CONV, E, D), E ** -0.5),
        'final_g': 1.0 + nrm((D,), 0.02),
    }


def reference(x_prompt, x_sample, c_prompt, c_sample, norm_g, mod_w, mod_b,
              rw_w_in, rw_mu, rw_w0, rw_w2, rw_a0, rw_a2, rw_k_k, rw_k_a, rw_r_k, rw_gn_w, rw_gn_b, rw_w_out,
              hy_w_in, hy_short_w, hy_short_b, hy_f_w1, hy_f_b1, hy_f_w_mid, hy_f_b_mid, hy_f_freq, hy_f_w_out,
              hy_f_decay, hy_bias, hy_w_out,
              cf_w_in, cf_dw_w, cf_dw_b, cf_ln_g, cf_ln_b, cf_w_out,
              final_g):
    rw = (rw_w_in, rw_mu, rw_w0, rw_w2, rw_a0, rw_a2, rw_k_k, rw_k_a, rw_r_k, rw_gn_w, rw_gn_b, rw_w_out)
    hy = (hy_w_in, hy_short_w, hy_short_b, hy_f_w1, hy_f_b1, hy_f_w_mid, hy_f_b_mid, hy_f_freq, hy_f_w_out,
          hy_f_decay, hy_bias, hy_w_out)
    cf = (cf_w_in, cf_dw_w, cf_dw_b, cf_ln_g, cf_ln_b, cf_w_out)
    y_prompt = trunk(x_prompt, c_prompt, norm_g, mod_w, mod_b, rw, hy, cf, final_g)
    y_sample = trunk(x_sample, c_sample, norm_g, mod_w, mod_b, rw, hy, cf, final_g)
    return (y_prompt, y_sample)
```

```python
import functools
import math

import numpy as np
import jax
import jax.numpy as jnp
from jax import lax
from jax.experimental import pallas as pl
from jax.experimental.pallas import tpu as pltpu

F32 = jnp.float32
BF16 = jnp.bfloat16

D_MODEL = 1024
WIDTH = 1024
DEPTH = 4
HEADS = 16
HEAD = 64
LORA = 64
RMS_EPS = 1e-6
LN_EPS = 1e-5
GN_EPS = 64e-5
POS_BANDS = 16
FILTER_HIDDEN = 64
CONV_WIDTH = 31
CONV_HALO = 16
LANES = 128
SEG_GROUP = 256
FFT_N2 = 64
VMEM_LIMIT = 56 * 1024 * 1024


def _cparams(*sem):
    return pltpu.CompilerParams(dimension_semantics=sem, vmem_limit_bytes=VMEM_LIMIT)


def _dot(a, b):
    return jnp.dot(a, b, preferred_element_type=F32)


def _split(x):
    hi = x.astype(BF16)
    lo = (x - hi.astype(F32)).astype(BF16)
    return hi, lo


def _dot_x3(a, b):
    ah, al = _split(a)
    bh, bl = _split(b)
    return _dot(ah, bh) + _dot(ah, bl) + _dot(al, bh)


def _dot_presplit(a, bh, bl):
    ah, al = _split(a)
    return _dot(ah, bh) + _dot(ah, bl) + _dot(al, bh)


def _seg_bcast(t, bd):
    outs = []
    for g in range(t.shape[1] // SEG_GROUP):
        th, tl = _split(t[:, g * SEG_GROUP:(g + 1) * SEG_GROUP])
        outs.append(_dot(th, bd) + _dot(tl, bd))
    return jnp.concatenate(outs, axis=1)


def _sigmoid(x):
    return 1.0 / (1.0 + jnp.exp(-x))


def _silu(x):
    return x * _sigmoid(x)


def _prenorm(x, ng, shift, scale):
    ms = jnp.mean(x * x, axis=-1, keepdims=True)
    return (x * lax.rsqrt(ms + RMS_EPS) * ng) * (1.0 + scale) + shift


def _mod_kernel(c_ref, w_ref, b_ref, o_ref):
    o_ref[0] = _dot_x3(_silu(c_ref[...]), w_ref[0]) + b_ref[0]


def _modulation(c_all, mod_w, mod_b):
    bc = c_all.shape[0]
    out = pl.pallas_call(
        _mod_kernel,
        grid=(DEPTH, 3),
        in_specs=[
            pl.BlockSpec((bc, D_MODEL), lambda i, j: (0, 0)),
            pl.BlockSpec((1, D_MODEL, D_MODEL), lambda i, j: (i, 0, j)),
            pl.BlockSpec((1, 1, D_MODEL), lambda i, j: (i, 0, j)),
        ],
        out_specs=pl.BlockSpec((1, bc, D_MODEL), lambda i, j: (i, 0, j)),
        out_shape=jax.ShapeDtypeStruct((DEPTH, bc, 3 * D_MODEL), F32),
        compiler_params=_cparams("arbitrary", "arbitrary"),
        name="modulation",
    )(c_all, mod_w, mod_b.reshape(DEPTH, 1, 3 * D_MODEL))
    return out.reshape(DEPTH, bc, 3, D_MODEL)


RW_TL = 128


def _rw_proj_kernel(x_ref, xp_ref, xn_ref, mod_ref, ng_ref, mu_ref, win_ref,
                    w0_ref, w2h_ref, w2l_ref, a0_ref, a2h_ref, a2l_ref,
                    kk_ref, ka_ref, rk_ref, bd_ref,
                    r_o, v_o, nkk_o, g_o, bv_o, w0_o, k0_o, b0_o, w1_o, k1_o, b1_o):
    t = pl.program_id(1)
    nt = pl.num_programs(1)
    tl = x_ref.shape[1]
    e = WIDTH
    shift = mod_ref[0, 0:1, :]
    scale = mod_ref[0, 1:2, :]
    ng = ng_ref[...]
    bd = bd_ref[...]

    h = _prenorm(x_ref[0], ng, shift, scale)
    hp = _prenorm(xp_ref[0], ng, shift, scale)[7:8, :]
    hn = _prenorm(xn_ref[0], ng, shift, scale)[0:1, :]
    hp = jnp.where(t > 0, hp, 0.0)
    hn = jnp.where(t < nt - 1, hn, 0.0)
    rows = lax.broadcasted_iota(jnp.int32, (tl, 1), 0)
    h_up = jnp.where(rows == 0, hp, pltpu.roll(h, 1, 0))
    h_dn = jnp.where(rows == tl - 1, hn, pltpu.roll(h, tl - 1, 0))
    xx = 0.5 * (h_up + h_dn) - h

    def proj(i, c0, c1):
        inp = (h + xx * mu_ref[i:i + 1, :]).astype(BF16)
        return _dot(inp, win_ref[:, c0:c1])

    r = proj(0, 0, e)
    k = proj(1, e, 2 * e)
    v = proj(2, 2 * e, 3 * e)
    g = proj(3, 3 * e, 4 * e)
    wl = proj(4, 4 * e, 4 * e + 2 * LORA)
    al = proj(5, 4 * e + 2 * LORA, 4 * e + 4 * LORA)
    twl = jnp.tanh(wl)

    kk0 = k * kk_ref[...]
    nrm = jnp.sqrt(_seg_bcast(kk0 * kk0, bd))
    kk = kk0 / jnp.maximum(nrm, 1e-12)
    ka = ka_ref[...]

    r_o[0] = r
    v_o[0] = v
    g_o[0] = g
    nkk_o[0] = -kk

    ksum = jnp.zeros_like(k)
    for d, (w_o, k_o, b_o) in enumerate(((w0_o, k0_o, b0_o), (w1_o, k1_o, b1_o))):
        w_raw = w0_ref[d:d + 1, :] + _dot_presplit(twl, w2h_ref[d], w2l_ref[d])
        z = -w_raw
        softplus = jnp.maximum(z, 0.0) + jnp.log(1.0 + jnp.exp(-jnp.abs(z)))
        w_log = -softplus - 0.5
        w_o[0] = jnp.exp(-jnp.exp(w_log))
        a = _sigmoid(a0_ref[d:d + 1, :] + _dot_presplit(al, a2h_ref[d], a2l_ref[d]))
        k_d = k * (1.0 + (a - 1.0) * ka)
        k_o[0] = k_d
        b_o[0] = kk * a
        ksum = ksum + k_d

    bonus = _seg_bcast(r * ksum * rk_ref[...], bd)
    bv_o[0] = bonus * v


def _rw_scan_kernel(rf, vf, af, wf, kf, bf, rb, vb, ab, wb, kb, bb, bd_ref, dl_ref,
                    yf_o, yb_o, s_ref):
    tb = pl.program_id(1)
    tc = rf.shape[1]

    @pl.when(tb == 0)
    def _():
        s_ref[...] = jnp.zeros_like(s_ref)

    bd = bd_ref[...]
    delta = dl_ref[...]

    def chain(s, r_row, v_row, a_row, w_row, k_row, b_row):
        sa = _seg_bcast(s * a_row, bd)
        vbc = _seg_bcast(delta * v_row, bd)
        s = s * w_row + sa * b_row + vbc * k_row
        z = _seg_bcast(s * r_row, bd)
        y_row = jnp.sum(z * delta, axis=0, keepdims=True)
        return s, y_row

    def body(i, carry):
        j = tc - 1 - i
        row = lambda ref, idx: ref[0, pl.ds(idx, 1), :]
        s0, y0 = chain(s_ref[0], row(rf, i), row(vf, i), row(af, i), row(wf, i), row(kf, i), row(bf, i))
        s1, y1 = chain(s_ref[1], row(rb, j), row(vb, j), row(ab, j), row(wb, j), row(kb, j), row(bb, j))
        s_ref[0] = s0
        s_ref[1] = s1
        yf_o[0, pl.ds(i, 1), :] = y0
        yb_o[0, pl.ds(j, 1), :] = y1
        return carry

    lax.fori_loop(0, tc, body, 0)


def _rw_out_kernel(yf_ref, yb_ref, bv_ref, g_ref, x_ref, mod_ref, gw_ref, gb_ref, wout_ref, bd_ref, o_ref):
    bd = bd_ref[...]
    y = yf_ref[0] + yb_ref[0]
    mean = _seg_bcast(y, bd) * (1.0 / HEAD)
    yc = y - mean
    var = _seg_bcast(yc * yc, bd) * (1.0 / HEAD)
    yn = yc * lax.rsqrt(var + GN_EPS) * gw_ref[...] + gb_ref[...]
    yo = (yn + bv_ref[0]) * _silu(g_ref[0])
    o = _dot(yo.astype(BF16), wout_ref[...])
    o_ref[0] = x_ref[0] + mod_ref[0, 2:3, :] * o


def _pad_lora(w):
    z = jnp.zeros_like(w[0])
    return jnp.stack([jnp.concatenate([w[0], z], axis=0), jnp.concatenate([z, w[1]], axis=0)], axis=0)


def _split_host(w):
    hi = w.astype(BF16)
    lo = (w - hi.astype(F32)).astype(BF16)
    return hi, lo


def _rwkv_layer(x, mod, ng, p, consts, scan_tc):
    (w_in, mu, w0, w2, a0, a2, k_k, k_a, r_k, gn_w, gn_b, w_out) = p
    b, l, d = x.shape
    e = WIDTH
    tl = RW_TL
    nt = l // tl
    bd, delta = consts["bd"], consts["delta"]
    w2h, w2l = _split_host(_pad_lora(w2))
    a2h, a2l = _split_host(_pad_lora(a2))
    ncols = w_in.shape[1]

    full = lambda shape: pl.BlockSpec(shape, lambda bi, ti: (0,) * len(shape))
    tile = pl.BlockSpec((1, tl, e), lambda bi, ti: (bi, ti, 0))
    hb = tl // 8
    lb = l // 8
    outs = pl.pallas_call(
        _rw_proj_kernel,
        grid=(b, nt),
        in_specs=[
            tile,
            pl.BlockSpec((1, 8, d), lambda bi, ti: (bi, jnp.maximum(ti * hb - 1, 0), 0)),
            pl.BlockSpec((1, 8, d), lambda bi, ti: (bi, jnp.minimum((ti + 1) * hb, lb - 1), 0)),
            pl.BlockSpec((1, 3, d), lambda bi, ti: (bi, 0, 0)),
            full((1, d)), full((6, d)), full((d, ncols)),
            full((2, e)), full((2, 2 * LORA, e)), full((2, 2 * LORA, e)),
            full((2, e)), full((2, 2 * LORA, e)), full((2, 2 * LORA, e)),
            full((1, e)), full((1, e)), full((1, e)),
            full((SEG_GROUP, SEG_GROUP)),
        ],
        out_specs=[tile] * 11,
        out_shape=[jax.ShapeDtypeStruct((b, l, e), F32)] * 11,
        compiler_params=_cparams("parallel", "arbitrary"),
        name="rwkv_proj",
    )(x, x, x, mod, ng.reshape(1, d), mu, w_in.astype(BF16),
      w0, w2h, w2l, a0, a2h, a2l,
      k_k.reshape(1, e), k_a.reshape(1, e), r_k.reshape(1, e), bd)
    r, v, nkk, g, bv, wd0, kd0, bd0, wd1, kd1, bd1 = outs

    tc = scan_tc
    ntc = l // tc
    fwd = pl.BlockSpec((1, tc, e), lambda bi, ti: (bi, ti, 0))
    bwd = pl.BlockSpec((1, tc, e), lambda bi, ti: (bi, ntc - 1 - ti, 0))
    yf, yb = pl.pallas_call(
        _rw_scan_kernel,
        grid=(b, ntc),
        in_specs=[fwd] * 6 + [bwd] * 6 + [full((SEG_GROUP, SEG_GROUP)), full((HEAD, e))],
        out_specs=[fwd, bwd],
        out_shape=[jax.ShapeDtypeStruct((b, l, e), F32)] * 2,
        scratch_shapes=[pltpu.VMEM((2, HEAD, e), F32)],
        compiler_params=_cparams("parallel", "arbitrary"),
        name="rwkv_scan",
    )(r, v, nkk, wd0, kd0, bd0, r, v, nkk, wd1, kd1, bd1, bd, delta)

    tlo = 256
    tile_o = pl.BlockSpec((1, tlo, e), lambda bi, ti: (bi, ti, 0))
    return pl.pallas_call(
        _rw_out_kernel,
        grid=(b, l // tlo),
        in_specs=[tile_o] * 5 + [
            pl.BlockSpec((1, 3, d), lambda bi, ti: (bi, 0, 0)),
            full((1, e)), full((1, e)), full((e, d)), full((SEG_GROUP, SEG_GROUP)),
        ],
        out_specs=tile_o,
        out_shape=jax.ShapeDtypeStruct((b, l, d), F32),
        compiler_params=_cparams("parallel", "arbitrary"),
        name="rwkv_out",
    )(yf, yb, bv, g, x, mod, gn_w.reshape(1, e), gn_b.reshape(1, e), w_out.astype(BF16), bd)


def _norm_proj_kernel(x_ref, mod_ref, ng_ref, w_ref, o_ref):
    h = _prenorm(x_ref[0], ng_ref[...], mod_ref[0, 0:1, :], mod_ref[0, 1:2, :])
    o_ref[0] = _dot(h.astype(BF16), w_ref[...])


def _norm_proj(x, mod, ng, w, name):
    b, l, d = x.shape
    n = w.shape[1]
    tl = 256
    return pl.pallas_call(
        _norm_proj_kernel,
        grid=(b, l // tl),
        in_specs=[
            pl.BlockSpec((1, tl, d), lambda bi, ti: (bi, ti, 0)),
            pl.BlockSpec((1, 3, d), lambda bi, ti: (bi, 0, 0)),
            pl.BlockSpec((1, d), lambda bi, ti: (0, 0)),
            pl.BlockSpec((d, n), lambda bi, ti: (0, 0)),
        ],
        out_specs=pl.BlockSpec((1, tl, n), lambda bi, ti: (bi, ti, 0)),
        out_shape=jax.ShapeDtypeStruct((b, l, n), F32),
        compiler_params=_cparams("parallel", "arbitrary"),
        name=name,
    )(x, mod, ng.reshape(1, d), w.astype(BF16))


def _short_conv_kernel(u_ref, w_ref, b_ref, o_ref):
    u = u_ref[0]
    l = u.shape[0]
    rows = lax.broadcasted_iota(jnp.int32, (l, 1), 0)
    up = jnp.where(rows == 0, 0.0, pltpu.roll(u, 1, 0))
    dn = jnp.where(rows == l - 1, 0.0, pltpu.roll(u, l - 1, 0))
    o_ref[0] = up * w_ref[0:1, :] + u * w_ref[1:2, :] + dn * w_ref[2:3, :] + b_ref[...]


def _short_conv(proj, w, bias):
    b, l, _ = proj.shape
    n = w.shape[1]
    blk = pl.BlockSpec((1, l, LANES), lambda bi, ci: (bi, 0, ci))
    return pl.pallas_call(
        _short_conv_kernel,
        grid=(b, n // LANES),
        in_specs=[blk,
                  pl.BlockSpec((3, LANES), lambda bi, ci: (0, ci)),
                  pl.BlockSpec((1, LANES), lambda bi, ci: (0, ci))],
        out_specs=blk,
        out_shape=jax.ShapeDtypeStruct((b, l, n), F32),
        compiler_params=_cparams("parallel", "arbitrary"),
        name="hyena_short_conv",
    )(proj, w, bias.reshape(1, n))


def _cmm(s4, s2, xre, xim, conj=False):
    r = s2.shape[0] // 2

    def prods(x):
        xh, xl = _split(x)
        p4 = _dot(s4, xh)
        p2 = _dot(s2, xl)
        re_x = p4[0:r] + p4[r:2 * r] + p2[0:r]
        im_x = p4[2 * r:3 * r] + p4[3 * r:4 * r] + p2[r:2 * r]
        return re_x, im_x

    rr, ir = prods(xre)
    if xim is None:
        return rr, ir
    ri, ii = prods(xim)
    if conj:
        return rr + ii, ri - ir
    return rr - ii, ir + ri


def _stacks(mre, mim):
    reh, rel = _split(mre)
    imh, iml = _split(mim)
    return jnp.concatenate([reh, rel, imh, iml], axis=0), jnp.concatenate([reh, imh], axis=0)


def _fft_fwd_outer(src_re, src_im, dst_re, dst_im, fa4_ref, fa2_ref, n1, n2, k1, scale=None):
    fa4 = fa4_ref[...]
    fa2 = fa2_ref[...]

    def body(i, carry):
        xre = src_re[pl.ds(i, k1, stride=n2), :]
        xim = None if src_im is None else src_im[pl.ds(i, k1, stride=n2), :]
        if scale is not None:
            xre = xre * scale
        are, aim = _cmm(fa4, fa2, xre, xim)
        dst_re[pl.ds(i, n1, stride=n2), :] = are
        dst_im[pl.ds(i, n1, stride=n2), :] = aim
        return carry

    lax.fori_loop(0, n2, body, 0)


def _twiddled_inner(c2_ref, s2_ref, twr_ref, twi_ref, k):
    tre = twr_ref[pl.ds(k, 1), :]
    tim = twi_ref[pl.ds(k, 1), :]
    c2 = c2_ref[...]
    s2 = s2_ref[...]
    return _stacks(c2 * tre - s2 * tim, c2 * tim + s2 * tre)


def _lconv_kernel(u_ref, x_ref, kf_ref, bias_ref, fa4_ref, fa2_ref, c2_ref, s2_ref, twr_ref, twi_ref,
                  fc4_ref, fc2_ref, c1_ref, s1_ref, twtr_ref, twti_ref, o_ref, wr_ref, wi_ref):
    l = u_ref.shape[1]
    n2 = FFT_N2
    n1 = 2 * l // n2
    half = n1 // 2

    _fft_fwd_outer(u_ref.at[0], u_ref.at[1], wr_ref, wi_ref, fa4_ref, fa2_ref, n1, n2, half)

    fc4 = fc4_ref[...]
    fc2 = fc2_ref[...]

    def slab(k, carry):
        rows = pl.ds(pl.multiple_of(k * n2, n2), n2)
        g4, g2 = _twiddled_inner(c2_ref, s2_ref, twr_ref, twi_ref, k)
        xre, xim = _cmm(g4, g2, wr_ref[rows, :], wi_ref[rows, :])
        kre = kf_ref[0, 0, rows, :]
        kim = kf_ref[0, 1, rows, :]
        yre = xre * kre - xim * kim
        yim = xre * kim + xim * kre
        are, aim = _cmm(fc4, fc2, yre, yim, conj=True)
        wr_ref[rows, :] = are
        wi_ref[rows, :] = aim
        return carry

    lax.fori_loop(0, n1, slab, 0)

    bias = bias_ref[0]

    def outer(i, carry):
        tre = twtr_ref[pl.ds(i, 1), :]
        tim = twti_ref[pl.ds(i, 1), :]
        c1 = c1_ref[...]
        s1 = s1_ref[...]
        q4, q2 = _stacks(c1 * tre - s1 * tim, -(c1 * tim + s1 * tre))
        yre, yim = _cmm(q4, q2, wr_ref[pl.ds(i, n1, stride=n2), :], wi_ref[pl.ds(i, n1, stride=n2), :])
        rows = pl.ds(i, half, stride=n2)
        u0 = u_ref[0, rows, :]
        u1 = u_ref[1, rows, :]
        o_ref[0, rows, :] = x_ref[0, rows, :] * (yre + u0 * bias)
        o_ref[1, rows, :] = x_ref[1, rows, :] * (yim + u1 * bias)
        return carry

    lax.fori_loop(0, n2, outer, 0)


def _filter_kernel(w1_ref, b1_ref, wm_ref, bm_ref, fr_ref, wof_ref, wob_ref, dec_ref, bands_ref,
                   fa4_ref, fa2_ref, c2_ref, s2_ref, twr_ref, twi_ref, o_ref):
    n = o_ref.shape[2]
    l = n // 2
    n2 = FFT_N2
    n1 = n // n2
    chunk = min(512, n)
    nchunk = n // chunk
    bands = bands_ref[...]
    lane = lax.broadcasted_iota(jnp.int32, (1, LANES), 1)
    w1h, w1l = _split(w1_ref[...])
    wm = [_split(wm_ref[i]) for i in range(2)]
    wof = _split(wof_ref[...])
    wob = _split(wob_ref[...])
    dec_f = jnp.abs(dec_ref[0, 0:1, :])
    dec_b = jnp.abs(dec_ref[0, 1:2, :])

    def gen(c, asum):
        base = pl.multiple_of(c * chunk, chunk)
        m = base + lax.broadcasted_iota(jnp.int32, (chunk, 1), 0)
        is_f = m < l
        pos = jnp.where(is_f, m, n - m).astype(F32)
        t = pos / max(l - 1, 1)
        ang = (2.0 * math.pi / l) * pos * bands
        feat = jnp.where(lane == 0, t,
                         jnp.where(lane <= POS_BANDS, jnp.cos(ang),
                                   jnp.where(lane <= 2 * POS_BANDS, -jnp.sin(ang), 0.0)))
        z = jnp.sin(fr_ref[0:1, :] * (_dot_presplit(feat, w1h, w1l) + b1_ref[...]))
        for i in range(2):
            z = jnp.sin(fr_ref[i + 1:i + 2, :] * (_dot_presplit(z, *wm[i]) + bm_ref[i:i + 1, :]))
        hf = _dot_presplit(z, *wof) * jnp.exp(-t * dec_f)
        hb = _dot_presplit(z, *wob) * jnp.exp(-t * dec_b)
        k2 = jnp.where(is_f, hf, jnp.where(m == l, 0.0, hb))
        o_ref[0, 0, pl.ds(base, chunk), :] = k2
        return asum + jnp.sum(jnp.abs(k2), axis=0, keepdims=True)

    asum = lax.fori_loop(0, nchunk, gen, jnp.zeros((1, LANES), F32))
    inv = 1.0 / asum

    re_ref = o_ref.at[0, 0]
    im_ref = o_ref.at[0, 1]
    _fft_fwd_outer(re_ref, None, re_ref, im_ref, fa4_ref, fa2_ref, n1, n2, n1, scale=inv)

    def slab(k, carry):
        rows = pl.ds(pl.multiple_of(k * n2, n2), n2)
        g4, g2 = _twiddled_inner(c2_ref, s2_ref, twr_ref, twi_ref, k)
        xre, xim = _cmm(g4, g2, re_ref[rows, :], im_ref[rows, :])
        re_ref[rows, :] = xre
        im_ref[rows, :] = xim
        return carry

    lax.fori_loop(0, n1, slab, 0)


def _dft_tables(l):
    n = 2 * l
    n2 = FFT_N2
    n1 = n // n2
    half = n1 // 2

    def cs(rows, cols, period):
        ang = 2.0 * np.pi * ((np.arange(rows)[:, None] * np.arange(cols)[None, :]) % period) / period
        return np.cos(ang), -np.sin(ang)

    def split_np(m):
        m32 = jnp.asarray(m, F32)
        hi = m32.astype(BF16)
        lo = (m32 - hi.astype(F32)).astype(BF16)
        return hi, lo

    def stacks_np(c, s):
        ch, cl = split_np(c)
        sh, sl = split_np(s)
        return jnp.concatenate([ch, cl, sh, sl], axis=0), jnp.concatenate([ch, sh], axis=0)

    c1, s1 = cs(n1, n1, n1)
    c2, s2 = cs(n2, n2, n2)
    tw_c, tw_s = cs(n1, n2, n)
    tabs = {}
    tabs["fa_half"] = stacks_np(c1[:, :half], s1[:, :half])
    tabs["fa_full"] = stacks_np(c1, s1)
    tabs["fc"] = stacks_np(c2, s2)
    tabs["c2"] = jnp.asarray(c2, F32)
    tabs["s2"] = jnp.asarray(s2, F32)
    tabs["twr"] = jnp.asarray(tw_c, F32)
    tabs["twi"] = jnp.asarray(tw_s, F32)
    tabs["c1h"] = jnp.asarray(c1[:half] / n, F32)
    tabs["s1h"] = jnp.asarray(s1[:half] / n, F32)
    tabs["twtr"] = jnp.asarray(tw_c.T, F32)
    tabs["twti"] = jnp.asarray(tw_s.T, F32)
    return tabs


def _const_spec(a, nd_grid):
    shape = a.shape
    if nd_grid == 2:
        return pl.BlockSpec(shape, lambda i, j: (0,) * len(shape))
    return pl.BlockSpec(shape, lambda i: (0,) * len(shape))


def _hyena_filters(l, p, tabs):
    (f_w1, f_b1, f_w_mid, f_b_mid, f_freq, f_w_out, f_decay) = p
    n = 2 * l
    e = WIDTH
    nblk = e // LANES
    w1p = jnp.zeros((LANES, FILTER_HIDDEN), F32).at[:f_w1.shape[0]].set(f_w1)
    bands = np.linspace(1e-4, POS_BANDS - 1, POS_BANDS, dtype=np.float32)
    bl = np.zeros((1, LANES), np.float32)
    bl[0, 1:1 + POS_BANDS] = bands
    bl[0, 1 + POS_BANDS:1 + 2 * POS_BANDS] = bands
    consts = [tabs["fa_full"][0], tabs["fa_full"][1], tabs["c2"], tabs["s2"], tabs["twr"], tabs["twi"]]
    return pl.pallas_call(
        _filter_kernel,
        grid=(2, nblk),
        in_specs=[
            _const_spec(w1p, 2),
            pl.BlockSpec((1, FILTER_HIDDEN), lambda o, j: (0, 0)),
            pl.BlockSpec((2, FILTER_HIDDEN, FILTER_HIDDEN), lambda o, j: (0, 0, 0)),
            pl.BlockSpec((2, FILTER_HIDDEN), lambda o, j: (0, 0)),
            pl.BlockSpec((3, FILTER_HIDDEN), lambda o, j: (0, 0)),
            pl.BlockSpec((FILTER_HIDDEN, LANES), lambda o, j: (0, o * 2 * nblk + j)),
            pl.BlockSpec((FILTER_HIDDEN, LANES), lambda o, j: (0, o * 2 * nblk + nblk + j)),
            pl.BlockSpec((1, 2, LANES), lambda o, j: (o, 0, j)),
            pl.BlockSpec((1, LANES), lambda o, j: (0, 0)),
        ] + [_const_spec(c, 2) for c in consts],
        out_specs=pl.BlockSpec((1, 2, n, LANES), lambda o, j: (o, 0, 0, j)),
        out_shape=jax.ShapeDtypeStruct((2, 2, n, e), F32),
        compiler_params=_cparams("parallel", "arbitrary"),
        name="hyena_filter",
    )(w1p, f_b1.reshape(1, FILTER_HIDDEN), f_w_mid, f_b_mid, f_freq, f_w_out, f_w_out,
      f_decay.reshape(2, 2, e), jnp.asarray(bl), *consts)


def _long_conv(u, xg, kf, order, bias, tabs, col_u, col_x):
    b, l, _ = u.shape
    e = WIDTH
    n = 2 * l
    nblk = e // LANES
    consts = [tabs["fa_half"][0], tabs["fa_half"][1], tabs["c2"], tabs["s2"], tabs["twr"], tabs["twi"],
              tabs["fc"][0], tabs["fc"][1], tabs["c1h"], tabs["s1h"], tabs["twtr"], tabs["twti"]]
    return pl.pallas_call(
        _lconv_kernel,
        grid=(nblk, b // 2),
        in_specs=[
            pl.BlockSpec((2, l, LANES), lambda j, pi: (pi, 0, col_u * nblk + j)),
            pl.BlockSpec((2, l, LANES), lambda j, pi: (pi, 0, col_x * nblk + j)),
            pl.BlockSpec((1, 2, n, LANES), lambda j, pi: (order, 0, 0, j)),
            pl.BlockSpec((1, 1, LANES), lambda j, pi: (order, 0, j)),
        ] + [_const_spec(c, 2) for c in consts],
        out_specs=pl.BlockSpec((2, l, LANES), lambda j, pi: (pi, 0, j)),
        out_shape=jax.ShapeDtypeStruct((b, l, e), F32),
        scratch_shapes=[pltpu.VMEM((n, LANES), F32), pltpu.VMEM((n, LANES), F32)],
        compiler_params=_cparams("parallel", "arbitrary"),
        name="hyena_long_conv",
    )(u, xg, kf, bias.reshape(2, 1, e), *consts)


def _gated_out_kernel(z_ref, g_ref, x_ref, mod_ref, w_ref, o_ref):
    zz = (z_ref[0] * _silu(g_ref[0])).astype(BF16)
    o_ref[0] = x_ref[0] + mod_ref[0, 2:3, :] * _dot(zz, w_ref[...])


def _gated_out(z, g_arr, g_col, x, mod, w_out, name):
    b, l, d = x.shape
    e = WIDTH
    tl = 256
    tile = lambda col: pl.BlockSpec((1, tl, e), lambda bi, ti: (bi, ti, col))
    return pl.pallas_call(
        _gated_out_kernel,
        grid=(b, l // tl),
        in_specs=[tile(0), tile(g_col), tile(0),
                  pl.BlockSpec((1, 3, d), lambda bi, ti: (bi, 0, 0)),
                  pl.BlockSpec((e, d), lambda bi, ti: (0, 0))],
        out_specs=tile(0),
        out_shape=jax.ShapeDtypeStruct((b, l, d), F32),
        compiler_params=_cparams("parallel", "arbitrary"),
        name=name,
    )(z, g_arr, x, mod, w_out.astype(BF16))


def _hyena_layer(x, mod, ng, p, tabs):
    (w_in, short_w, short_b, f_w1, f_b1, f_w_mid, f_b_mid, f_freq, f_w_out, f_decay, bias, w_out) = p
    l = x.shape[1]
    proj = _norm_proj(x, mod, ng, w_in, "hyena_proj")
    u3 = _short_conv(proj, short_w, short_b)
    kf = _hyena_filters(l, (f_w1, f_b1, f_w_mid, f_b_mid, f_freq, f_w_out, f_decay), tabs)
    z1 = _long_conv(u3, u3, kf, 0, bias, tabs, col_u=2, col_x=0)
    z2 = _long_conv(z1, u3, kf, 1, bias, tabs, col_u=0, col_x=1)
    return _gated_out(z2, proj, 3, x, mod, w_out, "hyena_out")


def _cf_proj_kernel(x_ref, mod_ref, ng_ref, w_ref, u_o, sg_o):
    e = WIDTH
    h = _prenorm(x_ref[0], ng_ref[...], mod_ref[0, 0:1, :], mod_ref[0, 1:2, :]).astype(BF16)
    a = _dot(h, w_ref[:, 0:e])
    bgate = _dot(h, w_ref[:, e:2 * e])
    g = _dot(h, w_ref[:, 2 * e:3 * e])
    u_o[0] = a * _sigmoid(bgate)
    sg_o[0] = _silu(g)


def _cf_out_kernel(u_ref, up_ref, un_ref, sg_ref, x_ref, mod_ref, dw_ref, db_ref, lg_ref, lb_ref, w_ref,
                   o_ref, ext_ref):
    t = pl.program_id(1)
    nt = pl.num_programs(1)
    tl = u_ref.shape[1]
    hh = CONV_HALO
    ext_ref[0:hh, :] = jnp.where(t > 0, up_ref[0], 0.0)
    ext_ref[hh:hh + tl, :] = u_ref[0]
    ext_ref[hh + tl:hh + tl + hh, :] = jnp.where(t < nt - 1, un_ref[0], 0.0)
    off = hh - (CONV_WIDTH - 1) // 2
    acc = jnp.zeros((tl, WIDTH), F32) + db_ref[...]
    for k in range(CONV_WIDTH):
        acc = acc + ext_ref[off + k:off + k + tl, :] * dw_ref[k:k + 1, :]
    mean = jnp.mean(acc, axis=-1, keepdims=True)
    cen = acc - mean
    var = jnp.mean(cen * cen, axis=-1, keepdims=True)
    y = cen * lax.rsqrt(var + LN_EPS) * lg_ref[...] + lb_ref[...]
    zz = (_silu(y) * sg_ref[0]).astype(BF16)
    o_ref[0] = x_ref[0] + mod_ref[0, 2:3, :] * _dot(zz, w_ref[...])


def _conformer_layer(x, mod, ng, p):
    (w_in, dw_w, dw_b, ln_g, ln_b, w_out) = p
    b, l, d = x.shape
    e = WIDTH
    tl = 256
    tile = pl.BlockSpec((1, tl, e), lambda bi, ti: (bi, ti, 0))
    modspec = pl.BlockSpec((1, 3, d), lambda bi, ti: (bi, 0, 0))
    vec = pl.BlockSpec((1, e), lambda bi, ti: (0, 0))
    u, sg = pl.pallas_call(
        _cf_proj_kernel,
        grid=(b, l // tl),
        in_specs=[tile, modspec, vec, pl.BlockSpec((d, 3 * e), lambda bi, ti: (0, 0))],
        out_specs=[tile, tile],
        out_shape=[jax.ShapeDtypeStruct((b, l, e), F32)] * 2,
        compiler_params=_cparams("parallel", "arbitrary"),
        name="conformer_proj",
    )(x, mod, ng.reshape(1, d), w_in.astype(BF16))

    hh = CONV_HALO
    hb = tl // hh
    lb = l // hh
    return pl.pallas_call(
        _cf_out_kernel,
        grid=(b, l // tl),
        in_specs=[
            tile,
            pl.BlockSpec((1, hh, e), lambda bi, ti: (bi, jnp.maximum(ti * hb - 1, 0), 0)),
            pl.BlockSpec((1, hh, e), lambda bi, ti: (bi, jnp.minimum((ti + 1) * hb, lb - 1), 0)),
            tile, tile, modspec,
            pl.BlockSpec((CONV_WIDTH, e), lambda bi, ti: (0, 0)),
            vec, vec, vec,
            pl.BlockSpec((e, d), lambda bi, ti: (0, 0)),
        ],
        out_specs=tile,
        out_shape=jax.ShapeDtypeStruct((b, l, d), F32),
        scratch_shapes=[pltpu.VMEM((tl + 2 * hh, e), F32)],
        compiler_params=_cparams("parallel", "arbitrary"),
        name="conformer_out",
    )(u, u, u, sg, x, mod, dw_w, dw_b.reshape(1, e), ln_g.reshape(1, e), ln_b.reshape(1, e),
      w_out.astype(BF16))


def _final_norm_kernel(x_ref, g_ref, o_ref):
    x = x_ref[0]
    ms = jnp.mean(x * x, axis=-1, keepdims=True)
    o_ref[0] = x * lax.rsqrt(ms + RMS_EPS) * g_ref[...]


def _final_norm(x, g):
    b, l, d = x.shape
    tl = 512
    tile = pl.BlockSpec((1, tl, d), lambda bi, ti: (bi, ti, 0))
    return pl.pallas_call(
        _final_norm_kernel,
        grid=(b, l // tl),
        in_specs=[tile, pl.BlockSpec((1, d), lambda bi, ti: (0, 0))],
        out_specs=tile,
        out_shape=jax.ShapeDtypeStruct((b, l, d), F32),
        compiler_params=_cparams("parallel", "arbitrary"),
        name="final_norm",
    )(x, g.reshape(1, d))


def _seg_constants():
    seg = np.arange(SEG_GROUP) // HEAD
    bd = (seg[:, None] == seg[None, :]).astype(np.float32)
    lane_in_head = np.arange(WIDTH) % HEAD
    delta = (np.arange(HEAD)[:, None] == lane_in_head[None, :]).astype(np.float32)
    return {"bd": jnp.asarray(bd, BF16), "delta": jnp.asarray(delta, F32)}


def _trunk(x, mod, norm_g, rw, hy, cf, final_g, consts, scan_tc=128):
    tabs = None
    for i in range(DEPTH):
        kind, j = i % 3, i // 3
        m = mod[i]
        if kind == 0:
            x = _rwkv_layer(x, m, norm_g[i], [p[j] for p in rw], consts, scan_tc)
        elif kind == 1:
            if tabs is None:
                tabs = _dft_tables(x.shape[1])
            x = _hyena_layer(x, m, norm_g[i], [p[j] for p in hy], tabs)
        else:
            x = _conformer_layer(x, m, norm_g[i], [p[j] for p in cf])
    return _final_norm(x, final_g)


def kernel(x_prompt, x_sample, c_prompt, c_sample, norm_g, mod_w, mod_b, rw_w_in, rw_mu, rw_w0, rw_w2, rw_a0, rw_a2, rw_k_k, rw_k_a, rw_r_k, rw_gn_w, rw_gn_b, rw_w_out, hy_w_in, hy_short_w, hy_short_b, hy_f_w1, hy_f_b1, hy_f_w_mid, hy_f_b_mid, hy_f_freq, hy_f_w_out, hy_f_decay, hy_bias, hy_w_out, cf_w_in, cf_dw_w, cf_dw_b, cf_ln_g, cf_ln_b, cf_w_out, final_g):
    rw = (rw_w_in, rw_mu, rw_w0, rw_w2, rw_a0, rw_a2, rw_k_k, rw_k_a, rw_r_k, rw_gn_w, rw_gn_b, rw_w_out)
    hy = (hy_w_in, hy_short_w, hy_short_b, hy_f_w1, hy_f_b1, hy_f_w_mid, hy_f_b_mid, hy_f_freq, hy_f_w_out,
          hy_f_decay, hy_bias, hy_w_out)
    cf = (cf_w_in, cf_dw_w, cf_dw_b, cf_ln_g, cf_ln_b, cf_w_out)
    consts = _seg_constants()
    bp = x_prompt.shape[0]
    mod = _modulation(jnp.concatenate([c_prompt, c_sample], axis=0), mod_w, mod_b)
    y_prompt = _trunk(x_prompt, mod[:, :bp], norm_g, rw, hy, cf, final_g, consts)
    y_sample = _trunk(x_sample, mod[:, bp:], norm_g, rw, hy, cf, final_g, consts)
    return (y_prompt, y_sample)
```

```python
import functools
import math

import numpy as np
import jax
import jax.numpy as jnp
from jax import lax
from jax.experimental import pallas as pl
from jax.experimental.pallas import tpu as pltpu

F32 = jnp.float32
BF16 = jnp.bfloat16

D_MODEL = 1024
WIDTH = 1024
DEPTH = 4
HEADS = 16
HEAD = 64
LORA = 64
RMS_EPS = 1e-6
LN_EPS = 1e-5
GN_EPS = 64e-5
POS_BANDS = 16
FILTER_HIDDEN = 64
CONV_WIDTH = 31
CONV_HALO = 16
LANES = 128
SEG_GROUP = 256
FFT_N2 = 64
FFT_UNROLL = 8
VMEM_LIMIT = 56 * 1024 * 1024


def _cparams(*sem):
    return pltpu.CompilerParams(dimension_semantics=sem, vmem_limit_bytes=VMEM_LIMIT)


def _dot(a, b):
    return jnp.dot(a, b, preferred_element_type=F32)


def _split(x):
    hi = x.astype(BF16)
    lo = (x - hi.astype(F32)).astype(BF16)
    return hi, lo


def _dot_x3(a, b):
    ah, al = _split(a)
    bh, bl = _split(b)
    return _dot(ah, bh) + _dot(ah, bl) + _dot(al, bh)


def _dot_presplit(a, bh, bl):
    ah, al = _split(a)
    return _dot(ah, bh) + _dot(ah, bl) + _dot(al, bh)


def _seg_bcast(t, bd):
    outs = []
    for g in range(t.shape[1] // SEG_GROUP):
        th, tl = _split(t[:, g * SEG_GROUP:(g + 1) * SEG_GROUP])
        outs.append(_dot(th, bd) + _dot(tl, bd))
    return jnp.concatenate(outs, axis=1)


def _sigmoid(x):
    return 1.0 / (1.0 + jnp.exp(-x))


def _silu(x):
    return x * _sigmoid(x)


def _prenorm(x, ng, shift, scale):
    ms = jnp.mean(x * x, axis=-1, keepdims=True)
    return (x * lax.rsqrt(ms + RMS_EPS) * ng) * (1.0 + scale) + shift


def _mod_kernel(c_ref, w_ref, b_ref, o_ref):
    o_ref[0] = _dot_x3(_silu(c_ref[...]), w_ref[0]) + b_ref[0]


def _modulation(c_all, mod_w, mod_b):
    bc = c_all.shape[0]
    out = pl.pallas_call(
        _mod_kernel,
        grid=(DEPTH, 3),
        in_specs=[
            pl.BlockSpec((bc, D_MODEL), lambda i, j: (0, 0)),
            pl.BlockSpec((1, D_MODEL, D_MODEL), lambda i, j: (i, 0, j)),
            pl.BlockSpec((1, 1, D_MODEL), lambda i, j: (i, 0, j)),
        ],
        out_specs=pl.BlockSpec((1, bc, D_MODEL), lambda i, j: (i, 0, j)),
        out_shape=jax.ShapeDtypeStruct((DEPTH, bc, 3 * D_MODEL), F32),
        compiler_params=_cparams("arbitrary", "arbitrary"),
        name="modulation",
    )(c_all, mod_w, mod_b.reshape(DEPTH, 1, 3 * D_MODEL))
    return out.reshape(DEPTH, bc, 3, D_MODEL)


RW_TL = 128
SCAN_BATCH = 2


def _rw_proj_kernel(x_ref, xp_ref, xn_ref, mod_ref, ng_ref, mu_ref, win_ref,
                    w0_ref, w2h_ref, w2l_ref, a0_ref, a2h_ref, a2l_ref,
                    kk_ref, ka_ref, rk_ref, bd_ref,
                    r_o, v_o, nkk_o, g_o, bv_o, w0_o, k0_o, b0_o, w1_o, k1_o, b1_o):
    t = pl.program_id(1)
    nt = pl.num_programs(1)
    tl = x_ref.shape[1]
    e = WIDTH
    shift = mod_ref[0, 0:1, :]
    scale = mod_ref[0, 1:2, :]
    ng = ng_ref[...]
    bd = bd_ref[...]

    h = _prenorm(x_ref[0], ng, shift, scale)
    hp = _prenorm(xp_ref[0], ng, shift, scale)[7:8, :]
    hn = _prenorm(xn_ref[0], ng, shift, scale)[0:1, :]
    hp = jnp.where(t > 0, hp, 0.0)
    hn = jnp.where(t < nt - 1, hn, 0.0)
    rows = lax.broadcasted_iota(jnp.int32, (tl, 1), 0)
    h_up = jnp.where(rows == 0, hp, pltpu.roll(h, 1, 0))
    h_dn = jnp.where(rows == tl - 1, hn, pltpu.roll(h, tl - 1, 0))
    xx = 0.5 * (h_up + h_dn) - h

    def proj(i, c0, c1):
        inp = (h + xx * mu_ref[i:i + 1, :]).astype(BF16)
        return _dot(inp, win_ref[:, c0:c1])

    r = proj(0, 0, e)
    k = proj(1, e, 2 * e)
    v = proj(2, 2 * e, 3 * e)
    g = proj(3, 3 * e, 4 * e)
    wl = proj(4, 4 * e, 4 * e + 2 * LORA)
    al = proj(5, 4 * e + 2 * LORA, 4 * e + 4 * LORA)
    twl = jnp.tanh(wl)

    kk0 = k * kk_ref[...]
    nrm = jnp.sqrt(_seg_bcast(kk0 * kk0, bd))
    kk = kk0 / jnp.maximum(nrm, 1e-12)
    ka = ka_ref[...]

    r_o[0] = r
    v_o[0] = v
    g_o[0] = g
    nkk_o[0] = -kk

    ksum = jnp.zeros_like(k)
    for d, (w_o, k_o, b_o) in enumerate(((w0_o, k0_o, b0_o), (w1_o, k1_o, b1_o))):
        w_raw = w0_ref[d:d + 1, :] + _dot_presplit(twl, w2h_ref[d], w2l_ref[d])
        z = -w_raw
        softplus = jnp.maximum(z, 0.0) + jnp.log(1.0 + jnp.exp(-jnp.abs(z)))
        w_log = -softplus - 0.5
        w_o[0] = jnp.exp(-jnp.exp(w_log))
        a = _sigmoid(a0_ref[d:d + 1, :] + _dot_presplit(al, a2h_ref[d], a2l_ref[d]))
        k_d = k * (1.0 + (a - 1.0) * ka)
        k_o[0] = k_d
        b_o[0] = kk * a
        ksum = ksum + k_d

    bonus = _seg_bcast(r * ksum * rk_ref[...], bd)
    bv_o[0] = bonus * v


def _rw_scan_kernel(rf, vf, af, wf, kf, bf, rb, vb, ab, wb, kb, bb, bd_ref, dl_ref,
                    yf_o, yb_o, s_ref):
    tb = pl.program_id(1)
    tc = rf.shape[1]

    @pl.when(tb == 0)
    def _():
        s_ref[...] = jnp.zeros_like(s_ref)

    bd = bd_ref[...]
    delta = dl_ref[...]
    delta_bf = delta.astype(BF16)

    def seg_bcast_stacked(parts):
        flat = [piece for p in parts for piece in p]
        outs = [[] for _ in parts]
        for g in range(WIDTH // SEG_GROUP):
            sl = slice(g * SEG_GROUP, (g + 1) * SEG_GROUP)
            res = _dot(jnp.concatenate([piece[:, sl] for piece in flat], axis=0), bd)
            at = 0
            for q, p in enumerate(parts):
                acc = res[at * HEAD:(at + 1) * HEAD]
                for extra in range(1, len(p)):
                    acc = acc + res[(at + extra) * HEAD:(at + extra + 1) * HEAD]
                at += len(p)
                outs[q].append(acc)
        return [jnp.concatenate(o, axis=1) for o in outs]

    nb = rf.shape[0]
    chains = [(q, rev, refs, y_o)
              for q in range(nb)
              for rev, refs, y_o in ((False, (rf, vf, af, wf, kf, bf), yf_o), (True, (rb, vb, ab, wb, kb, bb), yb_o))]

    def body(i, carry):
        rows = []
        for q, rev, refs, _ in chains:
            idx = tc - 1 - i if rev else i
            rows.append([ref[q, pl.ds(idx, 1), :] for ref in refs])
        states = [s_ref[c] for c in range(len(chains))]
        parts = []
        for s, (_, v_row, a_row, _, _, _) in zip(states, rows):
            parts += [_split(s * a_row), (delta_bf * v_row.astype(BF16),)]
        red = seg_bcast_stacked(parts)
        new_states = []
        for c, (s, (_, _, _, w_row, k_row, b_row)) in enumerate(zip(states, rows)):
            s = s * w_row + red[2 * c] * b_row + red[2 * c + 1] * k_row
            s_ref[c] = s
            new_states.append(s)
        zs = seg_bcast_stacked([((s * r[0]).astype(BF16),) for s, r in zip(new_states, rows)])
        for z, (q, rev, _, y_o) in zip(zs, chains):
            idx = tc - 1 - i if rev else i
            y_o[q, pl.ds(idx, 1), :] = jnp.sum(z * delta, axis=0, keepdims=True)
        return carry

    lax.fori_loop(0, tc, body, 0)


def _rw_out_kernel(yf_ref, yb_ref, bv_ref, g_ref, x_ref, mod_ref, gw_ref, gb_ref, wout_ref, bd_ref, o_ref):
    bd = bd_ref[...]
    y = yf_ref[0] + yb_ref[0]
    mean = _seg_bcast(y, bd) * (1.0 / HEAD)
    yc = y - mean
    var = _seg_bcast(yc * yc, bd) * (1.0 / HEAD)
    yn = yc * lax.rsqrt(var + GN_EPS) * gw_ref[...] + gb_ref[...]
    yo = (yn + bv_ref[0]) * _silu(g_ref[0])
    o = _dot(yo.astype(BF16), wout_ref[...])
    o_ref[0] = x_ref[0] + mod_ref[0, 2:3, :] * o


def _pad_lora(w):
    z = jnp.zeros_like(w[0])
    return jnp.stack([jnp.concatenate([w[0], z], axis=0), jnp.concatenate([z, w[1]], axis=0)], axis=0)


def _split_host(w):
    hi = w.astype(BF16)
    lo = (w - hi.astype(F32)).astype(BF16)
    return hi, lo


def _rwkv_layer(x, mod, ng, p, consts, scan_tc):
    (w_in, mu, w0, w2, a0, a2, k_k, k_a, r_k, gn_w, gn_b, w_out) = p
    b, l, d = x.shape
    e = WIDTH
    tl = RW_TL
    nt = l // tl
    bd, delta = consts["bd"], consts["delta"]
    w2h, w2l = _split_host(_pad_lora(w2))
    a2h, a2l = _split_host(_pad_lora(a2))
    ncols = w_in.shape[1]

    full = lambda shape: pl.BlockSpec(shape, lambda bi, ti: (0,) * len(shape))
    tile = pl.BlockSpec((1, tl, e), lambda bi, ti: (bi, ti, 0))
    hb = tl // 8
    lb = l // 8
    outs = pl.pallas_call(
        _rw_proj_kernel,
        grid=(b, nt),
        in_specs=[
            tile,
            pl.BlockSpec((1, 8, d), lambda bi, ti: (bi, jnp.maximum(ti * hb - 1, 0), 0)),
            pl.BlockSpec((1, 8, d), lambda bi, ti: (bi, jnp.minimum((ti + 1) * hb, lb - 1), 0)),
            pl.BlockSpec((1, 3, d), lambda bi, ti: (bi, 0, 0)),
            full((1, d)), full((6, d)), full((d, ncols)),
            full((2, e)), full((2, 2 * LORA, e)), full((2, 2 * LORA, e)),
            full((2, e)), full((2, 2 * LORA, e)), full((2, 2 * LORA, e)),
            full((1, e)), full((1, e)), full((1, e)),
            full((SEG_GROUP, SEG_GROUP)),
        ],
        out_specs=[tile] * 11,
        out_shape=[jax.ShapeDtypeStruct((b, l, e), F32)] * 11,
        compiler_params=_cparams("parallel", "arbitrary"),
        name="rwkv_proj",
    )(x, x, x, mod, ng.reshape(1, d), mu, w_in.astype(BF16),
      w0, w2h, w2l, a0, a2h, a2l,
      k_k.reshape(1, e), k_a.reshape(1, e), r_k.reshape(1, e), bd)
    r, v, nkk, g, bv, wd0, kd0, bd0, wd1, kd1, bd1 = outs

    tc = scan_tc
    ntc = l // tc
    nb = SCAN_BATCH
    fwd = pl.BlockSpec((nb, tc, e), lambda bi, ti: (bi, ti, 0))
    bwd = pl.BlockSpec((nb, tc, e), lambda bi, ti: (bi, ntc - 1 - ti, 0))
    yf, yb = pl.pallas_call(
        _rw_scan_kernel,
        grid=(b // nb, ntc),
        in_specs=[fwd] * 6 + [bwd] * 6 + [full((SEG_GROUP, SEG_GROUP)), full((HEAD, e))],
        out_specs=[fwd, bwd],
        out_shape=[jax.ShapeDtypeStruct((b, l, e), F32)] * 2,
        scratch_shapes=[pltpu.VMEM((2 * nb, HEAD, e), F32)],
        compiler_params=_cparams("parallel", "arbitrary"),
        name="rwkv_scan",
    )(r, v, nkk, wd0, kd0, bd0, r, v, nkk, wd1, kd1, bd1, bd, delta)

    tlo = 256
    tile_o = pl.BlockSpec((1, tlo, e), lambda bi, ti: (bi, ti, 0))
    return pl.pallas_call(
        _rw_out_kernel,
        grid=(b, l // tlo),
        in_specs=[tile_o] * 5 + [
            pl.BlockSpec((1, 3, d), lambda bi, ti: (bi, 0, 0)),
            full((1, e)), full((1, e)), full((e, d)), full((SEG_GROUP, SEG_GROUP)),
        ],
        out_specs=tile_o,
        out_shape=jax.ShapeDtypeStruct((b, l, d), F32),
        compiler_params=_cparams("parallel", "arbitrary"),
        name="rwkv_out",
    )(yf, yb, bv, g, x, mod, gn_w.reshape(1, e), gn_b.reshape(1, e), w_out.astype(BF16), bd)


def _norm_proj_kernel(x_ref, mod_ref, ng_ref, w_ref, o_ref):
    h = _prenorm(x_ref[0], ng_ref[...], mod_ref[0, 0:1, :], mod_ref[0, 1:2, :])
    o_ref[0] = _dot(h.astype(BF16), w_ref[...])


def _norm_proj(x, mod, ng, w, name):
    b, l, d = x.shape
    n = w.shape[1]
    tl = 256
    return pl.pallas_call(
        _norm_proj_kernel,
        grid=(b, l // tl),
        in_specs=[
            pl.BlockSpec((1, tl, d), lambda bi, ti: (bi, ti, 0)),
            pl.BlockSpec((1, 3, d), lambda bi, ti: (bi, 0, 0)),
            pl.BlockSpec((1, d), lambda bi, ti: (0, 0)),
            pl.BlockSpec((d, n), lambda bi, ti: (0, 0)),
        ],
        out_specs=pl.BlockSpec((1, tl, n), lambda bi, ti: (bi, ti, 0)),
        out_shape=jax.ShapeDtypeStruct((b, l, n), F32),
        compiler_params=_cparams("parallel", "arbitrary"),
        name=name,
    )(x, mod, ng.reshape(1, d), w.astype(BF16))


def _short_conv_kernel(u_ref, w_ref, b_ref, o_ref):
    u = u_ref[0]
    l = u.shape[0]
    rows = lax.broadcasted_iota(jnp.int32, (l, 1), 0)
    up = jnp.where(rows == 0, 0.0, pltpu.roll(u, 1, 0))
    dn = jnp.where(rows == l - 1, 0.0, pltpu.roll(u, l - 1, 0))
    o_ref[0] = up * w_ref[0:1, :] + u * w_ref[1:2, :] + dn * w_ref[2:3, :] + b_ref[...]


def _short_conv(proj, w, bias):
    b, l, _ = proj.shape
    n = w.shape[1]
    blk = pl.BlockSpec((1, l, LANES), lambda bi, ci: (bi, 0, ci))
    return pl.pallas_call(
        _short_conv_kernel,
        grid=(b, n // LANES),
        in_specs=[blk,
                  pl.BlockSpec((3, LANES), lambda bi, ci: (0, ci)),
                  pl.BlockSpec((1, LANES), lambda bi, ci: (0, ci))],
        out_specs=blk,
        out_shape=jax.ShapeDtypeStruct((b, l, n), F32),
        compiler_params=_cparams("parallel", "arbitrary"),
        name="hyena_short_conv",
    )(proj, w, bias.reshape(1, n))


def _cmm(s4, s2, xre, xim, conj=False):
    r = s2.shape[0] // 2

    def prods(x):
        xh, xl = _split(x)
        p4 = _dot(s4, xh)
        p2 = _dot(s2, xl)
        re_x = p4[0:r] + p4[r:2 * r] + p2[0:r]
        im_x = p4[2 * r:3 * r] + p4[3 * r:4 * r] + p2[r:2 * r]
        return re_x, im_x

    rr, ir = prods(xre)
    if xim is None:
        return rr, ir
    ri, ii = prods(xim)
    if conj:
        return rr + ii, ri - ir
    return rr - ii, ir + ri


def _stacks(mre, mim):
    reh, rel = _split(mre)
    imh, iml = _split(mim)
    return jnp.concatenate([reh, rel, imh, iml], axis=0), jnp.concatenate([reh, imh], axis=0)


def _fft_fwd_outer(src_re, src_im, dst_re, dst_im, fa4_ref, fa2_ref, n1, n2, k1, scale=None):
    fa4 = fa4_ref[...]
    fa2 = fa2_ref[...]

    def body(i, carry):
        xre = src_re[pl.ds(i, k1, stride=n2), :]
        xim = None if src_im is None else src_im[pl.ds(i, k1, stride=n2), :]
        if scale is not None:
            xre = xre * scale
        are, aim = _cmm(fa4, fa2, xre, xim)
        dst_re[pl.ds(i, n1, stride=n2), :] = are
        dst_im[pl.ds(i, n1, stride=n2), :] = aim
        return carry

    lax.fori_loop(0, n2, body, 0, unroll=FFT_UNROLL)


def _twiddled_inner(c2_ref, s2_ref, twr_ref, twi_ref, k):
    tre = twr_ref[pl.ds(k, 1), :]
    tim = twi_ref[pl.ds(k, 1), :]
    c2 = c2_ref[...]
    s2 = s2_ref[...]
    return _stacks(c2 * tre - s2 * tim, c2 * tim + s2 * tre)


def _lconv_kernel(u_ref, x_ref, kf_ref, bias_ref, fa4_ref, fa2_ref, c2_ref, s2_ref, twr_ref, twi_ref,
                  fc4_ref, fc2_ref, c1_ref, s1_ref, twtr_ref, twti_ref, o_ref, wr_ref, wi_ref):
    l = u_ref.shape[1]
    n2 = FFT_N2
    n1 = 2 * l // n2
    half = n1 // 2

    _fft_fwd_outer(u_ref.at[0], u_ref.at[1], wr_ref, wi_ref, fa4_ref, fa2_ref, n1, n2, half)

    fc4 = fc4_ref[...]
    fc2 = fc2_ref[...]

    def slab(k, carry):
        rows = pl.ds(pl.multiple_of(k * n2, n2), n2)
        g4, g2 = _twiddled_inner(c2_ref, s2_ref, twr_ref, twi_ref, k)
        xre, xim = _cmm(g4, g2, wr_ref[rows, :], wi_ref[rows, :])
        kre = kf_ref[0, 0, rows, :]
        kim = kf_ref[0, 1, rows, :]
        yre = xre * kre - xim * kim
        yim = xre * kim + xim * kre
        are, aim = _cmm(fc4, fc2, yre, yim, conj=True)
        wr_ref[rows, :] = are
        wi_ref[rows, :] = aim
        return carry

    lax.fori_loop(0, n1, slab, 0, unroll=FFT_UNROLL)

    bias = bias_ref[0]

    def outer(i, carry):
        tre = twtr_ref[pl.ds(i, 1), :]
        tim = twti_ref[pl.ds(i, 1), :]
        c1 = c1_ref[...]
        s1 = s1_ref[...]
        q4, q2 = _stacks(c1 * tre - s1 * tim, -(c1 * tim + s1 * tre))
        yre, yim = _cmm(q4, q2, wr_ref[pl.ds(i, n1, stride=n2), :], wi_ref[pl.ds(i, n1, stride=n2), :])
        rows = pl.ds(i, half, stride=n2)
        u0 = u_ref[0, rows, :]
        u1 = u_ref[1, rows, :]
        o_ref[0, rows, :] = x_ref[0, rows, :] * (yre + u0 * bias)
        o_ref[1, rows, :] = x_ref[1, rows, :] * (yim + u1 * bias)
        return carry

    lax.fori_loop(0, n2, outer, 0, unroll=FFT_UNROLL)


FILTER_CHUNK = 512


def _filter_pos(base, chunk, l):
    m = base + lax.broadcasted_iota(jnp.int32, (chunk, 1), 0)
    is_f = m < l
    pos = jnp.where(is_f, m, 2 * l - m).astype(F32)
    return m, is_f, pos


def _filter_mlp_kernel(l, w1_ref, b1_ref, wm_ref, bm_ref, fr_ref, bands_ref, z_ref):
    chunk = z_ref.shape[0]
    _, _, pos = _filter_pos(pl.program_id(0) * chunk, chunk, l)
    lane = lax.broadcasted_iota(jnp.int32, (1, LANES), 1)
    t = pos / max(l - 1, 1)
    ang = (2.0 * math.pi / l) * pos * bands_ref[...]
    feat = jnp.where(lane == 0, t,
                     jnp.where(lane <= POS_BANDS, jnp.cos(ang),
                               jnp.where(lane <= 2 * POS_BANDS, -jnp.sin(ang), 0.0)))
    z = jnp.sin(fr_ref[0:1, :] * (_dot_x3(feat, w1_ref[...]) + b1_ref[...]))
    for i in range(2):
        z = jnp.sin(fr_ref[i + 1:i + 2, :] * (_dot_x3(z, wm_ref[i]) + bm_ref[i:i + 1, :]))
    z_ref[...] = z


def _filter_kernel(z_ref, wof_ref, wob_ref, dec_ref,
                   fa4_ref, fa2_ref, c2_ref, s2_ref, twr_ref, twi_ref, o_ref):
    n = o_ref.shape[2]
    l = n // 2
    n2 = FFT_N2
    n1 = n // n2
    chunk = min(FILTER_CHUNK, n)
    nchunk = n // chunk
    wof = _split(wof_ref[...])
    wob = _split(wob_ref[...])
    dec_f = jnp.abs(dec_ref[0, 0:1, :])
    dec_b = jnp.abs(dec_ref[0, 1:2, :])

    def gen(c, asum):
        base = pl.multiple_of(c * chunk, chunk)
        m, is_f, pos = _filter_pos(base, chunk, l)
        t = pos / max(l - 1, 1)
        z = z_ref[pl.ds(base, chunk), :]
        hf = _dot_presplit(z, *wof) * jnp.exp(-t * dec_f)
        hb = _dot_presplit(z, *wob) * jnp.exp(-t * dec_b)
        k2 = jnp.where(is_f, hf, jnp.where(m == l, 0.0, hb))
        o_ref[0, 0, pl.ds(base, chunk), :] = k2
        return asum + jnp.sum(jnp.abs(k2), axis=0, keepdims=True)

    asum = lax.fori_loop(0, nchunk, gen, jnp.zeros((1, LANES), F32))
    inv = 1.0 / asum

    re_ref = o_ref.at[0, 0]
    im_ref = o_ref.at[0, 1]
    _fft_fwd_outer(re_ref, None, re_ref, im_ref, fa4_ref, fa2_ref, n1, n2, n1, scale=inv)

    def slab(k, carry):
        rows = pl.ds(pl.multiple_of(k * n2, n2), n2)
        g4, g2 = _twiddled_inner(c2_ref, s2_ref, twr_ref, twi_ref, k)
        xre, xim = _cmm(g4, g2, re_ref[rows, :], im_ref[rows, :])
        re_ref[rows, :] = xre
        im_ref[rows, :] = xim
        return carry

    lax.fori_loop(0, n1, slab, 0, unroll=FFT_UNROLL)


def _dft_tables(l):
    n = 2 * l
    n2 = FFT_N2
    n1 = n // n2
    half = n1 // 2

    def cs(rows, cols, period):
        ang = 2.0 * np.pi * ((np.arange(rows)[:, None] * np.arange(cols)[None, :]) % period) / period
        return np.cos(ang), -np.sin(ang)

    def split_np(m):
        m32 = jnp.asarray(m, F32)
        hi = m32.astype(BF16)
        lo = (m32 - hi.astype(F32)).astype(BF16)
        return hi, lo

    def stacks_np(c, s):
        ch, cl = split_np(c)
        sh, sl = split_np(s)
        return jnp.concatenate([ch, cl, sh, sl], axis=0), jnp.concatenate([ch, sh], axis=0)

    c1, s1 = cs(n1, n1, n1)
    c2, s2 = cs(n2, n2, n2)
    tw_c, tw_s = cs(n1, n2, n)
    tabs = {}
    tabs["fa_half"] = stacks_np(c1[:, :half], s1[:, :half])
    tabs["fa_full"] = stacks_np(c1, s1)
    tabs["fc"] = stacks_np(c2, s2)
    tabs["c2"] = jnp.asarray(c2, F32)
    tabs["s2"] = jnp.asarray(s2, F32)
    tabs["twr"] = jnp.asarray(tw_c, F32)
    tabs["twi"] = jnp.asarray(tw_s, F32)
    tabs["c1h"] = jnp.asarray(c1[:half] / n, F32)
    tabs["s1h"] = jnp.asarray(s1[:half] / n, F32)
    tabs["twtr"] = jnp.asarray(tw_c.T, F32)
    tabs["twti"] = jnp.asarray(tw_s.T, F32)
    return tabs


def _const_spec(a, nd_grid):
    shape = a.shape
    if nd_grid == 2:
        return pl.BlockSpec(shape, lambda i, j: (0,) * len(shape))
    return pl.BlockSpec(shape, lambda i: (0,) * len(shape))


def _hyena_filters(l, p, tabs):
    (f_w1, f_b1, f_w_mid, f_b_mid, f_freq, f_w_out, f_decay) = p
    n = 2 * l
    e = WIDTH
    nblk = e // LANES
    w1p = jnp.zeros((LANES, FILTER_HIDDEN), F32).at[:f_w1.shape[0]].set(f_w1)
    bands = np.linspace(1e-4, POS_BANDS - 1, POS_BANDS, dtype=np.float32)
    bl = np.zeros((1, LANES), np.float32)
    bl[0, 1:1 + POS_BANDS] = bands
    bl[0, 1 + POS_BANDS:1 + 2 * POS_BANDS] = bands
    consts = [tabs["fa_full"][0], tabs["fa_full"][1], tabs["c2"], tabs["s2"], tabs["twr"], tabs["twi"]]
    chunk = min(FILTER_CHUNK, n)
    mlp_in = [w1p, f_b1.reshape(1, FILTER_HIDDEN), f_w_mid, f_b_mid, f_freq, jnp.asarray(bl)]
    z = pl.pallas_call(
        functools.partial(_filter_mlp_kernel, l),
        grid=(n // chunk,),
        in_specs=[_const_spec(a, 1) for a in mlp_in],
        out_specs=pl.BlockSpec((chunk, FILTER_HIDDEN), lambda c: (c, 0)),
        out_shape=jax.ShapeDtypeStruct((n, FILTER_HIDDEN), F32),
        compiler_params=_cparams("parallel"),
        name="hyena_filter_mlp",
    )(*mlp_in)
    return pl.pallas_call(
        _filter_kernel,
        grid=(2, nblk),
        in_specs=[
            pl.BlockSpec((n, FILTER_HIDDEN), lambda o, j: (0, 0)),
            pl.BlockSpec((FILTER_HIDDEN, LANES), lambda o, j: (0, o * 2 * nblk + j)),
            pl.BlockSpec((FILTER_HIDDEN, LANES), lambda o, j: (0, o * 2 * nblk + nblk + j)),
            pl.BlockSpec((1, 2, LANES), lambda o, j: (o, 0, j)),
        ] + [_const_spec(c, 2) for c in consts],
        out_specs=pl.BlockSpec((1, 2, n, LANES), lambda o, j: (o, 0, 0, j)),
        out_shape=jax.ShapeDtypeStruct((2, 2, n, e), F32),
        compiler_params=_cparams("parallel", "arbitrary"),
        name="hyena_filter",
    )(z, f_w_out, f_w_out, f_decay.reshape(2, 2, e), *consts)


def _long_conv(u, xg, kf, order, bias, tabs, col_u, col_x):
    b, l, _ = u.shape
    e = WIDTH
    n = 2 * l
    nblk = e // LANES
    consts = [tabs["fa_half"][0], tabs["fa_half"][1], tabs["c2"], tabs["s2"], tabs["twr"], tabs["twi"],
              tabs["fc"][0], tabs["fc"][1], tabs["c1h"], tabs["s1h"], tabs["twtr"], tabs["twti"]]
    return pl.pallas_call(
        _lconv_kernel,
        grid=(nblk, b // 2),
        in_specs=[
            pl.BlockSpec((2, l, LANES), lambda j, pi: (pi, 0, col_u * nblk + j)),
            pl.BlockSpec((2, l, LANES), lambda j, pi: (pi, 0, col_x * nblk + j)),
            pl.BlockSpec((1, 2, n, LANES), lambda j, pi: (order, 0, 0, j)),
            pl.BlockSpec((1, 1, LANES), lambda j, pi: (order, 0, j)),
        ] + [_const_spec(c, 2) for c in consts],
        out_specs=pl.BlockSpec((2, l, LANES), lambda j, pi: (pi, 0, j)),
        out_shape=jax.ShapeDtypeStruct((b, l, e), F32),
        scratch_shapes=[pltpu.VMEM((n, LANES), F32), pltpu.VMEM((n, LANES), F32)],
        compiler_params=_cparams("parallel", "arbitrary"),
        name="hyena_long_conv",
    )(u, xg, kf, bias.reshape(2, 1, e), *consts)


def _gated_out_kernel(z_ref, g_ref, x_ref, mod_ref, w_ref, o_ref):
    zz = (z_ref[0] * _silu(g_ref[0])).astype(BF16)
    o_ref[0] = x_ref[0] + mod_ref[0, 2:3, :] * _dot(zz, w_ref[...])


def _gated_out(z, g_arr, g_col, x, mod, w_out, name):
    b, l, d = x.shape
    e = WIDTH
    tl = 256
    tile = lambda col: pl.BlockSpec((1, tl, e), lambda bi, ti: (bi, ti, col))
    return pl.pallas_call(
        _gated_out_kernel,
        grid=(b, l // tl),
        in_specs=[tile(0), tile(g_col), tile(0),
                  pl.BlockSpec((1, 3, d), lambda bi, ti: (bi, 0, 0)),
                  pl.BlockSpec((e, d), lambda bi, ti: (0, 0))],
        out_specs=tile(0),
        out_shape=jax.ShapeDtypeStruct((b, l, d), F32),
        compiler_params=_cparams("parallel", "arbitrary"),
        name=name,
    )(z, g_arr, x, mod, w_out.astype(BF16))


def _hyena_layer(x, mod, ng, p, tabs):
    (w_in, short_w, short_b, f_w1, f_b1, f_w_mid, f_b_mid, f_freq, f_w_out, f_decay, bias, w_out) = p
    l = x.shape[1]
    proj = _norm_proj(x, mod, ng, w_in, "hyena_proj")
    u3 = _short_conv(proj, short_w, short_b)
    kf = _hyena_filters(l, (f_w1, f_b1, f_w_mid, f_b_mid, f_freq, f_w_out, f_decay), tabs)
    z1 = _long_conv(u3, u3, kf, 0, bias, tabs, col_u=2, col_x=0)
    z2 = _long_conv(z1, u3, kf, 1, bias, tabs, col_u=0, col_x=1)
    return _gated_out(z2, proj, 3, x, mod, w_out, "hyena_out")


def _cf_proj_kernel(x_ref, mod_ref, ng_ref, w_ref, u_o, sg_o):
    e = WIDTH
    h = _prenorm(x_ref[0], ng_ref[...], mod_ref[0, 0:1, :], mod_ref[0, 1:2, :]).astype(BF16)
    a = _dot(h, w_ref[:, 0:e])
    bgate = _dot(h, w_ref[:, e:2 * e])
    g = _dot(h, w_ref[:, 2 * e:3 * e])
    u_o[0] = a * _sigmoid(bgate)
    sg_o[0] = _silu(g)


def _cf_out_kernel(u_ref, up_ref, un_ref, sg_ref, x_ref, mod_ref, dw_ref, db_ref, lg_ref, lb_ref, w_ref,
                   o_ref, ext_ref):
    t = pl.program_id(1)
    nt = pl.num_programs(1)
    tl = u_ref.shape[1]
    hh = CONV_HALO
    ext_ref[0:hh, :] = jnp.where(t > 0, up_ref[0], 0.0)
    ext_ref[hh:hh + tl, :] = u_ref[0]
    ext_ref[hh + tl:hh + tl + hh, :] = jnp.where(t < nt - 1, un_ref[0], 0.0)
    off = hh - (CONV_WIDTH - 1) // 2
    acc = jnp.zeros((tl, WIDTH), F32) + db_ref[...]
    for k in range(CONV_WIDTH):
        acc = acc + ext_ref[off + k:off + k + tl, :] * dw_ref[k:k + 1, :]
    mean = jnp.mean(acc, axis=-1, keepdims=True)
    cen = acc - mean
    var = jnp.mean(cen * cen, axis=-1, keepdims=True)
    y = cen * lax.rsqrt(var + LN_EPS) * lg_ref[...] + lb_ref[...]
    zz = (_silu(y) * sg_ref[0]).astype(BF16)
    o_ref[0] = x_ref[0] + mod_ref[0, 2:3, :] * _dot(zz, w_ref[...])


def _conformer_layer(x, mod, ng, p):
    (w_in, dw_w, dw_b, ln_g, ln_b, w_out) = p
    b, l, d = x.shape
    e = WIDTH
    tl = 256
    tile = pl.BlockSpec((1, tl, e), lambda bi, ti: (bi, ti, 0))
    modspec = pl.BlockSpec((1, 3, d), lambda bi, ti: (bi, 0, 0))
    vec = pl.BlockSpec((1, e), lambda bi, ti: (0, 0))
    u, sg = pl.pallas_call(
        _cf_proj_kernel,
        grid=(b, l // tl),
        in_specs=[tile, modspec, vec, pl.BlockSpec((d, 3 * e), lambda bi, ti: (0, 0))],
        out_specs=[tile, tile],
        out_shape=[jax.ShapeDtypeStruct((b, l, e), F32)] * 2,
        compiler_params=_cparams("parallel", "arbitrary"),
        name="conformer_proj",
    )(x, mod, ng.reshape(1, d), w_in.astype(BF16))

    hh = CONV_HALO
    hb = tl // hh
    lb = l // hh
    return pl.pallas_call(
        _cf_out_kernel,
        grid=(b, l // tl),
        in_specs=[
            tile,
            pl.BlockSpec((1, hh, e), lambda bi, ti: (bi, jnp.maximum(ti * hb - 1, 0), 0)),
            pl.BlockSpec((1, hh, e), lambda bi, ti: (bi, jnp.minimum((ti + 1) * hb, lb - 1), 0)),
            tile, tile, modspec,
            pl.BlockSpec((CONV_WIDTH, e), lambda bi, ti: (0, 0)),
            vec, vec, vec,
            pl.BlockSpec((e, d), lambda bi, ti: (0, 0)),
        ],
        out_specs=tile,
        out_shape=jax.ShapeDtypeStruct((b, l, d), F32),
        scratch_shapes=[pltpu.VMEM((tl + 2 * hh, e), F32)],
        compiler_params=_cparams("parallel", "arbitrary"),
        name="conformer_out",
    )(u, u, u, sg, x, mod, dw_w, dw_b.reshape(1, e), ln_g.reshape(1, e), ln_b.reshape(1, e),
      w_out.astype(BF16))


def _final_norm_kernel(x_ref, g_ref, o_ref):
    x = x_ref[0]
    ms = jnp.mean(x * x, axis=-1, keepdims=True)
    o_ref[0] = x * lax.rsqrt(ms + RMS_EPS) * g_ref[...]


def _final_norm(x, g):
    b, l, d = x.shape
    tl = 512
    tile = pl.BlockSpec((1, tl, d), lambda bi, ti: (bi, ti, 0))
    return pl.pallas_call(
        _final_norm_kernel,
        grid=(b, l // tl),
        in_specs=[tile, pl.BlockSpec((1, d), lambda bi, ti: (0, 0))],
        out_specs=tile,
        out_shape=jax.ShapeDtypeStruct((b, l, d), F32),
        compiler_params=_cparams("parallel", "arbitrary"),
        name="final_norm",
    )(x, g.reshape(1, d))


def _seg_constants():
    seg = np.arange(SEG_GROUP) // HEAD
    bd = (seg[:, None] == seg[None, :]).astype(np.float32)
    lane_in_head = np.arange(WIDTH) % HEAD
    delta = (np.arange(HEAD)[:, None] == lane_in_head[None, :]).astype(np.float32)
    return {"bd": jnp.asarray(bd, BF16), "delta": jnp.asarray(delta, F32)}


def _trunk(x, mod, norm_g, rw, hy, cf, final_g, consts, scan_tc=128):
    tabs = None
    for i in range(DEPTH):
        kind, j = i % 3, i // 3
        m = mod[i]
        if kind == 0:
            x = _rwkv_layer(x, m, norm_g[i], [p[j] for p in rw], consts, scan_tc)
        elif kind == 1:
            if tabs is None:
                tabs = _dft_tables(x.shape[1])
            x = _hyena_layer(x, m, norm_g[i], [p[j] for p in hy], tabs)
        else:
            x = _conformer_layer(x, m, norm_g[i], [p[j] for p in cf])
    return _final_norm(x, final_g)


def kernel(x_prompt, x_sample, c_prompt, c_sample, norm_g, mod_w, mod_b, rw_w_in, rw_mu, rw_w0, rw_w2, rw_a0, rw_a2, rw_k_k, rw_k_a, rw_r_k, rw_gn_w, rw_gn_b, rw_w_out, hy_w_in, hy_short_w, hy_short_b, hy_f_w1, hy_f_b1, hy_f_w_mid, hy_f_b_mid, hy_f_freq, hy_f_w_out, hy_f_decay, hy_bias, hy_w_out, cf_w_in, cf_dw_w, cf_dw_b, cf_ln_g, cf_ln_b, cf_w_out, final_g):
    rw = (rw_w_in, rw_mu, rw_w0, rw_w2, rw_a0, rw_a2, rw_k_k, rw_k_a, rw_r_k, rw_gn_w, rw_gn_b, rw_w_out)
    hy = (hy_w_in, hy_short_w, hy_short_b, hy_f_w1, hy_f_b1, hy_f_w_mid, hy_f_b_mid, hy_f_freq, hy_f_w_out,
          hy_f_decay, hy_bias, hy_w_out)
    cf = (cf_w_in, cf_dw_w, cf_dw_b, cf_ln_g, cf_ln_b, cf_w_out)
    consts = _seg_constants()
    bp = x_prompt.shape[0]
    mod = _modulation(jnp.concatenate([c_prompt, c_sample], axis=0), mod_w, mod_b)
    y_prompt = _trunk(x_prompt, mod[:, :bp], norm_g, rw, hy, cf, final_g, consts)
    y_sample = _trunk(x_sample, mod[:, bp:], norm_g, rw, hy, cf, final_g, consts)
    return (y_prompt, y_sample)
```

```python
import functools
import math

import numpy as np
import jax
import jax.numpy as jnp
from jax import lax
from jax.experimental import pallas as pl
from jax.experimental.pallas import tpu as pltpu

F32 = jnp.float32
BF16 = jnp.bfloat16

D_MODEL = 1024
WIDTH = 1024
DEPTH = 4
HEADS = 16
HEAD = 64
LORA = 64
RMS_EPS = 1e-6
LN_EPS = 1e-5
GN_EPS = 64e-5
POS_BANDS = 16
FILTER_HIDDEN = 64
CONV_WIDTH = 31
CONV_HALO = 16
LANES = 128
SEG_GROUP = 256
FFT_N2 = 64
FFT_UNROLL = 8
VMEM_LIMIT = 56 * 1024 * 1024


def _cparams(*sem):
    return pltpu.CompilerParams(dimension_semantics=sem, vmem_limit_bytes=VMEM_LIMIT)


def _dot(a, b):
    return jnp.dot(a, b, preferred_element_type=F32)


def _split(x):
    hi = x.astype(BF16)
    lo = (x - hi.astype(F32)).astype(BF16)
    return hi, lo


def _dot_x3(a, b):
    ah, al = _split(a)
    bh, bl = _split(b)
    return _dot(ah, bh) + _dot(ah, bl) + _dot(al, bh)


def _dot_presplit(a, bh, bl):
    ah, al = _split(a)
    return _dot(ah, bh) + _dot(ah, bl) + _dot(al, bh)


def _seg_bcast(t, bd):
    outs = []
    for g in range(t.shape[1] // SEG_GROUP):
        th, tl = _split(t[:, g * SEG_GROUP:(g + 1) * SEG_GROUP])
        outs.append(_dot(th, bd) + _dot(tl, bd))
    return jnp.concatenate(outs, axis=1)


def _sigmoid(x):
    return 1.0 / (1.0 + jnp.exp(-x))


def _silu(x):
    return x * _sigmoid(x)


def _prenorm(x, ng, shift, scale):
    ms = jnp.mean(x * x, axis=-1, keepdims=True)
    return (x * lax.rsqrt(ms + RMS_EPS) * ng) * (1.0 + scale) + shift


def _mod_kernel(c_ref, w_ref, b_ref, o_ref):
    o_ref[0] = _dot_x3(_silu(c_ref[...]), w_ref[0]) + b_ref[0]


def _modulation(c_all, mod_w, mod_b):
    bc = c_all.shape[0]
    out = pl.pallas_call(
        _mod_kernel,
        grid=(DEPTH, 3),
        in_specs=[
            pl.BlockSpec((bc, D_MODEL), lambda i, j: (0, 0)),
            pl.BlockSpec((1, D_MODEL, D_MODEL), lambda i, j: (i, 0, j)),
            pl.BlockSpec((1, 1, D_MODEL), lambda i, j: (i, 0, j)),
        ],
        out_specs=pl.BlockSpec((1, bc, D_MODEL), lambda i, j: (i, 0, j)),
        out_shape=jax.ShapeDtypeStruct((DEPTH, bc, 3 * D_MODEL), F32),
        compiler_params=_cparams("arbitrary", "arbitrary"),
        name="modulation",
    )(c_all, mod_w, mod_b.reshape(DEPTH, 1, 3 * D_MODEL))
    return out.reshape(DEPTH, bc, 3, D_MODEL)


RW_TL = 128
SCAN_BATCH = 1
SCAN_CHUNK = HEAD


def _rw_proj_kernel(x_ref, xp_ref, xn_ref, mod_ref, ng_ref, mu_ref, win_ref,
                    w0_ref, w2h_ref, w2l_ref, a0_ref, a2h_ref, a2l_ref,
                    kk_ref, ka_ref, rk_ref, bd_ref,
                    r_o, v_o, nkk_o, g_o, bv_o, w0_o, k0_o, b0_o, w1_o, k1_o, b1_o):
    t = pl.program_id(1)
    nt = pl.num_programs(1)
    tl = x_ref.shape[1]
    e = WIDTH
    shift = mod_ref[0, 0:1, :]
    scale = mod_ref[0, 1:2, :]
    ng = ng_ref[...]
    bd = bd_ref[...]

    h = _prenorm(x_ref[0], ng, shift, scale)
    hp = _prenorm(xp_ref[0], ng, shift, scale)[7:8, :]
    hn = _prenorm(xn_ref[0], ng, shift, scale)[0:1, :]
    hp = jnp.where(t > 0, hp, 0.0)
    hn = jnp.where(t < nt - 1, hn, 0.0)
    rows = lax.broadcasted_iota(jnp.int32, (tl, 1), 0)
    h_up = jnp.where(rows == 0, hp, pltpu.roll(h, 1, 0))
    h_dn = jnp.where(rows == tl - 1, hn, pltpu.roll(h, tl - 1, 0))
    xx = 0.5 * (h_up + h_dn) - h

    def proj(i, c0, c1):
        inp = (h + xx * mu_ref[i:i + 1, :]).astype(BF16)
        return _dot(inp, win_ref[:, c0:c1])

    r = proj(0, 0, e)
    k = proj(1, e, 2 * e)
    v = proj(2, 2 * e, 3 * e)
    g = proj(3, 3 * e, 4 * e)
    wl = proj(4, 4 * e, 4 * e + 2 * LORA)
    al = proj(5, 4 * e + 2 * LORA, 4 * e + 4 * LORA)
    twl = jnp.tanh(wl)

    kk0 = k * kk_ref[...]
    nrm = jnp.sqrt(_seg_bcast(kk0 * kk0, bd))
    kk = kk0 / jnp.maximum(nrm, 1e-12)
    ka = ka_ref[...]

    r_o[0] = r
    v_o[0] = v
    g_o[0] = g
    nkk_o[0] = -kk

    ksum = jnp.zeros_like(k)
    for d, (w_o, k_o, b_o) in enumerate(((w0_o, k0_o, b0_o), (w1_o, k1_o, b1_o))):
        w_raw = w0_ref[d:d + 1, :] + _dot_presplit(twl, w2h_ref[d], w2l_ref[d])
        z = -w_raw
        softplus = jnp.maximum(z, 0.0) + jnp.log(1.0 + jnp.exp(-jnp.abs(z)))
        w_log = -softplus - 0.5
        w_o[0] = -jnp.exp(w_log)
        a = _sigmoid(a0_ref[d:d + 1, :] + _dot_presplit(al, a2h_ref[d], a2l_ref[d]))
        k_d = k * (1.0 + (a - 1.0) * ka)
        k_o[0] = k_d
        b_o[0] = kk * a
        ksum = ksum + k_d

    bonus = _seg_bcast(r * ksum * rk_ref[...], bd)
    bv_o[0] = bonus * v


def _dot_nt(a, b):
    return lax.dot_general(a, b, (((1,), (1,)), ((), ())), preferred_element_type=F32)


def _dot_tn(a, b):
    return lax.dot_general(a, b, (((0,), (0,)), ((), ())), preferred_element_type=F32)


def _rw_chunk_kernel(rf, vf, af, wf, kf, bf, rb, vb, ab, wb, kb, bb, bd_ref, yf_o, yb_o, s_ref):
    tb = pl.program_id(1)
    nb, c, _ = rf.shape
    grp = SEG_GROUP
    hpg = grp // HEAD

    @pl.when(tb == 0)
    def _():
        s_ref[...] = jnp.zeros_like(s_ref)

    row = lax.broadcasted_iota(jnp.int32, (c, grp), 0)
    col = lax.broadcasted_iota(jnp.int32, (c, grp), 1) % HEAD
    eye = col == row
    eye_state = (lax.broadcasted_iota(jnp.int32, (HEAD, grp), 0)
                 == lax.broadcasted_iota(jnp.int32, (HEAD, grp), 1) % HEAD)
    tr = lax.broadcasted_iota(jnp.int32, (c, c), 0)
    tcq = lax.broadcasted_iota(jnp.int32, (c, c), 1)
    bd = bd_ref[...]
    bd_f32 = bd.astype(F32)

    def bd_weights(x):
        xh, xl = _split(x)
        return (jnp.concatenate([xh] * hpg, axis=0) * bd, jnp.concatenate([xl] * hpg, axis=0) * bd)

    def pmm(lhs, w, dot=_dot):
        m = lhs.shape[0]
        lh, ll = _split(lhs)
        res = dot(jnp.concatenate([lh, ll], axis=0), w[0])
        return res[:m] + res[m:] + dot(lh, w[1])

    def chain_operands(q, rev, refs):
        r, v, a, lw, k, b = [ref[q] for ref in refs]
        tri = (tcq >= tr if rev else tcq <= tr).astype(BF16)
        p1 = lw.astype(BF16)
        rem = lw - p1.astype(F32)
        p2 = rem.astype(BF16)
        p3 = (rem - p2.astype(F32)).astype(BF16)
        cs = _dot(tri, p1) + _dot(tri, p2) + _dot(tri, p3)
        total = cs[0:1] if rev else cs[c - 1:c]
        e_neg = jnp.exp(-cs)
        e_bar = jnp.exp(total - cs)
        return dict(at=a * jnp.exp(cs - lw), rt=r * jnp.exp(cs), bt=b * e_neg, kt=k * e_neg,
                    bb=b * e_bar, kb=k * e_bar, v=v, gc=jnp.exp(total),
                    strict=(col > row) if rev else (col < row),
                    incl=(col >= row) if rev else (col <= row))

    chains = [(q, rev, refs, y_o)
              for q in range(nb)
              for rev, refs, y_o in ((False, (rf, vf, af, wf, kf, bf), yf_o), (True, (rb, vb, ab, wb, kb, bb), yb_o))]
    ops = [chain_operands(q, rev, refs) for q, rev, refs, _ in chains]
    probs = [(ch, g) for ch in range(len(chains)) for g in range(WIDTH // grp)]
    pick = lambda name: [ops[ch][name][:, g * grp:(g + 1) * grp] for ch, g in probs]
    mask = lambda name: [ops[ch][name] for ch, _ in probs]
    at, rt, bt, kt, bb, kb, vv, gc = (pick(n) for n in ("at", "rt", "bt", "kt", "bb", "kb", "v", "gc"))
    strict, incl = mask("strict"), mask("incl")

    ar = [jnp.concatenate([x, y], axis=0) for x, y in zip(at, rt)]
    sc_b = [pmm(x, bd_weights(w), _dot_nt) for x, w in zip(ar, bt)]
    sc_k = [pmm(x, bd_weights(w), _dot_nt) for x, w in zip(ar, kt)]
    mab = [jnp.where(m, s[:c], 0.0) for m, s in zip(strict, sc_b)]
    mak = [jnp.where(m, s[:c], 0.0) for m, s in zip(strict, sc_k)]
    nrb = [jnp.where(m, s[c:], 0.0) for m, s in zip(incl, sc_b)]
    nrk = [jnp.where(m, s[c:], 0.0) for m, s in zip(incl, sc_k)]
    inv = [jnp.where(eye, 1.0, m) for m in mab]
    pw = [pmm(m, bd_weights(m)) for m in mab]
    doublings = int(math.log2(c)) - 1
    for it in range(doublings):
        if it == doublings - 1:
            inv = [i + pmm(i, bd_weights(p)) for i, p in zip(inv, pw)]
        else:
            both = [pmm(jnp.concatenate([p, i], axis=0), bd_weights(p)) for i, p in zip(inv, pw)]
            pw = [x[:c] for x in both]
            inv = [i + x[c:] for i, x in zip(inv, both)]
    ah = [pmm(i, bd_weights(x)) for i, x in zip(inv, at)]
    mv = [pmm(jnp.concatenate([x, y], axis=0), bd_weights(w)) for x, y, w in zip(mak, nrk, vv)]
    pv = [pmm(i, bd_weights(x[:c])) for i, x in zip(inv, mv)]
    rh = [x + pmm(n, bd_weights(w)) for x, n, w in zip(rt, nrb, ah)]
    yv = [x[c:] + pmm(n, bd_weights(w)) for x, n, w in zip(mv, nrb, pv)]
    dg = [jnp.where(eye_state, x, 0.0) for x in gc]

    x = [pmm(jnp.concatenate([a_, r_, d_], axis=0), bd_weights(s_ref[ch, g]))
         for a_, r_, d_, (ch, g) in zip(ah, rh, dg, probs)]
    p = [x_[:c] + w for x_, w in zip(x, pv)]
    for x_, w, (ch, g) in zip(x, yv, probs):
        q, _, _, y_o = chains[ch]
        y_o[q, :, g * grp:(g + 1) * grp] = x_[c:2 * c] + w
    for x_, p_, v_, b_, k_, (ch, g) in zip(x, p, vv, bb, kb, probs):
        bk_h, bk_l = _split(jnp.concatenate([b_, k_], axis=0))
        pv_h, pv_l = _split(jnp.concatenate([p_, v_], axis=0))
        full = _dot_tn(jnp.concatenate([bk_h, bk_h, bk_l], axis=0),
                       jnp.concatenate([pv_h, pv_l, pv_h], axis=0)) * bd_f32
        upd = full[0:HEAD]
        for hh in range(1, hpg):
            upd = upd + full[hh * HEAD:(hh + 1) * HEAD]
        s_ref[ch, g] = x_[2 * c:] + upd


def _rw_out_kernel(yf_ref, yb_ref, bv_ref, g_ref, x_ref, mod_ref, gw_ref, gb_ref, wout_ref, bd_ref, o_ref):
    bd = bd_ref[...]
    y = yf_ref[0] + yb_ref[0]
    mean = _seg_bcast(y, bd) * (1.0 / HEAD)
    yc = y - mean
    var = _seg_bcast(yc * yc, bd) * (1.0 / HEAD)
    yn = yc * lax.rsqrt(var + GN_EPS) * gw_ref[...] + gb_ref[...]
    yo = (yn + bv_ref[0]) * _silu(g_ref[0])
    o = _dot(yo.astype(BF16), wout_ref[...])
    o_ref[0] = x_ref[0] + mod_ref[0, 2:3, :] * o


def _pad_lora(w):
    z = jnp.zeros_like(w[0])
    return jnp.stack([jnp.concatenate([w[0], z], axis=0), jnp.concatenate([z, w[1]], axis=0)], axis=0)


def _split_host(w):
    hi = w.astype(BF16)
    lo = (w - hi.astype(F32)).astype(BF16)
    return hi, lo


def _rwkv_layer(x, mod, ng, p, consts):
    (w_in, mu, w0, w2, a0, a2, k_k, k_a, r_k, gn_w, gn_b, w_out) = p
    b, l, d = x.shape
    e = WIDTH
    tl = RW_TL
    nt = l // tl
    bd = consts["bd"]
    w2h, w2l = _split_host(_pad_lora(w2))
    a2h, a2l = _split_host(_pad_lora(a2))
    ncols = w_in.shape[1]

    full = lambda shape: pl.BlockSpec(shape, lambda bi, ti: (0,) * len(shape))
    tile = pl.BlockSpec((1, tl, e), lambda bi, ti: (bi, ti, 0))
    hb = tl // 8
    lb = l // 8
    outs = pl.pallas_call(
        _rw_proj_kernel,
        grid=(b, nt),
        in_specs=[
            tile,
            pl.BlockSpec((1, 8, d), lambda bi, ti: (bi, jnp.maximum(ti * hb - 1, 0), 0)),
            pl.BlockSpec((1, 8, d), lambda bi, ti: (bi, jnp.minimum((ti + 1) * hb, lb - 1), 0)),
            pl.BlockSpec((1, 3, d), lambda bi, ti: (bi, 0, 0)),
            full((1, d)), full((6, d)), full((d, ncols)),
            full((2, e)), full((2, 2 * LORA, e)), full((2, 2 * LORA, e)),
            full((2, e)), full((2, 2 * LORA, e)), full((2, 2 * LORA, e)),
            full((1, e)), full((1, e)), full((1, e)),
            full((SEG_GROUP, SEG_GROUP)),
        ],
        out_specs=[tile] * 11,
        out_shape=[jax.ShapeDtypeStruct((b, l, e), F32)] * 11,
        compiler_params=_cparams("parallel", "arbitrary"),
        name="rwkv_proj",
    )(x, x, x, mod, ng.reshape(1, d), mu, w_in.astype(BF16),
      w0, w2h, w2l, a0, a2h, a2l,
      k_k.reshape(1, e), k_a.reshape(1, e), r_k.reshape(1, e), bd)
    r, v, nkk, g, bv, wd0, kd0, bd0, wd1, kd1, bd1 = outs

    tc = SCAN_CHUNK
    ntc = l // tc
    nb = SCAN_BATCH
    fwd = pl.BlockSpec((nb, tc, e), lambda bi, ti: (bi, ti, 0))
    bwd = pl.BlockSpec((nb, tc, e), lambda bi, ti: (bi, ntc - 1 - ti, 0))
    yf, yb = pl.pallas_call(
        _rw_chunk_kernel,
        grid=(b // nb, ntc),
        in_specs=[fwd] * 6 + [bwd] * 6 + [full((SEG_GROUP, SEG_GROUP))],
        out_specs=[fwd, bwd],
        out_shape=[jax.ShapeDtypeStruct((b, l, e), F32)] * 2,
        scratch_shapes=[pltpu.VMEM((2 * nb, e // SEG_GROUP, HEAD, SEG_GROUP), F32)],
        compiler_params=_cparams("parallel", "arbitrary"),
        name="rwkv_scan",
    )(r, v, nkk, wd0, kd0, bd0, r, v, nkk, wd1, kd1, bd1, bd)

    tlo = 256
    tile_o = pl.BlockSpec((1, tlo, e), lambda bi, ti: (bi, ti, 0))
    return pl.pallas_call(
        _rw_out_kernel,
        grid=(b, l // tlo),
        in_specs=[tile_o] * 5 + [
            pl.BlockSpec((1, 3, d), lambda bi, ti: (bi, 0, 0)),
            full((1, e)), full((1, e)), full((e, d)), full((SEG_GROUP, SEG_GROUP)),
        ],
        out_specs=tile_o,
        out_shape=jax.ShapeDtypeStruct((b, l, d), F32),
        compiler_params=_cparams("parallel", "arbitrary"),
        name="rwkv_out",
    )(yf, yb, bv, g, x, mod, gn_w.reshape(1, e), gn_b.reshape(1, e), w_out.astype(BF16), bd)


def _norm_proj_kernel(x_ref, mod_ref, ng_ref, w_ref, o_ref):
    h = _prenorm(x_ref[0], ng_ref[...], mod_ref[0, 0:1, :], mod_ref[0, 1:2, :])
    o_ref[0] = _dot(h.astype(BF16), w_ref[...])


def _norm_proj(x, mod, ng, w, name):
    b, l, d = x.shape
    n = w.shape[1]
    tl = 256
    return pl.pallas_call(
        _norm_proj_kernel,
        grid=(b, l // tl),
        in_specs=[
            pl.BlockSpec((1, tl, d), lambda bi, ti: (bi, ti, 0)),
            pl.BlockSpec((1, 3, d), lambda bi, ti: (bi, 0, 0)),
            pl.BlockSpec((1, d), lambda bi, ti: (0, 0)),
            pl.BlockSpec((d, n), lambda bi, ti: (0, 0)),
        ],
        out_specs=pl.BlockSpec((1, tl, n), lambda bi, ti: (bi, ti, 0)),
        out_shape=jax.ShapeDtypeStruct((b, l, n), F32),
        compiler_params=_cparams("parallel", "arbitrary"),
        name=name,
    )(x, mod, ng.reshape(1, d), w.astype(BF16))


def _short_conv_kernel(u_ref, w_ref, b_ref, o_ref):
    u = u_ref[0]
    l = u.shape[0]
    rows = lax.broadcasted_iota(jnp.int32, (l, 1), 0)
    up = jnp.where(rows == 0, 0.0, pltpu.roll(u, 1, 0))
    dn = jnp.where(rows == l - 1, 0.0, pltpu.roll(u, l - 1, 0))
    o_ref[0] = up * w_ref[0:1, :] + u * w_ref[1:2, :] + dn * w_ref[2:3, :] + b_ref[...]


def _short_conv(proj, w, bias):
    b, l, _ = proj.shape
    n = w.shape[1]
    blk = pl.BlockSpec((1, l, LANES), lambda bi, ci: (bi, 0, ci))
    return pl.pallas_call(
        _short_conv_kernel,
        grid=(b, n // LANES),
        in_specs=[blk,
                  pl.BlockSpec((3, LANES), lambda bi, ci: (0, ci)),
                  pl.BlockSpec((1, LANES), lambda bi, ci: (0, ci))],
        out_specs=blk,
        out_shape=jax.ShapeDtypeStruct((b, l, n), F32),
        compiler_params=_cparams("parallel", "arbitrary"),
        name="hyena_short_conv",
    )(proj, w, bias.reshape(1, n))


def _cmm(s4, s2, xre, xim, conj=False):
    r = s2.shape[0] // 2

    def prods(x):
        xh, xl = _split(x)
        p4 = _dot(s4, xh)
        p2 = _dot(s2, xl)
        re_x = p4[0:r] + p4[r:2 * r] + p2[0:r]
        im_x = p4[2 * r:3 * r] + p4[3 * r:4 * r] + p2[r:2 * r]
        return re_x, im_x

    rr, ir = prods(xre)
    if xim is None:
        return rr, ir
    ri, ii = prods(xim)
    if conj:
        return rr + ii, ri - ir
    return rr - ii, ir + ri


def _stacks(mre, mim):
    reh, rel = _split(mre)
    imh, iml = _split(mim)
    return jnp.concatenate([reh, rel, imh, iml], axis=0), jnp.concatenate([reh, imh], axis=0)


def _fft_fwd_outer(src_re, src_im, dst_re, dst_im, fa4_ref, fa2_ref, n1, n2, k1, scale=None):
    fa4 = fa4_ref[...]
    fa2 = fa2_ref[...]

    def body(i, carry):
        xre = src_re[pl.ds(i, k1, stride=n2), :]
        xim = None if src_im is None else src_im[pl.ds(i, k1, stride=n2), :]
        if scale is not None:
            xre = xre * scale
        are, aim = _cmm(fa4, fa2, xre, xim)
        dst_re[pl.ds(i, n1, stride=n2), :] = are
        dst_im[pl.ds(i, n1, stride=n2), :] = aim
        return carry

    lax.fori_loop(0, n2, body, 0, unroll=FFT_UNROLL)


def _twiddled_inner(c2_ref, s2_ref, twr_ref, twi_ref, k):
    tre = twr_ref[pl.ds(k, 1), :]
    tim = twi_ref[pl.ds(k, 1), :]
    c2 = c2_ref[...]
    s2 = s2_ref[...]
    return _stacks(c2 * tre - s2 * tim, c2 * tim + s2 * tre)


def _lconv_kernel(u_ref, x_ref, kf_ref, bias_ref, fa4_ref, fa2_ref, c2_ref, s2_ref, twr_ref, twi_ref,
                  fc4_ref, fc2_ref, c1_ref, s1_ref, twtr_ref, twti_ref, o_ref, wr_ref, wi_ref):
    l = u_ref.shape[1]
    n2 = FFT_N2
    n1 = 2 * l // n2
    half = n1 // 2

    _fft_fwd_outer(u_ref.at[0], u_ref.at[1], wr_ref, wi_ref, fa4_ref, fa2_ref, n1, n2, half)

    fc4 = fc4_ref[...]
    fc2 = fc2_ref[...]

    def slab(k, carry):
        rows = pl.ds(pl.multiple_of(k * n2, n2), n2)
        g4, g2 = _twiddled_inner(c2_ref, s2_ref, twr_ref, twi_ref, k)
        xre, xim = _cmm(g4, g2, wr_ref[rows, :], wi_ref[rows, :])
        kre = kf_ref[0, 0, rows, :]
        kim = kf_ref[0, 1, rows, :]
        yre = xre * kre - xim * kim
        yim = xre * kim + xim * kre
        are, aim = _cmm(fc4, fc2, yre, yim, conj=True)
        wr_ref[rows, :] = are
        wi_ref[rows, :] = aim
        return carry

    lax.fori_loop(0, n1, slab, 0, unroll=FFT_UNROLL)

    bias = bias_ref[0]

    def outer(i, carry):
        tre = twtr_ref[pl.ds(i, 1), :]
        tim = twti_ref[pl.ds(i, 1), :]
        c1 = c1_ref[...]
        s1 = s1_ref[...]
        q4, q2 = _stacks(c1 * tre - s1 * tim, -(c1 * tim + s1 * tre))
        yre, yim = _cmm(q4, q2, wr_ref[pl.ds(i, n1, stride=n2), :], wi_ref[pl.ds(i, n1, stride=n2), :])
        rows = pl.ds(i, half, stride=n2)
        u0 = u_ref[0, rows, :]
        u1 = u_ref[1, rows, :]
        o_ref[0, rows, :] = x_ref[0, rows, :] * (yre + u0 * bias)
        o_ref[1, rows, :] = x_ref[1, rows, :] * (yim + u1 * bias)
        return carry

    lax.fori_loop(0, n2, outer, 0, unroll=FFT_UNROLL)


FILTER_CHUNK = 512


def _filter_pos(base, chunk, l):
    m = base + lax.broadcasted_iota(jnp.int32, (chunk, 1), 0)
    is_f = m < l
    pos = jnp.where(is_f, m, 2 * l - m).astype(F32)
    return m, is_f, pos


def _filter_mlp_kernel(l, w1_ref, b1_ref, wm_ref, bm_ref, fr_ref, bands_ref, z_ref):
    chunk = z_ref.shape[0]
    _, _, pos = _filter_pos(pl.program_id(0) * chunk, chunk, l)
    lane = lax.broadcasted_iota(jnp.int32, (1, LANES), 1)
    t = pos / max(l - 1, 1)
    ang = (2.0 * math.pi / l) * pos * bands_ref[...]
    feat = jnp.where(lane == 0, t,
                     jnp.where(lane <= POS_BANDS, jnp.cos(ang),
                               jnp.where(lane <= 2 * POS_BANDS, -jnp.sin(ang), 0.0)))
    z = jnp.sin(fr_ref[0:1, :] * (_dot_x3(feat, w1_ref[...]) + b1_ref[...]))
    for i in range(2):
        z = jnp.sin(fr_ref[i + 1:i + 2, :] * (_dot_x3(z, wm_ref[i]) + bm_ref[i:i + 1, :]))
    z_ref[...] = z


def _filter_kernel(z_ref, wof_ref, wob_ref, dec_ref,
                   fa4_ref, fa2_ref, c2_ref, s2_ref, twr_ref, twi_ref, o_ref):
    n = o_ref.shape[2]
    l = n // 2
    n2 = FFT_N2
    n1 = n // n2
    chunk = min(FILTER_CHUNK, n)
    nchunk = n // chunk
    wof = _split(wof_ref[...])
    wob = _split(wob_ref[...])
    dec_f = jnp.abs(dec_ref[0, 0:1, :])
    dec_b = jnp.abs(dec_ref[0, 1:2, :])

    def gen(c, asum):
        base = pl.multiple_of(c * chunk, chunk)
        m, is_f, pos = _filter_pos(base, chunk, l)
        t = pos / max(l - 1, 1)
        z = z_ref[pl.ds(base, chunk), :]
        hf = _dot_presplit(z, *wof) * jnp.exp(-t * dec_f)
        hb = _dot_presplit(z, *wob) * jnp.exp(-t * dec_b)
        k2 = jnp.where(is_f, hf, jnp.where(m == l, 0.0, hb))
        o_ref[0, 0, pl.ds(base, chunk), :] = k2
        return asum + jnp.sum(jnp.abs(k2), axis=0, keepdims=True)

    asum = lax.fori_loop(0, nchunk, gen, jnp.zeros((1, LANES), F32))
    inv = 1.0 / asum

    re_ref = o_ref.at[0, 0]
    im_ref = o_ref.at[0, 1]
    _fft_fwd_outer(re_ref, None, re_ref, im_ref, fa4_ref, fa2_ref, n1, n2, n1, scale=inv)

    def slab(k, carry):
        rows = pl.ds(pl.multiple_of(k * n2, n2), n2)
        g4, g2 = _twiddled_inner(c2_ref, s2_ref, twr_ref, twi_ref, k)
        xre, xim = _cmm(g4, g2, re_ref[rows, :], im_ref[rows, :])
        re_ref[rows, :] = xre
        im_ref[rows, :] = xim
        return carry

    lax.fori_loop(0, n1, slab, 0, unroll=FFT_UNROLL)


def _dft_tables(l):
    n = 2 * l
    n2 = FFT_N2
    n1 = n // n2
    half = n1 // 2

    def cs(rows, cols, period):
        ang = 2.0 * np.pi * ((np.arange(rows)[:, None] * np.arange(cols)[None, :]) % period) / period
        return np.cos(ang), -np.sin(ang)

    def split_np(m):
        m32 = jnp.asarray(m, F32)
        hi = m32.astype(BF16)
        lo = (m32 - hi.astype(F32)).astype(BF16)
        return hi, lo

    def stacks_np(c, s):
        ch, cl = split_np(c)
        sh, sl = split_np(s)
        return jnp.concatenate([ch, cl, sh, sl], axis=0), jnp.concatenate([ch, sh], axis=0)

    c1, s1 = cs(n1, n1, n1)
    c2, s2 = cs(n2, n2, n2)
    tw_c, tw_s = cs(n1, n2, n)
    tabs = {}
    tabs["fa_half"] = stacks_np(c1[:, :half], s1[:, :half])
    tabs["fa_full"] = stacks_np(c1, s1)
    tabs["fc"] = stacks_np(c2, s2)
    tabs["c2"] = jnp.asarray(c2, F32)
    tabs["s2"] = jnp.asarray(s2, F32)
    tabs["twr"] = jnp.asarray(tw_c, F32)
    tabs["twi"] = jnp.asarray(tw_s, F32)
    tabs["c1h"] = jnp.asarray(c1[:half] / n, F32)
    tabs["s1h"] = jnp.asarray(s1[:half] / n, F32)
    tabs["twtr"] = jnp.asarray(tw_c.T, F32)
    tabs["twti"] = jnp.asarray(tw_s.T, F32)
    return tabs


def _const_spec(a, nd_grid):
    shape = a.shape
    if nd_grid == 2:
        return pl.BlockSpec(shape, lambda i, j: (0,) * len(shape))
    return pl.BlockSpec(shape, lambda i: (0,) * len(shape))


def _hyena_filters(l, p, tabs):
    (f_w1, f_b1, f_w_mid, f_b_mid, f_freq, f_w_out, f_decay) = p
    n = 2 * l
    e = WIDTH
    nblk = e // LANES
    w1p = jnp.zeros((LANES, FILTER_HIDDEN), F32).at[:f_w1.shape[0]].set(f_w1)
    bands = np.linspace(1e-4, POS_BANDS - 1, POS_BANDS, dtype=np.float32)
    bl = np.zeros((1, LANES), np.float32)
    bl[0, 1:1 + POS_BANDS] = bands
    bl[0, 1 + POS_BANDS:1 + 2 * POS_BANDS] = bands
    consts = [tabs["fa_full"][0], tabs["fa_full"][1], tabs["c2"], tabs["s2"], tabs["twr"], tabs["twi"]]
    chunk = min(FILTER_CHUNK, n)
    mlp_in = [w1p, f_b1.reshape(1, FILTER_HIDDEN), f_w_mid, f_b_mid, f_freq, jnp.asarray(bl)]
    z = pl.pallas_call(
        functools.partial(_filter_mlp_kernel, l),
        grid=(n // chunk,),
        in_specs=[_const_spec(a, 1) for a in mlp_in],
        out_specs=pl.BlockSpec((chunk, FILTER_HIDDEN), lambda c: (c, 0)),
        out_shape=jax.ShapeDtypeStruct((n, FILTER_HIDDEN), F32),
        compiler_params=_cparams("parallel"),
        name="hyena_filter_mlp",
    )(*mlp_in)
    return pl.pallas_call(
        _filter_kernel,
        grid=(2, nblk),
        in_specs=[
            pl.BlockSpec((n, FILTER_HIDDEN), lambda o, j: (0, 0)),
            pl.BlockSpec((FILTER_HIDDEN, LANES), lambda o, j: (0, o * 2 * nblk + j)),
            pl.BlockSpec((FILTER_HIDDEN, LANES), lambda o, j: (0, o * 2 * nblk + nblk + j)),
            pl.BlockSpec((1, 2, LANES), lambda o, j: (o, 0, j)),
        ] + [_const_spec(c, 2) for c in consts],
        out_specs=pl.BlockSpec((1, 2, n, LANES), lambda o, j: (o, 0, 0, j)),
        out_shape=jax.ShapeDtypeStruct((2, 2, n, e), F32),
        compiler_params=_cparams("parallel", "arbitrary"),
        name="hyena_filter",
    )(z, f_w_out, f_w_out, f_decay.reshape(2, 2, e), *consts)


def _long_conv(u, xg, kf, order, bias, tabs, col_u, col_x):
    b, l, _ = u.shape
    e = WIDTH
    n = 2 * l
    nblk = e // LANES
    consts = [tabs["fa_half"][0], tabs["fa_half"][1], tabs["c2"], tabs["s2"], tabs["twr"], tabs["twi"],
              tabs["fc"][0], tabs["fc"][1], tabs["c1h"], tabs["s1h"], tabs["twtr"], tabs["twti"]]
    return pl.pallas_call(
        _lconv_kernel,
        grid=(nblk, b // 2),
        in_specs=[
            pl.BlockSpec((2, l, LANES), lambda j, pi: (pi, 0, col_u * nblk + j)),
            pl.BlockSpec((2, l, LANES), lambda j, pi: (pi, 0, col_x * nblk + j)),
            pl.BlockSpec((1, 2, n, LANES), lambda j, pi: (order, 0, 0, j)),
            pl.BlockSpec((1, 1, LANES), lambda j, pi: (order, 0, j)),
        ] + [_const_spec(c, 2) for c in consts],
        out_specs=pl.BlockSpec((2, l, LANES), lambda j, pi: (pi, 0, j)),
        out_shape=jax.ShapeDtypeStruct((b, l, e), F32),
        scratch_shapes=[pltpu.VMEM((n, LANES), F32), pltpu.VMEM((n, LANES), F32)],
        compiler_params=_cparams("parallel", "arbitrary"),
        name="hyena_long_conv",
    )(u, xg, kf, bias.reshape(2, 1, e), *consts)


def _gated_out_kernel(z_ref, g_ref, x_ref, mod_ref, w_ref, o_ref):
    zz = (z_ref[0] * _silu(g_ref[0])).astype(BF16)
    o_ref[0] = x_ref[0] + mod_ref[0, 2:3, :] * _dot(zz, w_ref[...])


def _gated_out(z, g_arr, g_col, x, mod, w_out, name):
    b, l, d = x.shape
    e = WIDTH
    tl = 256
    tile = lambda col: pl.BlockSpec((1, tl, e), lambda bi, ti: (bi, ti, col))
    return pl.pallas_call(
        _gated_out_kernel,
        grid=(b, l // tl),
        in_specs=[tile(0), tile(g_col), tile(0),
                  pl.BlockSpec((1, 3, d), lambda bi, ti: (bi, 0, 0)),
                  pl.BlockSpec((e, d), lambda bi, ti: (0, 0))],
        out_specs=tile(0),
        out_shape=jax.ShapeDtypeStruct((b, l, d), F32),
        compiler_params=_cparams("parallel", "arbitrary"),
        name=name,
    )(z, g_arr, x, mod, w_out.astype(BF16))


def _hyena_layer(x, mod, ng, p, tabs):
    (w_in, short_w, short_b, f_w1, f_b1, f_w_mid, f_b_mid, f_freq, f_w_out, f_decay, bias, w_out) = p
    l = x.shape[1]
    proj = _norm_proj(x, mod, ng, w_in, "hyena_proj")
    u3 = _short_conv(proj, short_w, short_b)
    kf = _hyena_filters(l, (f_w1, f_b1, f_w_mid, f_b_mid, f_freq, f_w_out, f_decay), tabs)
    z1 = _long_conv(u3, u3, kf, 0, bias, tabs, col_u=2, col_x=0)
    z2 = _long_conv(z1, u3, kf, 1, bias, tabs, col_u=0, col_x=1)
    return _gated_out(z2, proj, 3, x, mod, w_out, "hyena_out")


def _cf_proj_kernel(x_ref, mod_ref, ng_ref, w_ref, u_o, sg_o):
    e = WIDTH
    h = _prenorm(x_ref[0], ng_ref[...], mod_ref[0, 0:1, :], mod_ref[0, 1:2, :]).astype(BF16)
    a = _dot(h, w_ref[:, 0:e])
    bgate = _dot(h, w_ref[:, e:2 * e])
    g = _dot(h, w_ref[:, 2 * e:3 * e])
    u_o[0] = a * _sigmoid(bgate)
    sg_o[0] = _silu(g)


def _cf_out_kernel(u_ref, up_ref, un_ref, sg_ref, x_ref, mod_ref, dw_ref, db_ref, lg_ref, lb_ref, w_ref,
                   o_ref, ext_ref):
    t = pl.program_id(1)
    nt = pl.num_programs(1)
    tl = u_ref.shape[1]
    hh = CONV_HALO
    ext_ref[0:hh, :] = jnp.where(t > 0, up_ref[0], 0.0)
    ext_ref[hh:hh + tl, :] = u_ref[0]
    ext_ref[hh + tl:hh + tl + hh, :] = jnp.where(t < nt - 1, un_ref[0], 0.0)
    off = hh - (CONV_WIDTH - 1) // 2
    acc = jnp.zeros((tl, WIDTH), F32) + db_ref[...]
    for k in range(CONV_WIDTH):
        acc = acc + ext_ref[off + k:off + k + tl, :] * dw_ref[k:k + 1, :]
    mean = jnp.mean(acc, axis=-1, keepdims=True)
    cen = acc - mean
    var = jnp.mean(cen * cen, axis=-1, keepdims=True)
    y = cen * lax.rsqrt(var + LN_EPS) * lg_ref[...] + lb_ref[...]
    zz = (_silu(y) * sg_ref[0]).astype(BF16)
    o_ref[0] = x_ref[0] + mod_ref[0, 2:3, :] * _dot(zz, w_ref[...])


def _conformer_layer(x, mod, ng, p):
    (w_in, dw_w, dw_b, ln_g, ln_b, w_out) = p
    b, l, d = x.shape
    e = WIDTH
    tl = 256
    tile = pl.BlockSpec((1, tl, e), lambda bi, ti: (bi, ti, 0))
    modspec = pl.BlockSpec((1, 3, d), lambda bi, ti: (bi, 0, 0))
    vec = pl.BlockSpec((1, e), lambda bi, ti: (0, 0))
    u, sg = pl.pallas_call(
        _cf_proj_kernel,
        grid=(b, l // tl),
        in_specs=[tile, modspec, vec, pl.BlockSpec((d, 3 * e), lambda bi, ti: (0, 0))],
        out_specs=[tile, tile],
        out_shape=[jax.ShapeDtypeStruct((b, l, e), F32)] * 2,
        compiler_params=_cparams("parallel", "arbitrary"),
        name="conformer_proj",
    )(x, mod, ng.reshape(1, d), w_in.astype(BF16))

    hh = CONV_HALO
    hb = tl // hh
    lb = l // hh
    return pl.pallas_call(
        _cf_out_kernel,
        grid=(b, l // tl),
        in_specs=[
            tile,
            pl.BlockSpec((1, hh, e), lambda bi, ti: (bi, jnp.maximum(ti * hb - 1, 0), 0)),
            pl.BlockSpec((1, hh, e), lambda bi, ti: (bi, jnp.minimum((ti + 1) * hb, lb - 1), 0)),
            tile, tile, modspec,
            pl.BlockSpec((CONV_WIDTH, e), lambda bi, ti: (0, 0)),
            vec, vec, vec,
            pl.BlockSpec((e, d), lambda bi, ti: (0, 0)),
        ],
        out_specs=tile,
        out_shape=jax.ShapeDtypeStruct((b, l, d), F32),
        scratch_shapes=[pltpu.VMEM((tl + 2 * hh, e), F32)],
        compiler_params=_cparams("parallel", "arbitrary"),
        name="conformer_out",
    )(u, u, u, sg, x, mod, dw_w, dw_b.reshape(1, e), ln_g.reshape(1, e), ln_b.reshape(1, e),
      w_out.astype(BF16))


def _final_norm_kernel(x_ref, g_ref, o_ref):
    x = x_ref[0]
    ms = jnp.mean(x * x, axis=-1, keepdims=True)
    o_ref[0] = x * lax.rsqrt(ms + RMS_EPS) * g_ref[...]


def _final_norm(x, g):
    b, l, d = x.shape
    tl = 512
    tile = pl.BlockSpec((1, tl, d), lambda bi, ti: (bi, ti, 0))
    return pl.pallas_call(
        _final_norm_kernel,
        grid=(b, l // tl),
        in_specs=[tile, pl.BlockSpec((1, d), lambda bi, ti: (0, 0))],
        out_specs=tile,
        out_shape=jax.ShapeDtypeStruct((b, l, d), F32),
        compiler_params=_cparams("parallel", "arbitrary"),
        name="final_norm",
    )(x, g.reshape(1, d))


def _seg_constants():
    seg = np.arange(SEG_GROUP) // HEAD
    bd = (seg[:, None] == seg[None, :]).astype(np.float32)
    return {"bd": jnp.asarray(bd, BF16)}


def _trunk(x, mod, norm_g, rw, hy, cf, final_g, consts):
    tabs = None
    for i in range(DEPTH):
        kind, j = i % 3, i // 3
        m = mod[i]
        if kind == 0:
            x = _rwkv_layer(x, m, norm_g[i], [p[j] for p in rw], consts)
        elif kind == 1:
            if tabs is None:
                tabs = _dft_tables(x.shape[1])
            x = _hyena_layer(x, m, norm_g[i], [p[j] for p in hy], tabs)
        else:
            x = _conformer_layer(x, m, norm_g[i], [p[j] for p in cf])
    return _final_norm(x, final_g)


def kernel(x_prompt, x_sample, c_prompt, c_sample, norm_g, mod_w, mod_b, rw_w_in, rw_mu, rw_w0, rw_w2, rw_a0, rw_a2, rw_k_k, rw_k_a, rw_r_k, rw_gn_w, rw_gn_b, rw_w_out, hy_w_in, hy_short_w, hy_short_b, hy_f_w1, hy_f_b1, hy_f_w_mid, hy_f_b_mid, hy_f_freq, hy_f_w_out, hy_f_decay, hy_bias, hy_w_out, cf_w_in, cf_dw_w, cf_dw_b, cf_ln_g, cf_ln_b, cf_w_out, final_g):
    rw = (rw_w_in, rw_mu, rw_w0, rw_w2, rw_a0, rw_a2, rw_k_k, rw_k_a, rw_r_k, rw_gn_w, rw_gn_b, rw_w_out)
    hy = (hy_w_in, hy_short_w, hy_short_b, hy_f_w1, hy_f_b1, hy_f_w_mid, hy_f_b_mid, hy_f_freq, hy_f_w_out,
          hy_f_decay, hy_bias, hy_w_out)
    cf = (cf_w_in, cf_dw_w, cf_dw_b, cf_ln_g, cf_ln_b, cf_w_out)
    consts = _seg_constants()
    bp = x_prompt.shape[0]
    mod = _modulation(jnp.concatenate([c_prompt, c_sample], axis=0), mod_w, mod_b)
    y_prompt = _trunk(x_prompt, mod[:, :bp], norm_g, rw, hy, cf, final_g, consts)
    y_sample = _trunk(x_sample, mod[:, bp:], norm_g, rw, hy, cf, final_g, consts)
    return (y_prompt, y_sample)
```

```python
import functools
import math

import numpy as np
import jax
import jax.numpy as jnp
from jax import lax
from jax.experimental import pallas as pl
from jax.experimental.pallas import tpu as pltpu

F32 = jnp.float32
BF16 = jnp.bfloat16

D_MODEL = 1024
WIDTH = 1024
DEPTH = 4
HEADS = 16
HEAD = 64
LORA = 64
RMS_EPS = 1e-6
LN_EPS = 1e-5
GN_EPS = 64e-5
POS_BANDS = 16
FILTER_HIDDEN = 64
CONV_WIDTH = 31
CONV_HALO = 16
LANES = 128
SEG_GROUP = 256
FFT_N2 = 64
FFT_UNROLL = 8
VMEM_LIMIT = 56 * 1024 * 1024


def _cparams(*sem):
    return pltpu.CompilerParams(dimension_semantics=sem, vmem_limit_bytes=VMEM_LIMIT)


def _dot(a, b):
    return jnp.dot(a, b, preferred_element_type=F32)


def _split(x):
    hi = x.astype(BF16)
    lo = (x - hi.astype(F32)).astype(BF16)
    return hi, lo


def _dot_x3(a, b):
    ah, al = _split(a)
    bh, bl = _split(b)
    return _dot(ah, bh) + _dot(ah, bl) + _dot(al, bh)


def _dot_presplit(a, bh, bl):
    ah, al = _split(a)
    return _dot(ah, bh) + _dot(ah, bl) + _dot(al, bh)


def _seg_bcast(t, bd):
    outs = []
    for g in range(t.shape[1] // SEG_GROUP):
        th, tl = _split(t[:, g * SEG_GROUP:(g + 1) * SEG_GROUP])
        outs.append(_dot(th, bd) + _dot(tl, bd))
    return jnp.concatenate(outs, axis=1)


def _sigmoid(x):
    return 1.0 / (1.0 + jnp.exp(-x))


def _silu(x):
    return x * _sigmoid(x)


def _prenorm(x, ng, shift, scale):
    ms = jnp.mean(x * x, axis=-1, keepdims=True)
    return (x * lax.rsqrt(ms + RMS_EPS) * ng) * (1.0 + scale) + shift


def _mod_kernel(c_ref, w_ref, b_ref, o_ref):
    o_ref[0] = _dot_x3(_silu(c_ref[...]), w_ref[0]) + b_ref[0]


def _modulation(c_all, mod_w, mod_b):
    bc = c_all.shape[0]
    out = pl.pallas_call(
        _mod_kernel,
        grid=(DEPTH, 3),
        in_specs=[
            pl.BlockSpec((bc, D_MODEL), lambda i, j: (0, 0)),
            pl.BlockSpec((1, D_MODEL, D_MODEL), lambda i, j: (i, 0, j)),
            pl.BlockSpec((1, 1, D_MODEL), lambda i, j: (i, 0, j)),
        ],
        out_specs=pl.BlockSpec((1, bc, D_MODEL), lambda i, j: (i, 0, j)),
        out_shape=jax.ShapeDtypeStruct((DEPTH, bc, 3 * D_MODEL), F32),
        compiler_params=_cparams("arbitrary", "arbitrary"),
        name="modulation",
    )(c_all, mod_w, mod_b.reshape(DEPTH, 1, 3 * D_MODEL))
    return out.reshape(DEPTH, bc, 3, D_MODEL)


RW_TL = 256
SCAN_BATCH = 1
SCAN_CHUNK = HEAD


def _rw_proj_kernel(x_ref, xp_ref, xn_ref, mod_ref, ng_ref, mu_ref, win_ref,
                    w0_ref, w2h_ref, w2l_ref, a0_ref, a2h_ref, a2l_ref,
                    kk_ref, ka_ref, rk_ref, bd_ref,
                    r_o, v_o, nkk_o, g_o, bv_o, w0_o, k0_o, b0_o, w1_o, k1_o, b1_o):
    t = pl.program_id(1)
    nt = pl.num_programs(1)
    tl = x_ref.shape[1]
    e = WIDTH
    shift = mod_ref[0, 0:1, :]
    scale = mod_ref[0, 1:2, :]
    ng = ng_ref[...]
    bd = bd_ref[...]

    h = _prenorm(x_ref[0], ng, shift, scale)
    hp = _prenorm(xp_ref[0], ng, shift, scale)[7:8, :]
    hn = _prenorm(xn_ref[0], ng, shift, scale)[0:1, :]
    hp = jnp.where(t > 0, hp, 0.0)
    hn = jnp.where(t < nt - 1, hn, 0.0)
    rows = lax.broadcasted_iota(jnp.int32, (tl, 1), 0)
    h_up = jnp.where(rows == 0, hp, pltpu.roll(h, 1, 0))
    h_dn = jnp.where(rows == tl - 1, hn, pltpu.roll(h, tl - 1, 0))
    xx = 0.5 * (h_up + h_dn) - h

    def proj(i, c0, c1):
        inp = (h + xx * mu_ref[i:i + 1, :]).astype(BF16)
        return _dot(inp, win_ref[:, c0:c1])

    r = proj(0, 0, e)
    k = proj(1, e, 2 * e)
    v = proj(2, 2 * e, 3 * e)
    g = proj(3, 3 * e, 4 * e)
    wl = proj(4, 4 * e, 4 * e + 2 * LORA)
    al = proj(5, 4 * e + 2 * LORA, 4 * e + 4 * LORA)
    twl = jnp.tanh(wl)

    kk0 = k * kk_ref[...]
    nrm = jnp.sqrt(_seg_bcast(kk0 * kk0, bd))
    kk = kk0 / jnp.maximum(nrm, 1e-12)
    ka = ka_ref[...]

    r_o[0] = r
    v_o[0] = v
    g_o[0] = g
    nkk_o[0] = -kk

    ksum = jnp.zeros_like(k)
    for d, (w_o, k_o, b_o) in enumerate(((w0_o, k0_o, b0_o), (w1_o, k1_o, b1_o))):
        w_raw = w0_ref[d:d + 1, :] + _dot_presplit(twl, w2h_ref[d], w2l_ref[d])
        z = -w_raw
        softplus = jnp.maximum(z, 0.0) + jnp.log(1.0 + jnp.exp(-jnp.abs(z)))
        w_log = -softplus - 0.5
        w_o[0] = -jnp.exp(w_log)
        a = _sigmoid(a0_ref[d:d + 1, :] + _dot_presplit(al, a2h_ref[d], a2l_ref[d]))
        k_d = k * (1.0 + (a - 1.0) * ka)
        k_o[0] = k_d
        b_o[0] = kk * a
        ksum = ksum + k_d

    bonus = _seg_bcast(r * ksum * rk_ref[...], bd)
    bv_o[0] = bonus * v


def _dot_nt(a, b):
    return lax.dot_general(a, b, (((1,), (1,)), ((), ())), preferred_element_type=F32)


def _dot_tn(a, b):
    return lax.dot_general(a, b, (((0,), (0,)), ((), ())), preferred_element_type=F32)


def _rw_chunk_kernel(rf, vf, af, wf, kf, bf, rb, vb, ab, wb, kb, bb, bd_ref, yf_o, yb_o, s_ref):
    tb = pl.program_id(1)
    nb, c, _ = rf.shape
    grp = SEG_GROUP
    hpg = grp // HEAD

    @pl.when(tb == 0)
    def _():
        s_ref[...] = jnp.zeros_like(s_ref)

    row = lax.broadcasted_iota(jnp.int32, (c, grp), 0)
    col = lax.broadcasted_iota(jnp.int32, (c, grp), 1) % HEAD
    eye = col == row
    eye_state = (lax.broadcasted_iota(jnp.int32, (HEAD, grp), 0)
                 == lax.broadcasted_iota(jnp.int32, (HEAD, grp), 1) % HEAD)
    tr = lax.broadcasted_iota(jnp.int32, (c, c), 0)
    tcq = lax.broadcasted_iota(jnp.int32, (c, c), 1)
    bd = bd_ref[...]
    bd_f32 = bd.astype(F32)

    def bd_weights(x):
        xh, xl = _split(x)
        return (jnp.concatenate([xh] * hpg, axis=0) * bd, jnp.concatenate([xl] * hpg, axis=0) * bd)

    def pmm(lhs, w, dot=_dot, single=None):
        m = lhs.shape[0]
        lh, ll = _split(lhs)
        stack = [lh, ll] if single is None else [lh, ll, single.astype(BF16)]
        res = dot(jnp.concatenate(stack, axis=0), w[0])
        full = res[:m] + res[m:2 * m] + dot(lh, w[1])
        return full if single is None else (full, res[2 * m:])

    def pmm1(lhs, w):
        return _dot(lhs.astype(BF16), w[0])

    def chain_operands(q, rev, refs):
        r, v, a, lw, k, b = [ref[q] for ref in refs]
        tri = (tcq >= tr if rev else tcq <= tr).astype(BF16)
        p1 = lw.astype(BF16)
        rem = lw - p1.astype(F32)
        p2 = rem.astype(BF16)
        p3 = (rem - p2.astype(F32)).astype(BF16)
        cs = _dot(tri, p1) + _dot(tri, p2) + _dot(tri, p3)
        total = cs[0:1] if rev else cs[c - 1:c]
        e_neg = jnp.exp(-cs)
        e_bar = jnp.exp(total - cs)
        return dict(at=a * jnp.exp(cs - lw), rt=r * jnp.exp(cs), bt=b * e_neg, kt=k * e_neg,
                    bb=b * e_bar, kb=k * e_bar, v=v, gc=jnp.exp(total),
                    strict=(col > row) if rev else (col < row),
                    incl=(col >= row) if rev else (col <= row))

    chains = [(q, rev, refs, y_o)
              for q in range(nb)
              for rev, refs, y_o in ((False, (rf, vf, af, wf, kf, bf), yf_o), (True, (rb, vb, ab, wb, kb, bb), yb_o))]
    ops = [chain_operands(q, rev, refs) for q, rev, refs, _ in chains]
    probs = [(ch, g) for ch in range(len(chains)) for g in range(WIDTH // grp)]
    pick = lambda name: [ops[ch][name][:, g * grp:(g + 1) * grp] for ch, g in probs]
    mask = lambda name: [ops[ch][name] for ch, _ in probs]
    at, rt, bt, kt, bb, kb, vv, gc = (pick(n) for n in ("at", "rt", "bt", "kt", "bb", "kb", "v", "gc"))
    strict, incl = mask("strict"), mask("incl")

    sc_b = [pmm(x, bd_weights(w), _dot_nt, single=y) for x, y, w in zip(at, rt, bt)]
    sc_k = [pmm(x, bd_weights(w), _dot_nt, single=y) for x, y, w in zip(at, rt, kt)]
    mab = [jnp.where(m, s[0], 0.0) for m, s in zip(strict, sc_b)]
    mak = [jnp.where(m, s[0], 0.0) for m, s in zip(strict, sc_k)]
    nrb = [jnp.where(m, s[1], 0.0) for m, s in zip(incl, sc_b)]
    nrk = [jnp.where(m, s[1], 0.0) for m, s in zip(incl, sc_k)]
    inv = [jnp.where(eye, 1.0, m) for m in mab]
    pw = [pmm(m, bd_weights(m)) for m in mab]
    doublings = int(math.log2(c)) - 1
    for it in range(doublings):
        if it == doublings - 1:
            inv = [i + pmm(i, bd_weights(p)) for i, p in zip(inv, pw)]
        else:
            both = [pmm(jnp.concatenate([p, i], axis=0), bd_weights(p)) for i, p in zip(inv, pw)]
            pw = [x[:c] for x in both]
            inv = [i + x[c:] for i, x in zip(inv, both)]
    ah = [pmm(i, bd_weights(x)) for i, x in zip(inv, at)]
    mv = [pmm(x, bd_weights(w), single=y) for x, y, w in zip(mak, nrk, vv)]
    pv = [pmm(i, bd_weights(x[0])) for i, x in zip(inv, mv)]
    rh = [x + pmm1(n, bd_weights(w)) for x, n, w in zip(rt, nrb, ah)]
    yv = [x[1] + pmm1(n, bd_weights(w)) for x, n, w in zip(mv, nrb, pv)]
    dg = [jnp.where(eye_state, x, 0.0) for x in gc]

    xy = [pmm(jnp.concatenate([a_, d_], axis=0), bd_weights(s_ref[ch, g]), single=r_)
          for a_, r_, d_, (ch, g) in zip(ah, rh, dg, probs)]
    x = [t[0] for t in xy]
    p = [x_[:c] + w for x_, w in zip(x, pv)]
    for t, w, (ch, g) in zip(xy, yv, probs):
        q, _, _, y_o = chains[ch]
        y_o[q, :, g * grp:(g + 1) * grp] = t[1] + w
    for x_, p_, v_, b_, k_, (ch, g) in zip(x, p, vv, bb, kb, probs):
        bk_h, bk_l = _split(jnp.concatenate([b_, k_], axis=0))
        pv_h, pv_l = _split(jnp.concatenate([p_, v_], axis=0))
        full = _dot_tn(jnp.concatenate([bk_h, bk_h, bk_l], axis=0),
                       jnp.concatenate([pv_h, pv_l, pv_h], axis=0)) * bd_f32
        upd = full[0:HEAD]
        for hh in range(1, hpg):
            upd = upd + full[hh * HEAD:(hh + 1) * HEAD]
        s_ref[ch, g] = x_[c:] + upd


def _rw_out_kernel(final, yf_ref, yb_ref, bv_ref, g_ref, x_ref, mod_ref, gw_ref, gb_ref, wout_ref, bd_ref,
                   fg_ref, o_ref):
    bd = bd_ref[...]
    y = yf_ref[0] + yb_ref[0]
    mean = _seg_bcast(y, bd) * (1.0 / HEAD)
    yc = y - mean
    var = _seg_bcast(yc * yc, bd) * (1.0 / HEAD)
    yn = yc * lax.rsqrt(var + GN_EPS) * gw_ref[...] + gb_ref[...]
    yo = (yn + bv_ref[0]) * _silu(g_ref[0])
    o = _dot(yo.astype(BF16), wout_ref[...])
    res = x_ref[0] + mod_ref[0, 2:3, :] * o
    if final:
        ms = jnp.mean(res * res, axis=-1, keepdims=True)
        res = res * lax.rsqrt(ms + RMS_EPS) * fg_ref[...]
    o_ref[0] = res


def _pad_lora(w):
    z = jnp.zeros_like(w[0])
    return jnp.stack([jnp.concatenate([w[0], z], axis=0), jnp.concatenate([z, w[1]], axis=0)], axis=0)


def _split_host(w):
    hi = w.astype(BF16)
    lo = (w - hi.astype(F32)).astype(BF16)
    return hi, lo


def _rwkv_layer(x, mod, ng, p, consts, final_g=None):
    (w_in, mu, w0, w2, a0, a2, k_k, k_a, r_k, gn_w, gn_b, w_out) = p
    b, l, d = x.shape
    e = WIDTH
    tl = RW_TL
    nt = l // tl
    bd = consts["bd"]
    w2h, w2l = _split_host(_pad_lora(w2))
    a2h, a2l = _split_host(_pad_lora(a2))
    ncols = w_in.shape[1]

    full = lambda shape: pl.BlockSpec(shape, lambda bi, ti: (0,) * len(shape))
    tile = pl.BlockSpec((1, tl, e), lambda bi, ti: (bi, ti, 0))
    hb = tl // 8
    lb = l // 8
    outs = pl.pallas_call(
        _rw_proj_kernel,
        grid=(b, nt),
        in_specs=[
            tile,
            pl.BlockSpec((1, 8, d), lambda bi, ti: (bi, jnp.maximum(ti * hb - 1, 0), 0)),
            pl.BlockSpec((1, 8, d), lambda bi, ti: (bi, jnp.minimum((ti + 1) * hb, lb - 1), 0)),
            pl.BlockSpec((1, 3, d), lambda bi, ti: (bi, 0, 0)),
            full((1, d)), full((6, d)), full((d, ncols)),
            full((2, e)), full((2, 2 * LORA, e)), full((2, 2 * LORA, e)),
            full((2, e)), full((2, 2 * LORA, e)), full((2, 2 * LORA, e)),
            full((1, e)), full((1, e)), full((1, e)),
            full((SEG_GROUP, SEG_GROUP)),
        ],
        out_specs=[tile] * 11,
        out_shape=[jax.ShapeDtypeStruct((b, l, e), F32)] * 11,
        compiler_params=_cparams("parallel", "arbitrary"),
        name="rwkv_proj",
    )(x, x, x, mod, ng.reshape(1, d), mu, w_in.astype(BF16),
      w0, w2h, w2l, a0, a2h, a2l,
      k_k.reshape(1, e), k_a.reshape(1, e), r_k.reshape(1, e), bd)
    r, v, nkk, g, bv, wd0, kd0, bd0, wd1, kd1, bd1 = outs

    tc = SCAN_CHUNK
    ntc = l // tc
    nb = SCAN_BATCH
    fwd = pl.BlockSpec((nb, tc, e), lambda bi, ti: (bi, ti, 0))
    bwd = pl.BlockSpec((nb, tc, e), lambda bi, ti: (bi, ntc - 1 - ti, 0))
    yf, yb = pl.pallas_call(
        _rw_chunk_kernel,
        grid=(b // nb, ntc),
        in_specs=[fwd] * 6 + [bwd] * 6 + [full((SEG_GROUP, SEG_GROUP))],
        out_specs=[fwd, bwd],
        out_shape=[jax.ShapeDtypeStruct((b, l, e), F32)] * 2,
        scratch_shapes=[pltpu.VMEM((2 * nb, e // SEG_GROUP, HEAD, SEG_GROUP), F32)],
        compiler_params=_cparams("parallel", "arbitrary"),
        name="rwkv_scan",
    )(r, v, nkk, wd0, kd0, bd0, r, v, nkk, wd1, kd1, bd1, bd)

    tlo = 256
    tile_o = pl.BlockSpec((1, tlo, e), lambda bi, ti: (bi, ti, 0))
    final = final_g is not None
    fg = (final_g if final else jnp.ones((d,), F32)).reshape(1, d)
    return pl.pallas_call(
        functools.partial(_rw_out_kernel, final),
        grid=(b, l // tlo),
        in_specs=[tile_o] * 5 + [
            pl.BlockSpec((1, 3, d), lambda bi, ti: (bi, 0, 0)),
            full((1, e)), full((1, e)), full((e, d)), full((SEG_GROUP, SEG_GROUP)), full((1, d)),
        ],
        out_specs=tile_o,
        out_shape=jax.ShapeDtypeStruct((b, l, d), F32),
        compiler_params=_cparams("parallel", "arbitrary"),
        name="rwkv_out",
    )(yf, yb, bv, g, x, mod, gn_w.reshape(1, e), gn_b.reshape(1, e), w_out.astype(BF16), bd, fg)


def _norm_proj_kernel(x_ref, mod_ref, ng_ref, w_ref, o_ref):
    h = _prenorm(x_ref[0], ng_ref[...], mod_ref[0, 0:1, :], mod_ref[0, 1:2, :])
    o_ref[0] = _dot(h.astype(BF16), w_ref[...])


def _norm_proj(x, mod, ng, w, name):
    b, l, d = x.shape
    n = w.shape[1]
    tl = 256
    return pl.pallas_call(
        _norm_proj_kernel,
        grid=(b, l // tl),
        in_specs=[
            pl.BlockSpec((1, tl, d), lambda bi, ti: (bi, ti, 0)),
            pl.BlockSpec((1, 3, d), lambda bi, ti: (bi, 0, 0)),
            pl.BlockSpec((1, d), lambda bi, ti: (0, 0)),
            pl.BlockSpec((d, n), lambda bi, ti: (0, 0)),
        ],
        out_specs=pl.BlockSpec((1, tl, n), lambda bi, ti: (bi, ti, 0)),
        out_shape=jax.ShapeDtypeStruct((b, l, n), F32),
        compiler_params=_cparams("parallel", "arbitrary"),
        name=name,
    )(x, mod, ng.reshape(1, d), w.astype(BF16))


def _short_conv_kernel(u_ref, w_ref, b_ref, o_ref):
    u = u_ref[0]
    l = u.shape[0]
    rows = lax.broadcasted_iota(jnp.int32, (l, 1), 0)
    up = jnp.where(rows == 0, 0.0, pltpu.roll(u, 1, 0))
    dn = jnp.where(rows == l - 1, 0.0, pltpu.roll(u, l - 1, 0))
    o_ref[0] = up * w_ref[0:1, :] + u * w_ref[1:2, :] + dn * w_ref[2:3, :] + b_ref[...]


def _short_conv(proj, w, bias):
    b, l, _ = proj.shape
    n = w.shape[1]
    blk = pl.BlockSpec((1, l, LANES), lambda bi, ci: (bi, 0, ci))
    return pl.pallas_call(
        _short_conv_kernel,
        grid=(b, n // LANES),
        in_specs=[blk,
                  pl.BlockSpec((3, LANES), lambda bi, ci: (0, ci)),
                  pl.BlockSpec((1, LANES), lambda bi, ci: (0, ci))],
        out_specs=blk,
        out_shape=jax.ShapeDtypeStruct((b, l, n), F32),
        compiler_params=_cparams("parallel", "arbitrary"),
        name="hyena_short_conv",
    )(proj, w, bias.reshape(1, n))


def _cmm(st, xre, xim):
    s4, s2 = st
    r = s2.shape[0] // 2
    w = xre.shape[1]
    x = xre if xim is None else jnp.concatenate([xre, xim], axis=1)
    xh, xl = _split(x)
    p4 = _dot(s4, xh)
    p2 = _dot(s2, xl)
    re_x = p4[0:r] + p4[r:2 * r] + p2[0:r]
    im_x = p4[2 * r:3 * r] + p4[3 * r:4 * r] + p2[r:2 * r]
    if xim is None:
        return re_x, im_x
    return re_x[:, :w] - im_x[:, w:], im_x[:, :w] + re_x[:, w:]


def _stacks(mre, mim):
    reh, rel = _split(mre)
    imh, iml = _split(mim)
    return jnp.concatenate([reh, rel, imh, iml], axis=0), jnp.concatenate([reh, imh], axis=0)


def _fft_fwd_outer(src_re, src_im, dst_re, dst_im, fa_refs, n1, n2, k1, scale=None):
    fa = tuple(ref[...] for ref in fa_refs)

    def body(i, carry):
        xre = src_re[pl.ds(i, k1, stride=n2), :]
        xim = None if src_im is None else src_im[pl.ds(i, k1, stride=n2), :]
        if scale is not None:
            xre = xre * scale
        are, aim = _cmm(fa, xre, xim)
        dst_re[pl.ds(i, n1, stride=n2), :] = are
        dst_im[pl.ds(i, n1, stride=n2), :] = aim
        return carry

    lax.fori_loop(0, n2, body, 0, unroll=FFT_UNROLL)


def _twiddled_inner(c2_ref, s2_ref, twr_ref, twi_ref, k):
    tre = twr_ref[pl.ds(k, 1), :]
    tim = twi_ref[pl.ds(k, 1), :]
    c2 = c2_ref[...]
    s2 = s2_ref[...]
    return _stacks(c2 * tre - s2 * tim, c2 * tim + s2 * tre)


def _lconv_kernel(u_ref, x_ref, kf_ref, bias_ref, *refs):
    fa_refs, (c2_ref, s2_ref, twr_ref, twi_ref), fc_refs = refs[0:2], refs[2:6], refs[6:8]
    c1_ref, s1_ref, twtr_ref, twti_ref, o_ref, wr_ref, wi_ref = refs[8:]
    l = u_ref.shape[1]
    n2 = FFT_N2
    n1 = 2 * l // n2
    half = n1 // 2

    _fft_fwd_outer(u_ref.at[0], u_ref.at[1], wr_ref, wi_ref, fa_refs, n1, n2, half)

    fc = tuple(ref[...] for ref in fc_refs)

    def slab(k, carry):
        rows = pl.ds(pl.multiple_of(k * n2, n2), n2)
        g = _twiddled_inner(c2_ref, s2_ref, twr_ref, twi_ref, k)
        xre, xim = _cmm(g, wr_ref[rows, :], wi_ref[rows, :])
        kre = kf_ref[0, 0, rows, :]
        kim = kf_ref[0, 1, rows, :]
        yre = xre * kre - xim * kim
        yim = xre * kim + xim * kre
        are, aim = _cmm(fc, yre, yim)
        wr_ref[rows, :] = are
        wi_ref[rows, :] = aim
        return carry

    lax.fori_loop(0, n1, slab, 0, unroll=FFT_UNROLL)

    bias = bias_ref[0]

    def outer(i, carry):
        tre = twtr_ref[pl.ds(i, 1), :]
        tim = twti_ref[pl.ds(i, 1), :]
        c1 = c1_ref[...]
        s1 = s1_ref[...]
        qs = _stacks(c1 * tre - s1 * tim, -(c1 * tim + s1 * tre))
        yre, yim = _cmm(qs, wr_ref[pl.ds(i, n1, stride=n2), :], wi_ref[pl.ds(i, n1, stride=n2), :])
        rows = pl.ds(i, half, stride=n2)
        u0 = u_ref[0, rows, :]
        u1 = u_ref[1, rows, :]
        o_ref[0, rows, :] = x_ref[0, rows, :] * (yre + u0 * bias)
        o_ref[1, rows, :] = x_ref[1, rows, :] * (yim + u1 * bias)
        return carry

    lax.fori_loop(0, n2, outer, 0, unroll=FFT_UNROLL)


FILTER_CHUNK = 512


def _filter_pos(base, chunk, l):
    m = base + lax.broadcasted_iota(jnp.int32, (chunk, 1), 0)
    is_f = m < l
    pos = jnp.where(is_f, m, 2 * l - m).astype(F32)
    return m, is_f, pos


def _filter_mlp_kernel(l, w1_ref, b1_ref, wm_ref, bm_ref, fr_ref, bands_ref, z_ref):
    chunk = z_ref.shape[0]
    _, _, pos = _filter_pos(pl.program_id(0) * chunk, chunk, l)
    lane = lax.broadcasted_iota(jnp.int32, (1, LANES), 1)
    t = pos / max(l - 1, 1)
    ang = (2.0 * math.pi / l) * pos * bands_ref[...]
    feat = jnp.where(lane == 0, t,
                     jnp.where(lane <= POS_BANDS, jnp.cos(ang),
                               jnp.where(lane <= 2 * POS_BANDS, -jnp.sin(ang), 0.0)))
    z = jnp.sin(fr_ref[0:1, :] * (_dot_x3(feat, w1_ref[...]) + b1_ref[...]))
    for i in range(2):
        z = jnp.sin(fr_ref[i + 1:i + 2, :] * (_dot_x3(z, wm_ref[i]) + bm_ref[i:i + 1, :]))
    z_ref[...] = z


def _filter_kernel(z_ref, wof_ref, wob_ref, dec_ref, *refs):
    fa_refs, (c2_ref, s2_ref, twr_ref, twi_ref, o_ref) = refs[0:2], refs[2:]
    n = o_ref.shape[2]
    l = n // 2
    n2 = FFT_N2
    n1 = n // n2
    chunk = min(FILTER_CHUNK, n)
    nchunk = n // chunk
    wof = _split(wof_ref[...])
    wob = _split(wob_ref[...])
    dec_f = jnp.abs(dec_ref[0, 0:1, :])
    dec_b = jnp.abs(dec_ref[0, 1:2, :])

    def gen(c, asum):
        base = pl.multiple_of(c * chunk, chunk)
        m, is_f, pos = _filter_pos(base, chunk, l)
        t = pos / max(l - 1, 1)
        z = z_ref[pl.ds(base, chunk), :]
        hf = _dot_presplit(z, *wof) * jnp.exp(-t * dec_f)
        hb = _dot_presplit(z, *wob) * jnp.exp(-t * dec_b)
        k2 = jnp.where(is_f, hf, jnp.where(m == l, 0.0, hb))
        o_ref[0, 0, pl.ds(base, chunk), :] = k2
        return asum + jnp.sum(jnp.abs(k2), axis=0, keepdims=True)

    asum = lax.fori_loop(0, nchunk, gen, jnp.zeros((1, LANES), F32))
    inv = 1.0 / asum

    re_ref = o_ref.at[0, 0]
    im_ref = o_ref.at[0, 1]
    _fft_fwd_outer(re_ref, None, re_ref, im_ref, fa_refs, n1, n2, n1, scale=inv)

    def slab(k, carry):
        rows = pl.ds(pl.multiple_of(k * n2, n2), n2)
        g = _twiddled_inner(c2_ref, s2_ref, twr_ref, twi_ref, k)
        xre, xim = _cmm(g, re_ref[rows, :], im_ref[rows, :])
        re_ref[rows, :] = xre
        im_ref[rows, :] = xim
        return carry

    lax.fori_loop(0, n1, slab, 0, unroll=FFT_UNROLL)


def _dft_tables(l):
    n = 2 * l
    n2 = FFT_N2
    n1 = n // n2
    half = n1 // 2

    def cs(rows, cols, period):
        ang = 2.0 * np.pi * ((np.arange(rows)[:, None] * np.arange(cols)[None, :]) % period) / period
        return np.cos(ang), -np.sin(ang)

    def split_np(m):
        m32 = jnp.asarray(m, F32)
        hi = m32.astype(BF16)
        lo = (m32 - hi.astype(F32)).astype(BF16)
        return hi, lo

    def stacks_np(c, s):
        ch, cl = split_np(c)
        sh, sl = split_np(s)
        return [jnp.concatenate([ch, cl, sh, sl], axis=0), jnp.concatenate([ch, sh], axis=0)]

    c1, s1 = cs(n1, n1, n1)
    c2, s2 = cs(n2, n2, n2)
    tw_c, tw_s = cs(n1, n2, n)
    tabs = {}
    tabs["fa_half"] = stacks_np(c1[:, :half], s1[:, :half])
    tabs["fa_full"] = stacks_np(c1, s1)
    tabs["fc"] = stacks_np(c2, -s2)
    tabs["c2"] = jnp.asarray(c2, F32)
    tabs["s2"] = jnp.asarray(s2, F32)
    tabs["twr"] = jnp.asarray(tw_c, F32)
    tabs["twi"] = jnp.asarray(tw_s, F32)
    tabs["c1h"] = jnp.asarray(c1[:half] / n, F32)
    tabs["s1h"] = jnp.asarray(s1[:half] / n, F32)
    tabs["twtr"] = jnp.asarray(tw_c.T, F32)
    tabs["twti"] = jnp.asarray(tw_s.T, F32)
    return tabs


def _const_spec(a, nd_grid):
    shape = a.shape
    if nd_grid == 2:
        return pl.BlockSpec(shape, lambda i, j: (0,) * len(shape))
    return pl.BlockSpec(shape, lambda i: (0,) * len(shape))


def _hyena_filters(l, p, tabs):
    (f_w1, f_b1, f_w_mid, f_b_mid, f_freq, f_w_out, f_decay) = p
    n = 2 * l
    e = WIDTH
    nblk = e // LANES
    w1p = jnp.zeros((LANES, FILTER_HIDDEN), F32).at[:f_w1.shape[0]].set(f_w1)
    bands = np.linspace(1e-4, POS_BANDS - 1, POS_BANDS, dtype=np.float32)
    bl = np.zeros((1, LANES), np.float32)
    bl[0, 1:1 + POS_BANDS] = bands
    bl[0, 1 + POS_BANDS:1 + 2 * POS_BANDS] = bands
    consts = tabs["fa_full"] + [tabs["c2"], tabs["s2"], tabs["twr"], tabs["twi"]]
    chunk = min(FILTER_CHUNK, n)
    mlp_in = [w1p, f_b1.reshape(1, FILTER_HIDDEN), f_w_mid, f_b_mid, f_freq, jnp.asarray(bl)]
    z = pl.pallas_call(
        functools.partial(_filter_mlp_kernel, l),
        grid=(n // chunk,),
        in_specs=[_const_spec(a, 1) for a in mlp_in],
        out_specs=pl.BlockSpec((chunk, FILTER_HIDDEN), lambda c: (c, 0)),
        out_shape=jax.ShapeDtypeStruct((n, FILTER_HIDDEN), F32),
        compiler_params=_cparams("parallel"),
        name="hyena_filter_mlp",
    )(*mlp_in)
    return pl.pallas_call(
        _filter_kernel,
        grid=(2, nblk),
        in_specs=[
            pl.BlockSpec((n, FILTER_HIDDEN), lambda o, j: (0, 0)),
            pl.BlockSpec((FILTER_HIDDEN, LANES), lambda o, j: (0, o * 2 * nblk + j)),
            pl.BlockSpec((FILTER_HIDDEN, LANES), lambda o, j: (0, o * 2 * nblk + nblk + j)),
            pl.BlockSpec((1, 2, LANES), lambda o, j: (o, 0, j)),
        ] + [_const_spec(c, 2) for c in consts],
        out_specs=pl.BlockSpec((1, 2, n, LANES), lambda o, j: (o, 0, 0, j)),
        out_shape=jax.ShapeDtypeStruct((2, 2, n, e), F32),
        compiler_params=_cparams("parallel", "arbitrary"),
        name="hyena_filter",
    )(z, f_w_out, f_w_out, f_decay.reshape(2, 2, e), *consts)


def _long_conv(u, xg, kf, order, bias, tabs, col_u, col_x):
    b, l, _ = u.shape
    e = WIDTH
    n = 2 * l
    nblk = e // LANES
    consts = (tabs["fa_half"] + [tabs["c2"], tabs["s2"], tabs["twr"], tabs["twi"]]
              + tabs["fc"] + [tabs["c1h"], tabs["s1h"], tabs["twtr"], tabs["twti"]])
    return pl.pallas_call(
        _lconv_kernel,
        grid=(nblk, b // 2),
        in_specs=[
            pl.BlockSpec((2, l, LANES), lambda j, pi: (pi, 0, col_u * nblk + j)),
            pl.BlockSpec((2, l, LANES), lambda j, pi: (pi, 0, col_x * nblk + j)),
            pl.BlockSpec((1, 2, n, LANES), lambda j, pi: (order, 0, 0, j)),
            pl.BlockSpec((1, 1, LANES), lambda j, pi: (order, 0, j)),
        ] + [_const_spec(c, 2) for c in consts],
        out_specs=pl.BlockSpec((2, l, LANES), lambda j, pi: (pi, 0, j)),
        out_shape=jax.ShapeDtypeStruct((b, l, e), F32),
        scratch_shapes=[pltpu.VMEM((n, LANES), F32), pltpu.VMEM((n, LANES), F32)],
        compiler_params=_cparams("parallel", "arbitrary"),
        name="hyena_long_conv",
    )(u, xg, kf, bias.reshape(2, 1, e), *consts)


def _gated_out_kernel(z_ref, g_ref, x_ref, mod_ref, w_ref, o_ref):
    zz = (z_ref[0] * _silu(g_ref[0])).astype(BF16)
    o_ref[0] = x_ref[0] + mod_ref[0, 2:3, :] * _dot(zz, w_ref[...])


def _gated_out(z, g_arr, g_col, x, mod, w_out, name):
    b, l, d = x.shape
    e = WIDTH
    tl = 256
    tile = lambda col: pl.BlockSpec((1, tl, e), lambda bi, ti: (bi, ti, col))
    return pl.pallas_call(
        _gated_out_kernel,
        grid=(b, l // tl),
        in_specs=[tile(0), tile(g_col), tile(0),
                  pl.BlockSpec((1, 3, d), lambda bi, ti: (bi, 0, 0)),
                  pl.BlockSpec((e, d), lambda bi, ti: (0, 0))],
        out_specs=tile(0),
        out_shape=jax.ShapeDtypeStruct((b, l, d), F32),
        compiler_params=_cparams("parallel", "arbitrary"),
        name=name,
    )(z, g_arr, x, mod, w_out.astype(BF16))


def _hyena_layer(x, mod, ng, p, tabs):
    (w_in, short_w, short_b, f_w1, f_b1, f_w_mid, f_b_mid, f_freq, f_w_out, f_decay, bias, w_out) = p
    l = x.shape[1]
    proj = _norm_proj(x, mod, ng, w_in, "hyena_proj")
    u3 = _short_conv(proj, short_w, short_b)
    kf = _hyena_filters(l, (f_w1, f_b1, f_w_mid, f_b_mid, f_freq, f_w_out, f_decay), tabs)
    z1 = _long_conv(u3, u3, kf, 0, bias, tabs, col_u=2, col_x=0)
    z2 = _long_conv(z1, u3, kf, 1, bias, tabs, col_u=0, col_x=1)
    return _gated_out(z2, proj, 3, x, mod, w_out, "hyena_out")


def _cf_proj_kernel(x_ref, mod_ref, ng_ref, w_ref, u_o, sg_o):
    e = WIDTH
    h = _prenorm(x_ref[0], ng_ref[...], mod_ref[0, 0:1, :], mod_ref[0, 1:2, :]).astype(BF16)
    a = _dot(h, w_ref[:, 0:e])
    bgate = _dot(h, w_ref[:, e:2 * e])
    g = _dot(h, w_ref[:, 2 * e:3 * e])
    u_o[0] = a * _sigmoid(bgate)
    sg_o[0] = _silu(g)


def _cf_out_kernel(u_ref, up_ref, un_ref, sg_ref, x_ref, mod_ref, dw_ref, db_ref, lg_ref, lb_ref, w_ref,
                   o_ref, ext_ref):
    t = pl.program_id(1)
    nt = pl.num_programs(1)
    tl = u_ref.shape[1]
    hh = CONV_HALO
    ext_ref[0:hh, :] = jnp.where(t > 0, up_ref[0], 0.0)
    ext_ref[hh:hh + tl, :] = u_ref[0]
    ext_ref[hh + tl:hh + tl + hh, :] = jnp.where(t < nt - 1, un_ref[0], 0.0)
    off = hh - (CONV_WIDTH - 1) // 2
    acc = jnp.zeros((tl, WIDTH), F32) + db_ref[...]
    ext = ext_ref[...]
    nrow = ext.shape[0]
    for phase in range(8):
        win = ext if phase == 0 else pltpu.roll(ext, nrow - phase, 0)
        for k in range(CONV_WIDTH):
            if (off + k) % 8 == phase:
                base = off + k - phase
                acc = acc + win[base:base + tl, :] * dw_ref[k:k + 1, :]
    mean = jnp.mean(acc, axis=-1, keepdims=True)
    cen = acc - mean
    var = jnp.mean(cen * cen, axis=-1, keepdims=True)
    y = cen * lax.rsqrt(var + LN_EPS) * lg_ref[...] + lb_ref[...]
    zz = (_silu(y) * sg_ref[0]).astype(BF16)
    o_ref[0] = x_ref[0] + mod_ref[0, 2:3, :] * _dot(zz, w_ref[...])


def _conformer_layer(x, mod, ng, p):
    (w_in, dw_w, dw_b, ln_g, ln_b, w_out) = p
    b, l, d = x.shape
    e = WIDTH
    tl = 256
    tile = pl.BlockSpec((1, tl, e), lambda bi, ti: (bi, ti, 0))
    modspec = pl.BlockSpec((1, 3, d), lambda bi, ti: (bi, 0, 0))
    vec = pl.BlockSpec((1, e), lambda bi, ti: (0, 0))
    u, sg = pl.pallas_call(
        _cf_proj_kernel,
        grid=(b, l // tl),
        in_specs=[tile, modspec, vec, pl.BlockSpec((d, 3 * e), lambda bi, ti: (0, 0))],
        out_specs=[tile, tile],
        out_shape=[jax.ShapeDtypeStruct((b, l, e), F32)] * 2,
        compiler_params=_cparams("parallel", "arbitrary"),
        name="conformer_proj",
    )(x, mod, ng.reshape(1, d), w_in.astype(BF16))

    hh = CONV_HALO
    hb = tl // hh
    lb = l // hh
    return pl.pallas_call(
        _cf_out_kernel,
        grid=(b, l // tl),
        in_specs=[
            tile,
            pl.BlockSpec((1, hh, e), lambda bi, ti: (bi, jnp.maximum(ti * hb - 1, 0), 0)),
            pl.BlockSpec((1, hh, e), lambda bi, ti: (bi, jnp.minimum((ti + 1) * hb, lb - 1), 0)),
            tile, tile, modspec,
            pl.BlockSpec((CONV_WIDTH, e), lambda bi, ti: (0, 0)),
            vec, vec, vec,
            pl.BlockSpec((e, d), lambda bi, ti: (0, 0)),
        ],
        out_specs=tile,
        out_shape=jax.ShapeDtypeStruct((b, l, d), F32),
        scratch_shapes=[pltpu.VMEM((tl + 2 * hh, e), F32)],
        compiler_params=_cparams("parallel", "arbitrary"),
        name="conformer_out",
    )(u, u, u, sg, x, mod, dw_w, dw_b.reshape(1, e), ln_g.reshape(1, e), ln_b.reshape(1, e),
      w_out.astype(BF16))


def _final_norm_kernel(x_ref, g_ref, o_ref):
    x = x_ref[0]
    ms = jnp.mean(x * x, axis=-1, keepdims=True)
    o_ref[0] = x * lax.rsqrt(ms + RMS_EPS) * g_ref[...]


def _final_norm(x, g):
    b, l, d = x.shape
    tl = 512
    tile = pl.BlockSpec((1, tl, d), lambda bi, ti: (bi, ti, 0))
    return pl.pallas_call(
        _final_norm_kernel,
        grid=(b, l // tl),
        in_specs=[tile, pl.BlockSpec((1, d), lambda bi, ti: (0, 0))],
        out_specs=tile,
        out_shape=jax.ShapeDtypeStruct((b, l, d), F32),
        compiler_params=_cparams("parallel", "arbitrary"),
        name="final_norm",
    )(x, g.reshape(1, d))


def _seg_constants():
    seg = np.arange(SEG_GROUP) // HEAD
    bd = (seg[:, None] == seg[None, :]).astype(np.float32)
    return {"bd": jnp.asarray(bd, BF16)}


def _trunk(x, mod, norm_g, rw, hy, cf, final_g, consts):
    tabs = None
    for i in range(DEPTH):
        kind, j = i % 3, i // 3
        m = mod[i]
        last = i == DEPTH - 1
        if kind == 0:
            x = _rwkv_layer(x, m, norm_g[i], [p[j] for p in rw], consts, final_g if last else None)
            if last:
                return x
        elif kind == 1:
            if tabs is None:
                tabs = _dft_tables(x.shape[1])
            x = _hyena_layer(x, m, norm_g[i], [p[j] for p in hy], tabs)
        else:
            x = _conformer_layer(x, m, norm_g[i], [p[j] for p in cf])
    return _final_norm(x, final_g)


def kernel(x_prompt, x_sample, c_prompt, c_sample, norm_g, mod_w, mod_b, rw_w_in, rw_mu, rw_w0, rw_w2, rw_a0, rw_a2, rw_k_k, rw_k_a, rw_r_k, rw_gn_w, rw_gn_b, rw_w_out, hy_w_in, hy_short_w, hy_short_b, hy_f_w1, hy_f_b1, hy_f_w_mid, hy_f_b_mid, hy_f_freq, hy_f_w_out, hy_f_decay, hy_bias, hy_w_out, cf_w_in, cf_dw_w, cf_dw_b, cf_ln_g, cf_ln_b, cf_w_out, final_g):
    rw = (rw_w_in, rw_mu, rw_w0, rw_w2, rw_a0, rw_a2, rw_k_k, rw_k_a, rw_r_k, rw_gn_w, rw_gn_b, rw_w_out)
    hy = (hy_w_in, hy_short_w, hy_short_b, hy_f_w1, hy_f_b1, hy_f_w_mid, hy_f_b_mid, hy_f_freq, hy_f_w_out,
          hy_f_decay, hy_bias, hy_w_out)
    cf = (cf_w_in, cf_dw_w, cf_dw_b, cf_ln_g, cf_ln_b, cf_w_out)
    consts = _seg_constants()
    bp = x_prompt.shape[0]
    mod = _modulation(jnp.concatenate([c_prompt, c_sample], axis=0), mod_w, mod_b)
    y_prompt = _trunk(x_prompt, mod[:, :bp], norm_g, rw, hy, cf, final_g, consts)
    y_sample = _trunk(x_sample, mod[:, bp:], norm_g, rw, hy, cf, final_g, consts)
    return (y_prompt, y_sample)
```

```python
import functools
import math

import numpy as np
import jax
import jax.numpy as jnp
from jax import lax
from jax.experimental import pallas as pl
from jax.experimental.pallas import tpu as pltpu

F32 = jnp.float32
BF16 = jnp.bfloat16

D_MODEL = 1024
WIDTH = 1024
DEPTH = 4
HEADS = 16
HEAD = 64
LORA = 64
RMS_EPS = 1e-6
LN_EPS = 1e-5
GN_EPS = 64e-5
POS_BANDS = 16
FILTER_HIDDEN = 64
CONV_WIDTH = 31
CONV_HALO = 16
LANES = 128
SEG_GROUP = 256
FFT_N2 = 64
FFT_UNROLL = 8
VMEM_LIMIT = 56 * 1024 * 1024


def _cparams(*sem):
    return pltpu.CompilerParams(dimension_semantics=sem, vmem_limit_bytes=VMEM_LIMIT)


def _dot(a, b):
    return jnp.dot(a, b, preferred_element_type=F32)


def _split(x):
    hi = x.astype(BF16)
    lo = (x - hi.astype(F32)).astype(BF16)
    return hi, lo


def _dot_x3(a, b):
    ah, al = _split(a)
    bh, bl = _split(b)
    return _dot(ah, bh) + _dot(ah, bl) + _dot(al, bh)


def _dot_presplit(a, bh, bl):
    ah, al = _split(a)
    return _dot(ah, bh) + _dot(ah, bl) + _dot(al, bh)


def _dot_kstacked(a, b_stack):
    ah, al = _split(a)
    return _dot(jnp.concatenate([ah, al, ah], axis=1), b_stack)


def _seg_bcast(t, bd):
    outs = []
    for g in range(t.shape[1] // SEG_GROUP):
        th, tl = _split(t[:, g * SEG_GROUP:(g + 1) * SEG_GROUP])
        outs.append(_dot(th, bd) + _dot(tl, bd))
    return jnp.concatenate(outs, axis=1)


def _sigmoid(x):
    return 1.0 / (1.0 + jnp.exp(-x))


def _silu(x):
    return x * _sigmoid(x)


def _prenorm(x, ng, shift, scale):
    ms = jnp.mean(x * x, axis=-1, keepdims=True)
    return (x * lax.rsqrt(ms + RMS_EPS) * ng) * (1.0 + scale) + shift


def _mod_kernel(c_ref, w_ref, b_ref, o_ref):
    o_ref[0] = _dot_x3(_silu(c_ref[...]), w_ref[0]) + b_ref[0]


def _modulation(c_all, mod_w, mod_b):
    bc = c_all.shape[0]
    out = pl.pallas_call(
        _mod_kernel,
        grid=(DEPTH, 3),
        in_specs=[
            pl.BlockSpec((bc, D_MODEL), lambda i, j: (0, 0)),
            pl.BlockSpec((1, D_MODEL, D_MODEL), lambda i, j: (i, 0, j)),
            pl.BlockSpec((1, 1, D_MODEL), lambda i, j: (i, 0, j)),
        ],
        out_specs=pl.BlockSpec((1, bc, D_MODEL), lambda i, j: (i, 0, j)),
        out_shape=jax.ShapeDtypeStruct((DEPTH, bc, 3 * D_MODEL), F32),
        compiler_params=_cparams("arbitrary", "arbitrary"),
        name="modulation",
    )(c_all, mod_w, mod_b.reshape(DEPTH, 1, 3 * D_MODEL))
    return out.reshape(DEPTH, bc, 3, D_MODEL)


RW_TL = 256
SCAN_BATCH = 1
SCAN_CHUNK = HEAD


def _rw_proj_kernel(x_ref, xp_ref, xn_ref, mod_ref, ng_ref, mu_ref, win_ref,
                    w0_ref, w2_ref, a0_ref, a2_ref,
                    kk_ref, ka_ref, rk_ref, bd_ref,
                    r_o, v_o, nkk_o, g_o, bv_o, w0_o, k0_o, b0_o, w1_o, k1_o, b1_o):
    t = pl.program_id(1)
    nt = pl.num_programs(1)
    tl = x_ref.shape[1]
    e = WIDTH
    shift = mod_ref[0, 0:1, :]
    scale = mod_ref[0, 1:2, :]
    ng = ng_ref[...]
    bd = bd_ref[...]

    h = _prenorm(x_ref[0], ng, shift, scale)
    hp = _prenorm(xp_ref[0], ng, shift, scale)[7:8, :]
    hn = _prenorm(xn_ref[0], ng, shift, scale)[0:1, :]
    hp = jnp.where(t > 0, hp, 0.0)
    hn = jnp.where(t < nt - 1, hn, 0.0)
    rows = lax.broadcasted_iota(jnp.int32, (tl, 1), 0)
    h_up = jnp.where(rows == 0, hp, pltpu.roll(h, 1, 0))
    h_dn = jnp.where(rows == tl - 1, hn, pltpu.roll(h, tl - 1, 0))
    xx = 0.5 * (h_up + h_dn) - h

    def proj(i, c0, c1):
        inp = (h + xx * mu_ref[i:i + 1, :]).astype(BF16)
        return _dot(inp, win_ref[:, c0:c1])

    r = proj(0, 0, e)
    k = proj(1, e, 2 * e)
    v = proj(2, 2 * e, 3 * e)
    g = proj(3, 3 * e, 4 * e)
    wl = proj(4, 4 * e, 4 * e + 2 * LORA)
    al = proj(5, 4 * e + 2 * LORA, 4 * e + 4 * LORA)
    twl = jnp.tanh(wl)

    kk0 = k * kk_ref[...]
    nrm = jnp.sqrt(_seg_bcast(kk0 * kk0, bd))
    kk = kk0 / jnp.maximum(nrm, 1e-12)
    ka = ka_ref[...]

    r_o[0] = r
    v_o[0] = v
    g_o[0] = g
    nkk_o[0] = -kk

    ksum = jnp.zeros_like(k)
    for d, (w_o, k_o, b_o) in enumerate(((w0_o, k0_o, b0_o), (w1_o, k1_o, b1_o))):
        w_raw = w0_ref[d:d + 1, :] + _dot_kstacked(twl, w2_ref[d])
        z = -w_raw
        softplus = jnp.maximum(z, 0.0) + jnp.log(1.0 + jnp.exp(-jnp.abs(z)))
        w_log = -softplus - 0.5
        w_o[0] = -jnp.exp(w_log)
        a = _sigmoid(a0_ref[d:d + 1, :] + _dot_kstacked(al, a2_ref[d]))
        k_d = k * (1.0 + (a - 1.0) * ka)
        k_o[0] = k_d
        b_o[0] = kk * a
        ksum = ksum + k_d

    bonus = _seg_bcast(r * ksum * rk_ref[...], bd)
    bv_o[0] = bonus * v


def _dot_nt(a, b):
    return lax.dot_general(a, b, (((1,), (1,)), ((), ())), preferred_element_type=F32)


def _dot_tn(a, b):
    return lax.dot_general(a, b, (((0,), (0,)), ((), ())), preferred_element_type=F32)


def _rw_chunk_kernel(rf, vf, af, wf, kf, bf, rb, vb, ab, wb, kb, bb, bd_ref, yf_o, yb_o, s_ref):
    tb = pl.program_id(1)
    nb, c, _ = rf.shape
    grp = SEG_GROUP
    hpg = grp // HEAD

    @pl.when(tb == 0)
    def _():
        s_ref[...] = jnp.zeros_like(s_ref)

    row = lax.broadcasted_iota(jnp.int32, (c, grp), 0)
    col = lax.broadcasted_iota(jnp.int32, (c, grp), 1) % HEAD
    eye = col == row
    eye_state = (lax.broadcasted_iota(jnp.int32, (HEAD, grp), 0)
                 == lax.broadcasted_iota(jnp.int32, (HEAD, grp), 1) % HEAD)
    tr = lax.broadcasted_iota(jnp.int32, (c, 3 * c), 0)
    tcq = lax.broadcasted_iota(jnp.int32, (c, 3 * c), 1) % c
    bd = bd_ref[...]
    bd_f32 = bd.astype(F32)

    def bd_weights(x):
        xh, xl = _split(x)
        return (jnp.concatenate([xh] * hpg, axis=0) * bd, jnp.concatenate([xl] * hpg, axis=0) * bd)

    def pmm(lhs, w, dot=_dot, single=None):
        m = lhs.shape[0]
        lh, ll = _split(lhs)
        stack = [lh, ll] if single is None else [lh, ll, single.astype(BF16)]
        res = dot(jnp.concatenate(stack, axis=0), w[0])
        full = res[:m] + res[m:2 * m] + dot(lh, w[1])
        return full if single is None else (full, res[2 * m:])

    def pmm1(lhs, w):
        return _dot(lhs.astype(BF16), w[0])

    def chain_operands(q, rev, refs):
        r, v, a, lw, k, b = [ref[q] for ref in refs]
        tri = (tcq >= tr if rev else tcq <= tr).astype(BF16)
        p1 = lw.astype(BF16)
        rem = lw - p1.astype(F32)
        p2 = rem.astype(BF16)
        p3 = (rem - p2.astype(F32)).astype(BF16)
        cs = _dot(tri, jnp.concatenate([p1, p2, p3], axis=0))
        total = cs[0:1] if rev else cs[c - 1:c]
        e_neg = jnp.exp(-cs)
        e_bar = jnp.exp(total - cs)
        return dict(at=a * jnp.exp(cs - lw), rt=r * jnp.exp(cs), bt=b * e_neg, kt=k * e_neg,
                    bb=b * e_bar, kb=k * e_bar, v=v, gc=jnp.exp(total),
                    strict=(col > row) if rev else (col < row),
                    incl=(col >= row) if rev else (col <= row))

    chains = [(q, rev, refs, y_o)
              for q in range(nb)
              for rev, refs, y_o in ((False, (rf, vf, af, wf, kf, bf), yf_o), (True, (rb, vb, ab, wb, kb, bb), yb_o))]
    ops = [chain_operands(q, rev, refs) for q, rev, refs, _ in chains]
    probs = [(ch, g) for ch in range(len(chains)) for g in range(WIDTH // grp)]
    pick = lambda name: [ops[ch][name][:, g * grp:(g + 1) * grp] for ch, g in probs]
    mask = lambda name: [ops[ch][name] for ch, _ in probs]
    at, rt, bt, kt, bb, kb, vv, gc = (pick(n) for n in ("at", "rt", "bt", "kt", "bb", "kb", "v", "gc"))
    strict, incl = mask("strict"), mask("incl")

    sc_b = [pmm(x, bd_weights(w), _dot_nt, single=y) for x, y, w in zip(at, rt, bt)]
    sc_k = [pmm(x, bd_weights(w), _dot_nt, single=y) for x, y, w in zip(at, rt, kt)]
    mab = [jnp.where(m, s[0], 0.0) for m, s in zip(strict, sc_b)]
    mak = [jnp.where(m, s[0], 0.0) for m, s in zip(strict, sc_k)]
    nrb = [jnp.where(m, s[1], 0.0) for m, s in zip(incl, sc_b)]
    nrk = [jnp.where(m, s[1], 0.0) for m, s in zip(incl, sc_k)]
    inv = [jnp.where(eye, 1.0, m) for m in mab]
    pw = [pmm(m, bd_weights(m)) for m in mab]
    doublings = int(math.log2(c)) - 1
    for it in range(doublings):
        if it == doublings - 1:
            inv = [i + pmm(i, bd_weights(p)) for i, p in zip(inv, pw)]
        else:
            both = [pmm(jnp.concatenate([p, i], axis=0), bd_weights(p)) for i, p in zip(inv, pw)]
            pw = [x[:c] for x in both]
            inv = [i + x[c:] for i, x in zip(inv, both)]
    mv = [pmm(x, bd_weights(w), single=y) for x, y, w in zip(mak, nrk, vv)]
    dg = [jnp.where(eye_state, x, 0.0) for x in gc]

    xy = [pmm(jnp.concatenate([a_, d_], axis=0), bd_weights(s_ref[ch, g]), single=r_)
          for a_, r_, d_, (ch, g) in zip(at, rt, dg, probs)]
    x = [t[0] for t in xy]
    p = [pmm(i, bd_weights(x_[:c] + z[0])) for i, x_, z in zip(inv, x, mv)]
    for t, z, n, p_, (ch, g) in zip(xy, mv, nrb, p, probs):
        q, _, _, y_o = chains[ch]
        y_o[q, :, g * grp:(g + 1) * grp] = t[1] + z[1] + pmm1(n, bd_weights(p_))
    for x_, p_, v_, b_, k_, (ch, g) in zip(x, p, vv, bb, kb, probs):
        bk_h, bk_l = _split(jnp.concatenate([b_, k_], axis=0))
        pv_h, pv_l = _split(jnp.concatenate([p_, v_], axis=0))
        full = _dot_tn(jnp.concatenate([bk_h, bk_h, bk_l], axis=0),
                       jnp.concatenate([pv_h, pv_l, pv_h], axis=0)) * bd_f32
        upd = full[0:HEAD]
        for hh in range(1, hpg):
            upd = upd + full[hh * HEAD:(hh + 1) * HEAD]
        s_ref[ch, g] = x_[c:] + upd


def _rw_out_kernel(final, yf_ref, yb_ref, bv_ref, g_ref, x_ref, mod_ref, gw_ref, gb_ref, wout_ref, bd_ref,
                   fg_ref, o_ref):
    bd = bd_ref[...]
    y = yf_ref[0] + yb_ref[0]
    mean = _seg_bcast(y, bd) * (1.0 / HEAD)
    yc = y - mean
    var = _seg_bcast(yc * yc, bd) * (1.0 / HEAD)
    yn = yc * lax.rsqrt(var + GN_EPS) * gw_ref[...] + gb_ref[...]
    yo = (yn + bv_ref[0]) * _silu(g_ref[0])
    o = _dot(yo.astype(BF16), wout_ref[...])
    res = x_ref[0] + mod_ref[0, 2:3, :] * o
    if final:
        ms = jnp.mean(res * res, axis=-1, keepdims=True)
        res = res * lax.rsqrt(ms + RMS_EPS) * fg_ref[...]
    o_ref[0] = res


def _pad_lora(w):
    z = jnp.zeros_like(w[0])
    return jnp.stack([jnp.concatenate([w[0], z], axis=0), jnp.concatenate([z, w[1]], axis=0)], axis=0)


def _split_host(w):
    hi = w.astype(BF16)
    lo = (w - hi.astype(F32)).astype(BF16)
    return hi, lo


def _rwkv_layer(x, mod, ng, p, consts, final_g=None):
    (w_in, mu, w0, w2, a0, a2, k_k, k_a, r_k, gn_w, gn_b, w_out) = p
    b, l, d = x.shape
    e = WIDTH
    tl = RW_TL
    nt = l // tl
    bd = consts["bd"]
    def lora_stack(w):
        hi, lo = _split_host(_pad_lora(w))
        return jnp.concatenate([hi, hi, lo], axis=1)

    w2s, a2s = lora_stack(w2), lora_stack(a2)
    ncols = w_in.shape[1]

    full = lambda shape: pl.BlockSpec(shape, lambda bi, ti: (0,) * len(shape))
    tile = pl.BlockSpec((1, tl, e), lambda bi, ti: (bi, ti, 0))
    hb = tl // 8
    lb = l // 8
    outs = pl.pallas_call(
        _rw_proj_kernel,
        grid=(b, nt),
        in_specs=[
            tile,
            pl.BlockSpec((1, 8, d), lambda bi, ti: (bi, jnp.maximum(ti * hb - 1, 0), 0)),
            pl.BlockSpec((1, 8, d), lambda bi, ti: (bi, jnp.minimum((ti + 1) * hb, lb - 1), 0)),
            pl.BlockSpec((1, 3, d), lambda bi, ti: (bi, 0, 0)),
            full((1, d)), full((6, d)), full((d, ncols)),
            full((2, e)), full((2, 6 * LORA, e)),
            full((2, e)), full((2, 6 * LORA, e)),
            full((1, e)), full((1, e)), full((1, e)),
            full((SEG_GROUP, SEG_GROUP)),
        ],
        out_specs=[tile] * 11,
        out_shape=[jax.ShapeDtypeStruct((b, l, e), F32)] * 11,
        compiler_params=_cparams("parallel", "arbitrary"),
        name="rwkv_proj",
    )(x, x, x, mod, ng.reshape(1, d), mu, w_in.astype(BF16),
      w0, w2s, a0, a2s,
      k_k.reshape(1, e), k_a.reshape(1, e), r_k.reshape(1, e), bd)
    r, v, nkk, g, bv, wd0, kd0, bd0, wd1, kd1, bd1 = outs

    tc = SCAN_CHUNK
    ntc = l // tc
    nb = SCAN_BATCH
    fwd = pl.BlockSpec((nb, tc, e), lambda bi, ti: (bi, ti, 0))
    bwd = pl.BlockSpec((nb, tc, e), lambda bi, ti: (bi, ntc - 1 - ti, 0))
    yf, yb = pl.pallas_call(
        _rw_chunk_kernel,
        grid=(b // nb, ntc),
        in_specs=[fwd] * 6 + [bwd] * 6 + [full((SEG_GROUP, SEG_GROUP))],
        out_specs=[fwd, bwd],
        out_shape=[jax.ShapeDtypeStruct((b, l, e), F32)] * 2,
        scratch_shapes=[pltpu.VMEM((2 * nb, e // SEG_GROUP, HEAD, SEG_GROUP), F32)],
        compiler_params=_cparams("parallel", "arbitrary"),
        name="rwkv_scan",
    )(r, v, nkk, wd0, kd0, bd0, r, v, nkk, wd1, kd1, bd1, bd)

    tlo = 256
    tile_o = pl.BlockSpec((1, tlo, e), lambda bi, ti: (bi, ti, 0))
    final = final_g is not None
    fg = (final_g if final else jnp.ones((d,), F32)).reshape(1, d)
    return pl.pallas_call(
        functools.partial(_rw_out_kernel, final),
        grid=(b, l // tlo),
        in_specs=[tile_o] * 5 + [
            pl.BlockSpec((1, 3, d), lambda bi, ti: (bi, 0, 0)),
            full((1, e)), full((1, e)), full((e, d)), full((SEG_GROUP, SEG_GROUP)), full((1, d)),
        ],
        out_specs=tile_o,
        out_shape=jax.ShapeDtypeStruct((b, l, d), F32),
        compiler_params=_cparams("parallel", "arbitrary"),
        name="rwkv_out",
    )(yf, yb, bv, g, x, mod, gn_w.reshape(1, e), gn_b.reshape(1, e), w_out.astype(BF16), bd, fg)


def _norm_proj_kernel(x_ref, mod_ref, ng_ref, w_ref, o_ref):
    h = _prenorm(x_ref[0], ng_ref[...], mod_ref[0, 0:1, :], mod_ref[0, 1:2, :])
    o_ref[0] = _dot(h.astype(BF16), w_ref[...])


def _norm_proj(x, mod, ng, w, name):
    b, l, d = x.shape
    n = w.shape[1]
    tl = 256
    return pl.pallas_call(
        _norm_proj_kernel,
        grid=(b, l // tl),
        in_specs=[
            pl.BlockSpec((1, tl, d), lambda bi, ti: (bi, ti, 0)),
            pl.BlockSpec((1, 3, d), lambda bi, ti: (bi, 0, 0)),
            pl.BlockSpec((1, d), lambda bi, ti: (0, 0)),
            pl.BlockSpec((d, n), lambda bi, ti: (0, 0)),
        ],
        out_specs=pl.BlockSpec((1, tl, n), lambda bi, ti: (bi, ti, 0)),
        out_shape=jax.ShapeDtypeStruct((b, l, n), F32),
        compiler_params=_cparams("parallel", "arbitrary"),
        name=name,
    )(x, mod, ng.reshape(1, d), w.astype(BF16))


def _short_conv_kernel(u_ref, w_ref, b_ref, o_ref):
    u = u_ref[0]
    l = u.shape[0]
    rows = lax.broadcasted_iota(jnp.int32, (l, 1), 0)
    up = jnp.where(rows == 0, 0.0, pltpu.roll(u, 1, 0))
    dn = jnp.where(rows == l - 1, 0.0, pltpu.roll(u, l - 1, 0))
    o_ref[0] = up * w_ref[0:1, :] + u * w_ref[1:2, :] + dn * w_ref[2:3, :] + b_ref[...]


def _short_conv(proj, w, bias):
    b, l, _ = proj.shape
    n = w.shape[1]
    blk = pl.BlockSpec((1, l, LANES), lambda bi, ci: (bi, 0, ci))
    return pl.pallas_call(
        _short_conv_kernel,
        grid=(b, n // LANES),
        in_specs=[blk,
                  pl.BlockSpec((3, LANES), lambda bi, ci: (0, ci)),
                  pl.BlockSpec((1, LANES), lambda bi, ci: (0, ci))],
        out_specs=blk,
        out_shape=jax.ShapeDtypeStruct((b, l, n), F32),
        compiler_params=_cparams("parallel", "arbitrary"),
        name="hyena_short_conv",
    )(proj, w, bias.reshape(1, n))


def _cmm(lhs, xre, xim):
    r = lhs.shape[0] // 2
    w = xre.shape[1]
    x = xre if xim is None else jnp.concatenate([xre, xim], axis=1)
    xh, xl = _split(x)
    out = _dot(lhs, jnp.concatenate([xh, xl, xh], axis=0))
    re_x, im_x = out[:r], out[r:]
    if xim is None:
        return re_x, im_x
    return re_x[:, :w] - im_x[:, w:], im_x[:, :w] + re_x[:, w:]


def _kstack(mre, mim, k):
    m = jnp.concatenate([mre, mim], axis=0)
    hi, lo = _split(m)
    if m.shape[1] == k:
        return jnp.concatenate([hi, hi, lo], axis=1)
    lane = lax.broadcasted_iota(jnp.int32, m.shape, 1)
    return jnp.where(lane < 2 * k, hi, lo)


def _fft_fwd_outer(src_re, src_im, dst_re, dst_im, fa_ref, n1, n2, k1, scale=None):
    fa = fa_ref[...]

    def body(i, carry):
        xre = src_re[pl.ds(i, k1, stride=n2), :]
        xim = None if src_im is None else src_im[pl.ds(i, k1, stride=n2), :]
        if scale is not None:
            xre = xre * scale
        are, aim = _cmm(fa, xre, xim)
        dst_re[pl.ds(i, n1, stride=n2), :] = are
        dst_im[pl.ds(i, n1, stride=n2), :] = aim
        return carry

    lax.fori_loop(0, n2, body, 0, unroll=FFT_UNROLL)


def _twiddled_inner(c2_ref, s2_ref, twr_ref, twi_ref, k):
    tre = twr_ref[pl.ds(k, 1), :]
    tim = twi_ref[pl.ds(k, 1), :]
    c2 = c2_ref[...]
    s2 = s2_ref[...]
    return _kstack(c2 * tre - s2 * tim, c2 * tim + s2 * tre, FFT_N2)


def _lconv_kernel(u_ref, x_ref, kf_ref, bias_ref, fa_ref, c2_ref, s2_ref, twr_ref, twi_ref, fc_ref,
                  c1_ref, s1_ref, twtr_ref, twti_ref, o_ref, wr_ref, wi_ref):
    l = u_ref.shape[1]
    n2 = FFT_N2
    n1 = 2 * l // n2
    half = n1 // 2

    _fft_fwd_outer(u_ref.at[0], u_ref.at[1], wr_ref, wi_ref, fa_ref, n1, n2, half)

    fc = fc_ref[...]

    def slab(k, carry):
        rows = pl.ds(pl.multiple_of(k * n2, n2), n2)
        g = _twiddled_inner(c2_ref, s2_ref, twr_ref, twi_ref, k)
        xre, xim = _cmm(g, wr_ref[rows, :], wi_ref[rows, :])
        kre = kf_ref[0, 0, rows, :]
        kim = kf_ref[0, 1, rows, :]
        yre = xre * kre - xim * kim
        yim = xre * kim + xim * kre
        are, aim = _cmm(fc, yre, yim)
        wr_ref[rows, :] = are
        wi_ref[rows, :] = aim
        return carry

    lax.fori_loop(0, n1, slab, 0, unroll=FFT_UNROLL)

    bias = bias_ref[0]

    def outer(i, carry):
        tre = twtr_ref[pl.ds(i, 1), :]
        tim = twti_ref[pl.ds(i, 1), :]
        c1 = c1_ref[...]
        s1 = s1_ref[...]
        qs = _kstack(c1 * tre - s1 * tim, -(c1 * tim + s1 * tre), n1)
        yre, yim = _cmm(qs, wr_ref[pl.ds(i, n1, stride=n2), :], wi_ref[pl.ds(i, n1, stride=n2), :])
        rows = pl.ds(i, half, stride=n2)
        u0 = u_ref[0, rows, :]
        u1 = u_ref[1, rows, :]
        o_ref[0, rows, :] = x_ref[0, rows, :] * (yre + u0 * bias)
        o_ref[1, rows, :] = x_ref[1, rows, :] * (yim + u1 * bias)
        return carry

    lax.fori_loop(0, n2, outer, 0, unroll=FFT_UNROLL)


FILTER_CHUNK = 512


def _filter_pos(base, chunk, l):
    m = base + lax.broadcasted_iota(jnp.int32, (chunk, 1), 0)
    is_f = m < l
    pos = jnp.where(is_f, m, 2 * l - m).astype(F32)
    return m, is_f, pos


def _filter_mlp_kernel(l, w1_ref, b1_ref, wm_ref, bm_ref, fr_ref, bands_ref, z_ref):
    chunk = z_ref.shape[0]
    _, _, pos = _filter_pos(pl.program_id(0) * chunk, chunk, l)
    lane = lax.broadcasted_iota(jnp.int32, (1, LANES), 1)
    t = pos / max(l - 1, 1)
    ang = (2.0 * math.pi / l) * pos * bands_ref[...]
    feat = jnp.where(lane == 0, t,
                     jnp.where(lane <= POS_BANDS, jnp.cos(ang),
                               jnp.where(lane <= 2 * POS_BANDS, -jnp.sin(ang), 0.0)))
    z = jnp.sin(fr_ref[0:1, :] * (_dot_x3(feat, w1_ref[...]) + b1_ref[...]))
    for i in range(2):
        z = jnp.sin(fr_ref[i + 1:i + 2, :] * (_dot_x3(z, wm_ref[i]) + bm_ref[i:i + 1, :]))
    z_ref[...] = z


def _filter_kernel(z_ref, wof_ref, wob_ref, dec_ref, fa_ref, c2_ref, s2_ref, twr_ref, twi_ref, o_ref):
    n = o_ref.shape[2]
    l = n // 2
    n2 = FFT_N2
    n1 = n // n2
    chunk = min(FILTER_CHUNK, n)
    nchunk = n // chunk
    wof = _split(wof_ref[...])
    wob = _split(wob_ref[...])
    dec_f = jnp.abs(dec_ref[0, 0:1, :])
    dec_b = jnp.abs(dec_ref[0, 1:2, :])

    def gen(c, asum):
        base = pl.multiple_of(c * chunk, chunk)
        m, is_f, pos = _filter_pos(base, chunk, l)
        t = pos / max(l - 1, 1)
        z = z_ref[pl.ds(base, chunk), :]
        hf = _dot_presplit(z, *wof) * jnp.exp(-t * dec_f)
        hb = _dot_presplit(z, *wob) * jnp.exp(-t * dec_b)
        k2 = jnp.where(is_f, hf, jnp.where(m == l, 0.0, hb))
        o_ref[0, 0, pl.ds(base, chunk), :] = k2
        return asum + jnp.sum(jnp.abs(k2), axis=0, keepdims=True)

    asum = lax.fori_loop(0, nchunk, gen, jnp.zeros((1, LANES), F32))
    inv = 1.0 / asum

    re_ref = o_ref.at[0, 0]
    im_ref = o_ref.at[0, 1]
    _fft_fwd_outer(re_ref, None, re_ref, im_ref, fa_ref, n1, n2, n1, scale=inv)

    def slab(k, carry):
        rows = pl.ds(pl.multiple_of(k * n2, n2), n2)
        g = _twiddled_inner(c2_ref, s2_ref, twr_ref, twi_ref, k)
        xre, xim = _cmm(g, re_ref[rows, :], im_ref[rows, :])
        re_ref[rows, :] = xre
        im_ref[rows, :] = xim
        return carry

    lax.fori_loop(0, n1, slab, 0, unroll=FFT_UNROLL)


def _dft_tables(l):
    n = 2 * l
    n2 = FFT_N2
    n1 = n // n2
    half = n1 // 2

    def cs(rows, cols, period):
        ang = 2.0 * np.pi * ((np.arange(rows)[:, None] * np.arange(cols)[None, :]) % period) / period
        return np.cos(ang), -np.sin(ang)

    def split_np(m):
        m32 = jnp.asarray(m, F32)
        hi = m32.astype(BF16)
        lo = (m32 - hi.astype(F32)).astype(BF16)
        return hi, lo

    def kstack_np(c, s):
        ch, cl = split_np(c)
        sh, sl = split_np(s)
        return jnp.concatenate([jnp.concatenate([ch, ch, cl], axis=1),
                                jnp.concatenate([sh, sh, sl], axis=1)], axis=0)

    def wide(m):
        return jnp.asarray(m if m.shape[1] % LANES == 0 else np.tile(m, (1, 3)), F32)

    c1, s1 = cs(n1, n1, n1)
    c2, s2 = cs(n2, n2, n2)
    tw_c, tw_s = cs(n1, n2, n)
    tabs = {}
    tabs["fa_half"] = kstack_np(c1[:, :half], s1[:, :half])
    tabs["fa_full"] = kstack_np(c1, s1)
    tabs["fc"] = kstack_np(c2, -s2)
    tabs["c2"] = wide(c2)
    tabs["s2"] = wide(s2)
    tabs["twr"] = wide(tw_c)
    tabs["twi"] = wide(tw_s)
    tabs["c1h"] = wide(c1[:half] / n)
    tabs["s1h"] = wide(s1[:half] / n)
    tabs["twtr"] = wide(tw_c.T)
    tabs["twti"] = wide(tw_s.T)
    return tabs


def _const_spec(a, nd_grid):
    shape = a.shape
    if nd_grid == 2:
        return pl.BlockSpec(shape, lambda i, j: (0,) * len(shape))
    return pl.BlockSpec(shape, lambda i: (0,) * len(shape))


def _hyena_filters(l, p, tabs):
    (f_w1, f_b1, f_w_mid, f_b_mid, f_freq, f_w_out, f_decay) = p
    n = 2 * l
    e = WIDTH
    nblk = e // LANES
    w1p = jnp.zeros((LANES, FILTER_HIDDEN), F32).at[:f_w1.shape[0]].set(f_w1)
    bands = np.linspace(1e-4, POS_BANDS - 1, POS_BANDS, dtype=np.float32)
    bl = np.zeros((1, LANES), np.float32)
    bl[0, 1:1 + POS_BANDS] = bands
    bl[0, 1 + POS_BANDS:1 + 2 * POS_BANDS] = bands
    consts = [tabs["fa_full"], tabs["c2"], tabs["s2"], tabs["twr"], tabs["twi"]]
    chunk = min(FILTER_CHUNK, n)
    mlp_in = [w1p, f_b1.reshape(1, FILTER_HIDDEN), f_w_mid, f_b_mid, f_freq, jnp.asarray(bl)]
    z = pl.pallas_call(
        functools.partial(_filter_mlp_kernel, l),
        grid=(n // chunk,),
        in_specs=[_const_spec(a, 1) for a in mlp_in],
        out_specs=pl.BlockSpec((chunk, FILTER_HIDDEN), lambda c: (c, 0)),
        out_shape=jax.ShapeDtypeStruct((n, FILTER_HIDDEN), F32),
        compiler_params=_cparams("parallel"),
        name="hyena_filter_mlp",
    )(*mlp_in)
    return pl.pallas_call(
        _filter_kernel,
        grid=(2, nblk),
        in_specs=[
            pl.BlockSpec((n, FILTER_HIDDEN), lambda o, j: (0, 0)),
            pl.BlockSpec((FILTER_HIDDEN, LANES), lambda o, j: (0, o * 2 * nblk + j)),
            pl.BlockSpec((FILTER_HIDDEN, LANES), lambda o, j: (0, o * 2 * nblk + nblk + j)),
            pl.BlockSpec((1, 2, LANES), lambda o, j: (o, 0, j)),
        ] + [_const_spec(c, 2) for c in consts],
        out_specs=pl.BlockSpec((1, 2, n, LANES), lambda o, j: (o, 0, 0, j)),
        out_shape=jax.ShapeDtypeStruct((2, 2, n, e), F32),
        compiler_params=_cparams("parallel", "arbitrary"),
        name="hyena_filter",
    )(z, f_w_out, f_w_out, f_decay.reshape(2, 2, e), *consts)


def _long_conv(u, xg, kf, order, bias, tabs, col_u, col_x):
    b, l, _ = u.shape
    e = WIDTH
    n = 2 * l
    nblk = e // LANES
    consts = [tabs["fa_half"], tabs["c2"], tabs["s2"], tabs["twr"], tabs["twi"],
              tabs["fc"], tabs["c1h"], tabs["s1h"], tabs["twtr"], tabs["twti"]]
    return pl.pallas_call(
        _lconv_kernel,
        grid=(nblk, b // 2),
        in_specs=[
            pl.BlockSpec((2, l, LANES), lambda j, pi: (pi, 0, col_u * nblk + j)),
            pl.BlockSpec((2, l, LANES), lambda j, pi: (pi, 0, col_x * nblk + j)),
            pl.BlockSpec((1, 2, n, LANES), lambda j, pi: (order, 0, 0, j)),
            pl.BlockSpec((1, 1, LANES), lambda j, pi: (order, 0, j)),
        ] + [_const_spec(c, 2) for c in consts],
        out_specs=pl.BlockSpec((2, l, LANES), lambda j, pi: (pi, 0, j)),
        out_shape=jax.ShapeDtypeStruct((b, l, e), F32),
        scratch_shapes=[pltpu.VMEM((n, LANES), F32), pltpu.VMEM((n, LANES), F32)],
        compiler_params=_cparams("parallel", "arbitrary"),
        name="hyena_long_conv",
    )(u, xg, kf, bias.reshape(2, 1, e), *consts)


def _gated_out_kernel(z_ref, g_ref, x_ref, mod_ref, w_ref, o_ref):
    zz = (z_ref[0] * _silu(g_ref[0])).astype(BF16)
    o_ref[0] = x_ref[0] + mod_ref[0, 2:3, :] * _dot(zz, w_ref[...])


def _gated_out(z, g_arr, g_col, x, mod, w_out, name):
    b, l, d = x.shape
    e = WIDTH
    tl = 256
    tile = lambda col: pl.BlockSpec((1, tl, e), lambda bi, ti: (bi, ti, col))
    return pl.pallas_call(
        _gated_out_kernel,
        grid=(b, l // tl),
        in_specs=[tile(0), tile(g_col), tile(0),
                  pl.BlockSpec((1, 3, d), lambda bi, ti: (bi, 0, 0)),
                  pl.BlockSpec((e, d), lambda bi, ti: (0, 0))],
        out_specs=tile(0),
        out_shape=jax.ShapeDtypeStruct((b, l, d), F32),
        compiler_params=_cparams("parallel", "arbitrary"),
        name=name,
    )(z, g_arr, x, mod, w_out.astype(BF16))


def _hyena_layer(x, mod, ng, p, tabs):
    (w_in, short_w, short_b, f_w1, f_b1, f_w_mid, f_b_mid, f_freq, f_w_out, f_decay, bias, w_out) = p
    l = x.shape[1]
    proj = _norm_proj(x, mod, ng, w_in, "hyena_proj")
    u3 = _short_conv(proj, short_w, short_b)
    kf = _hyena_filters(l, (f_w1, f_b1, f_w_mid, f_b_mid, f_freq, f_w_out, f_decay), tabs)
    z1 = _long_conv(u3, u3, kf, 0, bias, tabs, col_u=2, col_x=0)
    z2 = _long_conv(z1, u3, kf, 1, bias, tabs, col_u=0, col_x=1)
    return _gated_out(z2, proj, 3, x, mod, w_out, "hyena_out")


def _cf_proj_kernel(x_ref, mod_ref, ng_ref, w_ref, u_o, sg_o):
    e = WIDTH
    h = _prenorm(x_ref[0], ng_ref[...], mod_ref[0, 0:1, :], mod_ref[0, 1:2, :]).astype(BF16)
    a = _dot(h, w_ref[:, 0:e])
    bgate = _dot(h, w_ref[:, e:2 * e])
    g = _dot(h, w_ref[:, 2 * e:3 * e])
    u_o[0] = a * _sigmoid(bgate)
    sg_o[0] = _silu(g)


def _cf_out_kernel(u_ref, up_ref, un_ref, sg_ref, x_ref, mod_ref, dw_ref, db_ref, lg_ref, lb_ref, w_ref,
                   o_ref, ext_ref):
    t = pl.program_id(1)
    nt = pl.num_programs(1)
    tl = u_ref.shape[1]
    hh = CONV_HALO
    ext_ref[0:hh, :] = jnp.where(t > 0, up_ref[0], 0.0)
    ext_ref[hh:hh + tl, :] = u_ref[0]
    ext_ref[hh + tl:hh + tl + hh, :] = jnp.where(t < nt - 1, un_ref[0], 0.0)
    off = hh - (CONV_WIDTH - 1) // 2
    acc = jnp.zeros((tl, WIDTH), F32) + db_ref[...]
    ext = ext_ref[...]
    nrow = ext.shape[0]
    for phase in range(8):
        win = ext if phase == 0 else pltpu.roll(ext, nrow - phase, 0)
        for k in range(CONV_WIDTH):
            if (off + k) % 8 == phase:
                base = off + k - phase
                acc = acc + win[base:base + tl, :] * dw_ref[k:k + 1, :]
    mean = jnp.mean(acc, axis=-1, keepdims=True)
    cen = acc - mean
    var = jnp.mean(cen * cen, axis=-1, keepdims=True)
    y = cen * lax.rsqrt(var + LN_EPS) * lg_ref[...] + lb_ref[...]
    zz = (_silu(y) * sg_ref[0]).astype(BF16)
    o_ref[0] = x_ref[0] + mod_ref[0, 2:3, :] * _dot(zz, w_ref[...])


def _conformer_layer(x, mod, ng, p):
    (w_in, dw_w, dw_b, ln_g, ln_b, w_out) = p
    b, l, d = x.shape
    e = WIDTH
    tl = 256
    tile = pl.BlockSpec((1, tl, e), lambda bi, ti: (bi, ti, 0))
    modspec = pl.BlockSpec((1, 3, d), lambda bi, ti: (bi, 0, 0))
    vec = pl.BlockSpec((1, e), lambda bi, ti: (0, 0))
    u, sg = pl.pallas_call(
        _cf_proj_kernel,
        grid=(b, l // tl),
        in_specs=[tile, modspec, vec, pl.BlockSpec((d, 3 * e), lambda bi, ti: (0, 0))],
        out_specs=[tile, tile],
        out_shape=[jax.ShapeDtypeStruct((b, l, e), F32)] * 2,
        compiler_params=_cparams("parallel", "arbitrary"),
        name="conformer_proj",
    )(x, mod, ng.reshape(1, d), w_in.astype(BF16))

    hh = CONV_HALO
    hb = tl // hh
    lb = l // hh
    return pl.pallas_call(
        _cf_out_kernel,
        grid=(b, l // tl),
        in_specs=[
            tile,
            pl.BlockSpec((1, hh, e), lambda bi, ti: (bi, jnp.maximum(ti * hb - 1, 0), 0)),
            pl.BlockSpec((1, hh, e), lambda bi, ti: (bi, jnp.minimum((ti + 1) * hb, lb - 1), 0)),
            tile, tile, modspec,
            pl.BlockSpec((CONV_WIDTH, e), lambda bi, ti: (0, 0)),
            vec, vec, vec,
            pl.BlockSpec((e, d), lambda bi, ti: (0, 0)),
        ],
        out_specs=tile,
        out_shape=jax.ShapeDtypeStruct((b, l, d), F32),
        scratch_shapes=[pltpu.VMEM((tl + 2 * hh, e), F32)],
        compiler_params=_cparams("parallel", "arbitrary"),
        name="conformer_out",
    )(u, u, u, sg, x, mod, dw_w, dw_b.reshape(1, e), ln_g.reshape(1, e), ln_b.reshape(1, e),
      w_out.astype(BF16))


def _final_norm_kernel(x_ref, g_ref, o_ref):
    x = x_ref[0]
    ms = jnp.mean(x * x, axis=-1, keepdims=True)
    o_ref[0] = x * lax.rsqrt(ms + RMS_EPS) * g_ref[...]


def _final_norm(x, g):
    b, l, d = x.shape
    tl = 512
    tile = pl.BlockSpec((1, tl, d), lambda bi, ti: (bi, ti, 0))
    return pl.pallas_call(
        _final_norm_kernel,
        grid=(b, l // tl),
        in_specs=[tile, pl.BlockSpec((1, d), lambda bi, ti: (0, 0))],
        out_specs=tile,
        out_shape=jax.ShapeDtypeStruct((b, l, d), F32),
        compiler_params=_cparams("parallel", "arbitrary"),
        name="final_norm",
    )(x, g.reshape(1, d))


def _seg_constants():
    seg = np.arange(SEG_GROUP) // HEAD
    bd = (seg[:, None] == seg[None, :]).astype(np.float32)
    return {"bd": jnp.asarray(bd, BF16)}


def _trunk(x, mod, norm_g, rw, hy, cf, final_g, consts):
    tabs = None
    for i in range(DEPTH):
        kind, j = i % 3, i // 3
        m = mod[i]
        last = i == DEPTH - 1
        if kind == 0:
            x = _rwkv_layer(x, m, norm_g[i], [p[j] for p in rw], consts, final_g if last else None)
            if last:
                return x
        elif kind == 1:
            if tabs is None:
                tabs = _dft_tables(x.shape[1])
            x = _hyena_layer(x, m, norm_g[i], [p[j] for p in hy], tabs)
        else:
            x = _conformer_layer(x, m, norm_g[i], [p[j] for p in cf])
    return _final_norm(x, final_g)


def kernel(x_prompt, x_sample, c_prompt, c_sample, norm_g, mod_w, mod_b, rw_w_in, rw_mu, rw_w0, rw_w2, rw_a0, rw_a2, rw_k_k, rw_k_a, rw_r_k, rw_gn_w, rw_gn_b, rw_w_out, hy_w_in, hy_short_w, hy_short_b, hy_f_w1, hy_f_b1, hy_f_w_mid, hy_f_b_mid, hy_f_freq, hy_f_w_out, hy_f_decay, hy_bias, hy_w_out, cf_w_in, cf_dw_w, cf_dw_b, cf_ln_g, cf_ln_b, cf_w_out, final_g):
    rw = (rw_w_in, rw_mu, rw_w0, rw_w2, rw_a0, rw_a2, rw_k_k, rw_k_a, rw_r_k, rw_gn_w, rw_gn_b, rw_w_out)
    hy = (hy_w_in, hy_short_w, hy_short_b, hy_f_w1, hy_f_b1, hy_f_w_mid, hy_f_b_mid, hy_f_freq, hy_f_w_out,
          hy_f_decay, hy_bias, hy_w_out)
    cf = (cf_w_in, cf_dw_w, cf_dw_b, cf_ln_g, cf_ln_b, cf_w_out)
    consts = _seg_constants()
    bp = x_prompt.shape[0]
    mod = _modulation(jnp.concatenate([c_prompt, c_sample], axis=0), mod_w, mod_b)
    y_prompt = _trunk(x_prompt, mod[:, :bp], norm_g, rw, hy, cf, final_g, consts)
    y_sample = _trunk(x_sample, mod[:, bp:], norm_g, rw, hy, cf, final_g, consts)
    return (y_prompt, y_sample)
```

```python
import functools
import math

import numpy as np
import jax
import jax.numpy as jnp
from jax import lax
from jax.experimental import pallas as pl
from jax.experimental.pallas import tpu as pltpu

F32 = jnp.float32
BF16 = jnp.bfloat16

D_MODEL = 1024
WIDTH = 1024
DEPTH = 4
HEADS = 16
HEAD = 64
LORA = 64
RMS_EPS = 1e-6
LN_EPS = 1e-5
GN_EPS = 64e-5
POS_BANDS = 16
FILTER_HIDDEN = 64
CONV_WIDTH = 31
CONV_HALO = 16
LANES = 128
SEG_GROUP = 256
FFT_N2 = 64
FFT_UNROLL = 8
VMEM_LIMIT = 56 * 1024 * 1024


def _cparams(*sem):
    return pltpu.CompilerParams(dimension_semantics=sem, vmem_limit_bytes=VMEM_LIMIT)


def _dot(a, b):
    return jnp.dot(a, b, preferred_element_type=F32)


def _split(x):
    hi = x.astype(BF16)
    lo = (x - hi.astype(F32)).astype(BF16)
    return hi, lo


def _dot_x3(a, b):
    ah, al = _split(a)
    bh, bl = _split(b)
    return _dot(ah, bh) + _dot(ah, bl) + _dot(al, bh)


def _dot_presplit(a, bh, bl):
    ah, al = _split(a)
    return _dot(ah, bh) + _dot(ah, bl) + _dot(al, bh)


def _dot_kstacked(a, b_stack):
    ah, al = _split(a)
    return _dot(jnp.concatenate([ah, al, ah], axis=1), b_stack)


def _seg_bcast(t, bd):
    outs = []
    for g in range(t.shape[1] // SEG_GROUP):
        th, tl = _split(t[:, g * SEG_GROUP:(g + 1) * SEG_GROUP])
        outs.append(_dot(th, bd) + _dot(tl, bd))
    return jnp.concatenate(outs, axis=1)


def _sigmoid(x):
    return 1.0 / (1.0 + jnp.exp(-x))


def _silu(x):
    return x * _sigmoid(x)


def _prenorm(x, ng, shift, scale):
    ms = jnp.mean(x * x, axis=-1, keepdims=True)
    return (x * lax.rsqrt(ms + RMS_EPS) * ng) * (1.0 + scale) + shift


def _mod_kernel(c_ref, w_ref, b_ref, o_ref):
    o_ref[0] = _dot_x3(_silu(c_ref[...]), w_ref[0]) + b_ref[0]


def _modulation(c_all, mod_w, mod_b):
    bc = c_all.shape[0]
    out = pl.pallas_call(
        _mod_kernel,
        grid=(DEPTH, 3),
        in_specs=[
            pl.BlockSpec((bc, D_MODEL), lambda i, j: (0, 0)),
            pl.BlockSpec((1, D_MODEL, D_MODEL), lambda i, j: (i, 0, j)),
            pl.BlockSpec((1, 1, D_MODEL), lambda i, j: (i, 0, j)),
        ],
        out_specs=pl.BlockSpec((1, bc, D_MODEL), lambda i, j: (i, 0, j)),
        out_shape=jax.ShapeDtypeStruct((DEPTH, bc, 3 * D_MODEL), F32),
        compiler_params=_cparams("arbitrary", "arbitrary"),
        name="modulation",
    )(c_all, mod_w, mod_b.reshape(DEPTH, 1, 3 * D_MODEL))
    return out.reshape(DEPTH, bc, 3, D_MODEL)


RW_TL = 256
SCAN_BATCH = 2
SCAN_CHUNK = 32


def _rw_proj_kernel(x_ref, xp_ref, xn_ref, mod_ref, ng_ref, mu_ref, win_ref,
                    w0_ref, w2_ref, a0_ref, a2_ref,
                    kk_ref, ka_ref, rk_ref, bd_ref,
                    r_o, v_o, nkk_o, g_o, bv_o, w0_o, k0_o, b0_o, w1_o, k1_o, b1_o):
    t = pl.program_id(1)
    nt = pl.num_programs(1)
    tl = x_ref.shape[1]
    e = WIDTH
    shift = mod_ref[0, 0:1, :]
    scale = mod_ref[0, 1:2, :]
    ng = ng_ref[...]
    bd = bd_ref[...]

    h = _prenorm(x_ref[0], ng, shift, scale)
    hp = _prenorm(xp_ref[0], ng, shift, scale)[7:8, :]
    hn = _prenorm(xn_ref[0], ng, shift, scale)[0:1, :]
    hp = jnp.where(t > 0, hp, 0.0)
    hn = jnp.where(t < nt - 1, hn, 0.0)
    rows = lax.broadcasted_iota(jnp.int32, (tl, 1), 0)
    h_up = jnp.where(rows == 0, hp, pltpu.roll(h, 1, 0))
    h_dn = jnp.where(rows == tl - 1, hn, pltpu.roll(h, tl - 1, 0))
    xx = 0.5 * (h_up + h_dn) - h

    def proj(i, c0, c1):
        inp = (h + xx * mu_ref[i:i + 1, :]).astype(BF16)
        return _dot(inp, win_ref[:, c0:c1])

    r = proj(0, 0, e)
    k = proj(1, e, 2 * e)
    v = proj(2, 2 * e, 3 * e)
    g = proj(3, 3 * e, 4 * e)
    wl = proj(4, 4 * e, 4 * e + 2 * LORA)
    al = proj(5, 4 * e + 2 * LORA, 4 * e + 4 * LORA)
    twl = jnp.tanh(wl)

    kk0 = k * kk_ref[...]
    nrm = jnp.sqrt(_seg_bcast(kk0 * kk0, bd))
    kk = kk0 / jnp.maximum(nrm, 1e-12)
    ka = ka_ref[...]

    r_o[0] = r
    v_o[0] = v
    g_o[0] = g
    nkk_o[0] = -kk

    ksum = jnp.zeros_like(k)
    for d, (w_o, k_o, b_o) in enumerate(((w0_o, k0_o, b0_o), (w1_o, k1_o, b1_o))):
        w_raw = w0_ref[d:d + 1, :] + _dot_kstacked(twl, w2_ref[d])
        z = -w_raw
        softplus = jnp.maximum(z, 0.0) + jnp.log(1.0 + jnp.exp(-jnp.abs(z)))
        w_log = -softplus - 0.5
        w_o[0] = -jnp.exp(w_log)
        a = _sigmoid(a0_ref[d:d + 1, :] + _dot_kstacked(al, a2_ref[d]))
        k_d = k * (1.0 + (a - 1.0) * ka)
        k_o[0] = k_d
        b_o[0] = kk * a
        ksum = ksum + k_d

    bonus = _seg_bcast(r * ksum * rk_ref[...], bd)
    bv_o[0] = bonus * v


def _dot_nt(a, b):
    return lax.dot_general(a, b, (((1,), (1,)), ((), ())), preferred_element_type=F32)


def _dot_tn(a, b):
    return lax.dot_general(a, b, (((0,), (0,)), ((), ())), preferred_element_type=F32)


def _rw_chunk_kernel(rf, vf, af, wf, kf, bf, rb, vb, ab, wb, kb, bb, bd_ref, bdsj_ref, bdinv_ref,
                     yf_o, yb_o, s_ref):
    tb = pl.program_id(1)
    nb, c, _ = rf.shape
    grp = SEG_GROUP
    hpg = grp // HEAD
    sw = hpg * c
    pair = grp // sw

    @pl.when(tb == 0)
    def _():
        s_ref[...] = jnp.zeros_like(s_ref)

    row = lax.broadcasted_iota(jnp.int32, (c, sw), 0)
    col = lax.broadcasted_iota(jnp.int32, (c, sw), 1) % c
    eye_inv = (lax.broadcasted_iota(jnp.int32, (c, grp), 0)
               == lax.broadcasted_iota(jnp.int32, (c, grp), 1) % c)
    eye_state = (lax.broadcasted_iota(jnp.int32, (HEAD, grp), 0)
                 == lax.broadcasted_iota(jnp.int32, (HEAD, grp), 1) % HEAD)
    tr = lax.broadcasted_iota(jnp.int32, (c, 3 * c), 0)
    tcq = lax.broadcasted_iota(jnp.int32, (c, 3 * c), 1) % c
    bd = bd_ref[...]
    bd_sj = bdsj_ref[...]
    bd_inv = bdinv_ref[...]
    bd_f32 = bd.astype(F32)

    def blockdiag(x, mask):
        reps = mask.shape[0] // x.shape[0]
        xh, xl = _split(x)
        return (jnp.concatenate([xh] * reps, axis=0) * mask, jnp.concatenate([xl] * reps, axis=0) * mask)

    def pmm(lhs, w, dot=_dot, single=None):
        m = lhs.shape[0]
        lh, ll = _split(lhs)
        stack = [lh, ll] if single is None else [lh, ll, single.astype(BF16)]
        res = dot(jnp.concatenate(stack, axis=0), w[0])
        full = res[:m] + res[m:2 * m] + dot(lh, w[1])
        return full if single is None else (full, res[2 * m:])

    def pmm1(lhs, w):
        return _dot(lhs.astype(BF16), w[0])

    def chain_operands(q, rev, refs):
        r, v, a, lw, k, b = [ref[q] for ref in refs]
        tri = (tcq >= tr if rev else tcq <= tr).astype(BF16)
        p1 = lw.astype(BF16)
        rem = lw - p1.astype(F32)
        p2 = rem.astype(BF16)
        p3 = (rem - p2.astype(F32)).astype(BF16)
        cs = _dot(tri, jnp.concatenate([p1, p2, p3], axis=0))
        total = cs[0:1] if rev else cs[c - 1:c]
        e_neg = jnp.exp(-cs)
        e_bar = jnp.exp(total - cs)
        return dict(at=a * jnp.exp(cs - lw), rt=r * jnp.exp(cs), bt=b * e_neg, kt=k * e_neg,
                    bb=b * e_bar, kb=k * e_bar, v=v, gc=jnp.exp(total),
                    strict=(col > row) if rev else (col < row),
                    incl=(col >= row) if rev else (col <= row))

    chains = [(q, rev, refs, y_o)
              for q in range(nb)
              for rev, refs, y_o in ((False, (rf, vf, af, wf, kf, bf), yf_o), (True, (rb, vb, ab, wb, kb, bb), yb_o))]
    ops = [chain_operands(q, rev, refs) for q, rev, refs, _ in chains]
    probs = [(ch, g) for ch in range(len(chains)) for g in range(WIDTH // grp)]
    pick = lambda name: [ops[ch][name][:, g * grp:(g + 1) * grp] for ch, g in probs]
    mask = lambda name: [ops[ch][name] for ch, _ in probs]
    at, rt, bt, kt, bb, kb, vv, gc = (pick(n) for n in ("at", "rt", "bt", "kt", "bb", "kb", "v", "gc"))
    strict, incl = mask("strict"), mask("incl")

    sc_b = [pmm(x, blockdiag(w, bd_sj), _dot_nt, single=y) for x, y, w in zip(at, rt, bt)]
    sc_k = [pmm(x, blockdiag(w, bd_sj), _dot_nt, single=y) for x, y, w in zip(at, rt, kt)]
    mab = [jnp.where(m, s[0], 0.0) for m, s in zip(strict, sc_b)]
    mak = [jnp.where(m, s[0], 0.0) for m, s in zip(strict, sc_k)]
    nrb = [jnp.where(m, s[1], 0.0) for m, s in zip(incl, sc_b)]
    nrk = [jnp.where(m, s[1], 0.0) for m, s in zip(incl, sc_k)]
    wide = [jnp.concatenate(mab[i:i + pair], axis=1) for i in range(0, len(probs), pair)]
    inv = [jnp.where(eye_inv, 1.0, m) for m in wide]
    pw = [pmm(m, blockdiag(m, bd_inv)) for m in wide]
    doublings = int(math.log2(c)) - 1
    for it in range(doublings):
        if it == doublings - 1:
            inv = [i + pmm(i, blockdiag(p, bd_inv)) for i, p in zip(inv, pw)]
        else:
            both = [pmm(jnp.concatenate([p, i], axis=0), blockdiag(p, bd_inv)) for i, p in zip(inv, pw)]
            pw = [x[:c] for x in both]
            inv = [i + x[c:] for i, x in zip(inv, both)]
    inv = [w[:, j * sw:(j + 1) * sw] for w in inv for j in range(pair)]
    mv = [pmm(x, blockdiag(w, bd_sj), single=y) for x, y, w in zip(mak, nrk, vv)]
    dg = [jnp.where(eye_state, x, 0.0) for x in gc]

    xy = [pmm(jnp.concatenate([a_, d_], axis=0), blockdiag(s_ref[ch, g], bd), single=r_)
          for a_, r_, d_, (ch, g) in zip(at, rt, dg, probs)]
    x = [t[0] for t in xy]
    p = [pmm(i, blockdiag(x_[:c] + z[0], bd_sj)) for i, x_, z in zip(inv, x, mv)]
    for t, z, n, p_, (ch, g) in zip(xy, mv, nrb, p, probs):
        q, _, _, y_o = chains[ch]
        y_o[q, :, g * grp:(g + 1) * grp] = t[1] + z[1] + pmm1(n, blockdiag(p_, bd_sj))
    for x_, p_, v_, b_, k_, (ch, g) in zip(x, p, vv, bb, kb, probs):
        bk_h, bk_l = _split(jnp.concatenate([b_, k_], axis=0))
        pv_h, pv_l = _split(jnp.concatenate([p_, v_], axis=0))
        full = _dot_tn(jnp.concatenate([bk_h, bk_h, bk_l], axis=0),
                       jnp.concatenate([pv_h, pv_l, pv_h], axis=0)) * bd_f32
        upd = full[0:HEAD]
        for hh in range(1, hpg):
            upd = upd + full[hh * HEAD:(hh + 1) * HEAD]
        s_ref[ch, g] = x_[c:] + upd


def _rw_out_kernel(final, yf_ref, yb_ref, bv_ref, g_ref, x_ref, mod_ref, gw_ref, gb_ref, wout_ref, bd_ref,
                   fg_ref, o_ref):
    bd = bd_ref[...]
    y = yf_ref[0] + yb_ref[0]
    mean = _seg_bcast(y, bd) * (1.0 / HEAD)
    yc = y - mean
    var = _seg_bcast(yc * yc, bd) * (1.0 / HEAD)
    yn = yc * lax.rsqrt(var + GN_EPS) * gw_ref[...] + gb_ref[...]
    yo = (yn + bv_ref[0]) * _silu(g_ref[0])
    o = _dot(yo.astype(BF16), wout_ref[...])
    res = x_ref[0] + mod_ref[0, 2:3, :] * o
    if final:
        ms = jnp.mean(res * res, axis=-1, keepdims=True)
        res = res * lax.rsqrt(ms + RMS_EPS) * fg_ref[...]
    o_ref[0] = res


def _pad_lora(w):
    z = jnp.zeros_like(w[0])
    return jnp.stack([jnp.concatenate([w[0], z], axis=0), jnp.concatenate([z, w[1]], axis=0)], axis=0)


def _split_host(w):
    hi = w.astype(BF16)
    lo = (w - hi.astype(F32)).astype(BF16)
    return hi, lo


def _rwkv_layer(x, mod, ng, p, consts, final_g=None):
    (w_in, mu, w0, w2, a0, a2, k_k, k_a, r_k, gn_w, gn_b, w_out) = p
    b, l, d = x.shape
    e = WIDTH
    tl = RW_TL
    nt = l // tl
    bd = consts["bd"]
    def lora_stack(w):
        hi, lo = _split_host(_pad_lora(w))
        return jnp.concatenate([hi, hi, lo], axis=1)

    w2s, a2s = lora_stack(w2), lora_stack(a2)
    ncols = w_in.shape[1]

    full = lambda shape: pl.BlockSpec(shape, lambda bi, ti: (0,) * len(shape))
    tile = pl.BlockSpec((1, tl, e), lambda bi, ti: (bi, ti, 0))
    hb = tl // 8
    lb = l // 8
    outs = pl.pallas_call(
        _rw_proj_kernel,
        grid=(b, nt),
        in_specs=[
            tile,
            pl.BlockSpec((1, 8, d), lambda bi, ti: (bi, jnp.maximum(ti * hb - 1, 0), 0)),
            pl.BlockSpec((1, 8, d), lambda bi, ti: (bi, jnp.minimum((ti + 1) * hb, lb - 1), 0)),
            pl.BlockSpec((1, 3, d), lambda bi, ti: (bi, 0, 0)),
            full((1, d)), full((6, d)), full((d, ncols)),
            full((2, e)), full((2, 6 * LORA, e)),
            full((2, e)), full((2, 6 * LORA, e)),
            full((1, e)), full((1, e)), full((1, e)),
            full((SEG_GROUP, SEG_GROUP)),
        ],
        out_specs=[tile] * 11,
        out_shape=[jax.ShapeDtypeStruct((b, l, e), F32)] * 11,
        compiler_params=_cparams("parallel", "arbitrary"),
        name="rwkv_proj",
    )(x, x, x, mod, ng.reshape(1, d), mu, w_in.astype(BF16),
      w0, w2s, a0, a2s,
      k_k.reshape(1, e), k_a.reshape(1, e), r_k.reshape(1, e), bd)
    r, v, nkk, g, bv, wd0, kd0, bd0, wd1, kd1, bd1 = outs

    tc = SCAN_CHUNK
    ntc = l // tc
    nb = SCAN_BATCH
    fwd = pl.BlockSpec((nb, tc, e), lambda bi, ti: (bi, ti, 0))
    bwd = pl.BlockSpec((nb, tc, e), lambda bi, ti: (bi, ntc - 1 - ti, 0))
    yf, yb = pl.pallas_call(
        _rw_chunk_kernel,
        grid=(b // nb, ntc),
        in_specs=[fwd] * 6 + [bwd] * 6 + [full(consts[n].shape) for n in ("bd", "bd_sj", "bd_inv")],
        out_specs=[fwd, bwd],
        out_shape=[jax.ShapeDtypeStruct((b, l, e), F32)] * 2,
        scratch_shapes=[pltpu.VMEM((2 * nb, e // SEG_GROUP, HEAD, SEG_GROUP), F32)],
        compiler_params=_cparams("parallel", "arbitrary"),
        name="rwkv_scan",
    )(r, v, nkk, wd0, kd0, bd0, r, v, nkk, wd1, kd1, bd1, bd, consts["bd_sj"], consts["bd_inv"])

    tlo = 256
    tile_o = pl.BlockSpec((1, tlo, e), lambda bi, ti: (bi, ti, 0))
    final = final_g is not None
    fg = (final_g if final else jnp.ones((d,), F32)).reshape(1, d)
    return pl.pallas_call(
        functools.partial(_rw_out_kernel, final),
        grid=(b, l // tlo),
        in_specs=[tile_o] * 5 + [
            pl.BlockSpec((1, 3, d), lambda bi, ti: (bi, 0, 0)),
            full((1, e)), full((1, e)), full((e, d)), full((SEG_GROUP, SEG_GROUP)), full((1, d)),
        ],
        out_specs=tile_o,
        out_shape=jax.ShapeDtypeStruct((b, l, d), F32),
        compiler_params=_cparams("parallel", "arbitrary"),
        name="rwkv_out",
    )(yf, yb, bv, g, x, mod, gn_w.reshape(1, e), gn_b.reshape(1, e), w_out.astype(BF16), bd, fg)


def _hy_proj_kernel(x_ref, xp_ref, xn_ref, mod_ref, ng_ref, w_ref, sw_ref, sb_ref, u_o, g_o):
    t = pl.program_id(1)
    nt = pl.num_programs(1)
    tl = x_ref.shape[1]
    nu = u_o.shape[2]
    ng, shift, scale = ng_ref[...], mod_ref[0, 0:1, :], mod_ref[0, 1:2, :]
    proj = _dot(_prenorm(x_ref[0], ng, shift, scale).astype(BF16), w_ref[...])
    halo = jnp.concatenate([_prenorm(xp_ref[0], ng, shift, scale), _prenorm(xn_ref[0], ng, shift, scale)], axis=0)
    hproj = _dot(halo.astype(BF16), w_ref[:, 0:nu])
    prev = jnp.where(t > 0, hproj[7:8], 0.0)
    nxt = jnp.where(t < nt - 1, hproj[8:9], 0.0)
    u = proj[:, 0:nu]
    rows = lax.broadcasted_iota(jnp.int32, (tl, 1), 0)
    up = jnp.where(rows == 0, prev, pltpu.roll(u, 1, 0))
    dn = jnp.where(rows == tl - 1, nxt, pltpu.roll(u, tl - 1, 0))
    u_o[0] = up * sw_ref[0:1, :] + u * sw_ref[1:2, :] + dn * sw_ref[2:3, :] + sb_ref[...]
    g_o[0] = proj[:, nu:]


def _hy_proj(x, mod, ng, w, short_w, short_b):
    b, l, d = x.shape
    e = WIDTH
    nu = short_w.shape[1]
    tl = 256
    hb, lb = tl // 8, l // 8
    full = lambda shape: pl.BlockSpec(shape, lambda bi, ti: (0,) * len(shape))
    return pl.pallas_call(
        _hy_proj_kernel,
        grid=(b, l // tl),
        in_specs=[
            pl.BlockSpec((1, tl, d), lambda bi, ti: (bi, ti, 0)),
            pl.BlockSpec((1, 8, d), lambda bi, ti: (bi, jnp.maximum(ti * hb - 1, 0), 0)),
            pl.BlockSpec((1, 8, d), lambda bi, ti: (bi, jnp.minimum((ti + 1) * hb, lb - 1), 0)),
            pl.BlockSpec((1, 3, d), lambda bi, ti: (bi, 0, 0)),
            full((1, d)), full((d, nu + e)), full((3, nu)), full((1, nu)),
        ],
        out_specs=[pl.BlockSpec((1, tl, nu), lambda bi, ti: (bi, ti, 0)),
                   pl.BlockSpec((1, tl, e), lambda bi, ti: (bi, ti, 0))],
        out_shape=[jax.ShapeDtypeStruct((b, l, nu), F32), jax.ShapeDtypeStruct((b, l, e), F32)],
        compiler_params=_cparams("parallel", "arbitrary"),
        name="hyena_proj",
    )(x, x, x, mod, ng.reshape(1, d), w.astype(BF16), short_w, short_b.reshape(1, nu))


def _cmm(lhs, xre, xim):
    r = lhs.shape[0] // 2
    w = xre.shape[1]
    x = xre if xim is None else jnp.concatenate([xre, xim], axis=1)
    xh, xl = _split(x)
    out = _dot(lhs, jnp.concatenate([xh, xl, xh], axis=0))
    re_x, im_x = out[:r], out[r:]
    if xim is None:
        return re_x, im_x
    return re_x[:, :w] - im_x[:, w:], im_x[:, :w] + re_x[:, w:]


def _kstack(mre, mim, k):
    m = jnp.concatenate([mre, mim], axis=0)
    hi, lo = _split(m)
    if m.shape[1] == k:
        return jnp.concatenate([hi, hi, lo], axis=1)
    lane = lax.broadcasted_iota(jnp.int32, m.shape, 1)
    return jnp.where(lane < 2 * k, hi, lo)


def _fft_fwd_outer(src_re, src_im, dst_re, dst_im, fa_ref, n1, n2, k1, scale=None):
    fa = fa_ref[...]

    def body(i, carry):
        xre = src_re[pl.ds(i, k1, stride=n2), :]
        xim = None if src_im is None else src_im[pl.ds(i, k1, stride=n2), :]
        if scale is not None:
            xre = xre * scale
        are, aim = _cmm(fa, xre, xim)
        dst_re[pl.ds(i, n1, stride=n2), :] = are
        dst_im[pl.ds(i, n1, stride=n2), :] = aim
        return carry

    lax.fori_loop(0, n2, body, 0, unroll=FFT_UNROLL)


def _twiddled_inner(c2_ref, s2_ref, twr_ref, twi_ref, k):
    tre = twr_ref[pl.ds(k, 1), :]
    tim = twi_ref[pl.ds(k, 1), :]
    c2 = c2_ref[...]
    s2 = s2_ref[...]
    return _kstack(c2 * tre - s2 * tim, c2 * tim + s2 * tre, FFT_N2)


def _lconv_kernel(u_ref, x_ref, kf_ref, bias_ref, fa_ref, c2_ref, s2_ref, twr_ref, twi_ref, fc_ref,
                  c1_ref, s1_ref, twtr_ref, twti_ref, o_ref, wr_ref, wi_ref):
    l = u_ref.shape[1]
    n2 = FFT_N2
    n1 = 2 * l // n2
    half = n1 // 2

    _fft_fwd_outer(u_ref.at[0], u_ref.at[1], wr_ref, wi_ref, fa_ref, n1, n2, half)

    fc = fc_ref[...]

    def slab(k, carry):
        rows = pl.ds(pl.multiple_of(k * n2, n2), n2)
        g = _twiddled_inner(c2_ref, s2_ref, twr_ref, twi_ref, k)
        xre, xim = _cmm(g, wr_ref[rows, :], wi_ref[rows, :])
        kre = kf_ref[0, 0, rows, :]
        kim = kf_ref[0, 1, rows, :]
        yre = xre * kre - xim * kim
        yim = xre * kim + xim * kre
        are, aim = _cmm(fc, yre, yim)
        wr_ref[rows, :] = are
        wi_ref[rows, :] = aim
        return carry

    lax.fori_loop(0, n1, slab, 0, unroll=FFT_UNROLL)

    bias = bias_ref[0]

    def outer(i, carry):
        tre = twtr_ref[pl.ds(i, 1), :]
        tim = twti_ref[pl.ds(i, 1), :]
        c1 = c1_ref[...]
        s1 = s1_ref[...]
        qs = _kstack(c1 * tre - s1 * tim, -(c1 * tim + s1 * tre), n1)
        yre, yim = _cmm(qs, wr_ref[pl.ds(i, n1, stride=n2), :], wi_ref[pl.ds(i, n1, stride=n2), :])
        rows = pl.ds(i, half, stride=n2)
        u0 = u_ref[0, rows, :]
        u1 = u_ref[1, rows, :]
        o_ref[0, rows, :] = x_ref[0, rows, :] * (yre + u0 * bias)
        o_ref[1, rows, :] = x_ref[1, rows, :] * (yim + u1 * bias)
        return carry

    lax.fori_loop(0, n2, outer, 0, unroll=FFT_UNROLL)


FILTER_CHUNK = 512


def _filter_pos(base, chunk, l):
    m = base + lax.broadcasted_iota(jnp.int32, (chunk, 1), 0)
    is_f = m < l
    pos = jnp.where(is_f, m, 2 * l - m).astype(F32)
    return m, is_f, pos


def _filter_mlp_kernel(l, w1_ref, b1_ref, wm_ref, bm_ref, fr_ref, bands_ref, z_ref):
    chunk = z_ref.shape[0]
    _, _, pos = _filter_pos(pl.program_id(0) * chunk, chunk, l)
    lane = lax.broadcasted_iota(jnp.int32, (1, LANES), 1)
    t = pos / max(l - 1, 1)
    ang = (2.0 * math.pi / l) * pos * bands_ref[...]
    feat = jnp.where(lane == 0, t,
                     jnp.where(lane <= POS_BANDS, jnp.cos(ang),
                               jnp.where(lane <= 2 * POS_BANDS, -jnp.sin(ang), 0.0)))
    z = jnp.sin(fr_ref[0:1, :] * (_dot_x3(feat, w1_ref[...]) + b1_ref[...]))
    for i in range(2):
        z = jnp.sin(fr_ref[i + 1:i + 2, :] * (_dot_x3(z, wm_ref[i]) + bm_ref[i:i + 1, :]))
    z_ref[...] = z


def _filter_kernel(z_ref, wof_ref, wob_ref, dec_ref, fa_ref, c2_ref, s2_ref, twr_ref, twi_ref, o_ref):
    n = o_ref.shape[2]
    l = n // 2
    n2 = FFT_N2
    n1 = n // n2
    chunk = min(FILTER_CHUNK, n)
    nchunk = n // chunk
    wof = _split(wof_ref[...])
    wob = _split(wob_ref[...])
    dec_f = jnp.abs(dec_ref[0, 0:1, :])
    dec_b = jnp.abs(dec_ref[0, 1:2, :])

    def gen(c, asum):
        base = pl.multiple_of(c * chunk, chunk)
        m, is_f, pos = _filter_pos(base, chunk, l)
        t = pos / max(l - 1, 1)
        z = z_ref[pl.ds(base, chunk), :]
        hf = _dot_presplit(z, *wof) * jnp.exp(-t * dec_f)
        hb = _dot_presplit(z, *wob) * jnp.exp(-t * dec_b)
        k2 = jnp.where(is_f, hf, jnp.where(m == l, 0.0, hb))
        o_ref[0, 0, pl.ds(base, chunk), :] = k2
        return asum + jnp.sum(jnp.abs(k2), axis=0, keepdims=True)

    asum = lax.fori_loop(0, nchunk, gen, jnp.zeros((1, LANES), F32))
    inv = 1.0 / asum

    re_ref = o_ref.at[0, 0]
    im_ref = o_ref.at[0, 1]
    _fft_fwd_outer(re_ref, None, re_ref, im_ref, fa_ref, n1, n2, n1, scale=inv)

    def slab(k, carry):
        rows = pl.ds(pl.multiple_of(k * n2, n2), n2)
        g = _twiddled_inner(c2_ref, s2_ref, twr_ref, twi_ref, k)
        xre, xim = _cmm(g, re_ref[rows, :], im_ref[rows, :])
        re_ref[rows, :] = xre
        im_ref[rows, :] = xim
        return carry

    lax.fori_loop(0, n1, slab, 0, unroll=FFT_UNROLL)


def _dft_tables(l):
    n = 2 * l
    n2 = FFT_N2
    n1 = n // n2
    half = n1 // 2

    def cs(rows, cols, period):
        ang = 2.0 * np.pi * ((np.arange(rows)[:, None] * np.arange(cols)[None, :]) % period) / period
        return np.cos(ang), -np.sin(ang)

    def split_np(m):
        m32 = jnp.asarray(m, F32)
        hi = m32.astype(BF16)
        lo = (m32 - hi.astype(F32)).astype(BF16)
        return hi, lo

    def kstack_np(c, s):
        ch, cl = split_np(c)
        sh, sl = split_np(s)
        return jnp.concatenate([jnp.concatenate([ch, ch, cl], axis=1),
                                jnp.concatenate([sh, sh, sl], axis=1)], axis=0)

    def wide(m):
        return jnp.asarray(m if m.shape[1] % LANES == 0 else np.tile(m, (1, 3)), F32)

    c1, s1 = cs(n1, n1, n1)
    c2, s2 = cs(n2, n2, n2)
    tw_c, tw_s = cs(n1, n2, n)
    tabs = {}
    tabs["fa_half"] = kstack_np(c1[:, :half], s1[:, :half])
    tabs["fa_full"] = kstack_np(c1, s1)
    tabs["fc"] = kstack_np(c2, -s2)
    tabs["c2"] = wide(c2)
    tabs["s2"] = wide(s2)
    tabs["twr"] = wide(tw_c)
    tabs["twi"] = wide(tw_s)
    tabs["c1h"] = wide(c1[:half] / n)
    tabs["s1h"] = wide(s1[:half] / n)
    tabs["twtr"] = wide(tw_c.T)
    tabs["twti"] = wide(tw_s.T)
    return tabs


def _const_spec(a, nd_grid):
    shape = a.shape
    if nd_grid == 2:
        return pl.BlockSpec(shape, lambda i, j: (0,) * len(shape))
    return pl.BlockSpec(shape, lambda i: (0,) * len(shape))


def _hyena_filters(l, p, tabs):
    (f_w1, f_b1, f_w_mid, f_b_mid, f_freq, f_w_out, f_decay) = p
    n = 2 * l
    e = WIDTH
    nblk = e // LANES
    w1p = jnp.zeros((LANES, FILTER_HIDDEN), F32).at[:f_w1.shape[0]].set(f_w1)
    bands = np.linspace(1e-4, POS_BANDS - 1, POS_BANDS, dtype=np.float32)
    bl = np.zeros((1, LANES), np.float32)
    bl[0, 1:1 + POS_BANDS] = bands
    bl[0, 1 + POS_BANDS:1 + 2 * POS_BANDS] = bands
    consts = [tabs["fa_full"], tabs["c2"], tabs["s2"], tabs["twr"], tabs["twi"]]
    chunk = min(FILTER_CHUNK, n)
    mlp_in = [w1p, f_b1.reshape(1, FILTER_HIDDEN), f_w_mid, f_b_mid, f_freq, jnp.asarray(bl)]
    z = pl.pallas_call(
        functools.partial(_filter_mlp_kernel, l),
        grid=(n // chunk,),
        in_specs=[_const_spec(a, 1) for a in mlp_in],
        out_specs=pl.BlockSpec((chunk, FILTER_HIDDEN), lambda c: (c, 0)),
        out_shape=jax.ShapeDtypeStruct((n, FILTER_HIDDEN), F32),
        compiler_params=_cparams("parallel"),
        name="hyena_filter_mlp",
    )(*mlp_in)
    return pl.pallas_call(
        _filter_kernel,
        grid=(2, nblk),
        in_specs=[
            pl.BlockSpec((n, FILTER_HIDDEN), lambda o, j: (0, 0)),
            pl.BlockSpec((FILTER_HIDDEN, LANES), lambda o, j: (0, o * 2 * nblk + j)),
            pl.BlockSpec((FILTER_HIDDEN, LANES), lambda o, j: (0, o * 2 * nblk + nblk + j)),
            pl.BlockSpec((1, 2, LANES), lambda o, j: (o, 0, j)),
        ] + [_const_spec(c, 2) for c in consts],
        out_specs=pl.BlockSpec((1, 2, n, LANES), lambda o, j: (o, 0, 0, j)),
        out_shape=jax.ShapeDtypeStruct((2, 2, n, e), F32),
        compiler_params=_cparams("parallel", "arbitrary"),
        name="hyena_filter",
    )(z, f_w_out, f_w_out, f_decay.reshape(2, 2, e), *consts)


def _long_conv(u, xg, kf, order, bias, tabs, col_u, col_x):
    b, l, _ = u.shape
    e = WIDTH
    n = 2 * l
    nblk = e // LANES
    consts = [tabs["fa_half"], tabs["c2"], tabs["s2"], tabs["twr"], tabs["twi"],
              tabs["fc"], tabs["c1h"], tabs["s1h"], tabs["twtr"], tabs["twti"]]
    return pl.pallas_call(
        _lconv_kernel,
        grid=(nblk, b // 2),
        in_specs=[
            pl.BlockSpec((2, l, LANES), lambda j, pi: (pi, 0, col_u * nblk + j)),
            pl.BlockSpec((2, l, LANES), lambda j, pi: (pi, 0, col_x * nblk + j)),
            pl.BlockSpec((1, 2, n, LANES), lambda j, pi: (order, 0, 0, j)),
            pl.BlockSpec((1, 1, LANES), lambda j, pi: (order, 0, j)),
        ] + [_const_spec(c, 2) for c in consts],
        out_specs=pl.BlockSpec((2, l, LANES), lambda j, pi: (pi, 0, j)),
        out_shape=jax.ShapeDtypeStruct((b, l, e), F32),
        scratch_shapes=[pltpu.VMEM((n, LANES), F32), pltpu.VMEM((n, LANES), F32)],
        compiler_params=_cparams("parallel", "arbitrary"),
        name="hyena_long_conv",
    )(u, xg, kf, bias.reshape(2, 1, e), *consts)


def _gated_out_kernel(z_ref, g_ref, x_ref, mod_ref, w_ref, o_ref):
    zz = (z_ref[0] * _silu(g_ref[0])).astype(BF16)
    o_ref[0] = x_ref[0] + mod_ref[0, 2:3, :] * _dot(zz, w_ref[...])


def _gated_out(z, g_arr, g_col, x, mod, w_out, name):
    b, l, d = x.shape
    e = WIDTH
    tl = 256
    tile = lambda col: pl.BlockSpec((1, tl, e), lambda bi, ti: (bi, ti, col))
    return pl.pallas_call(
        _gated_out_kernel,
        grid=(b, l // tl),
        in_specs=[tile(0), tile(g_col), tile(0),
                  pl.BlockSpec((1, 3, d), lambda bi, ti: (bi, 0, 0)),
                  pl.BlockSpec((e, d), lambda bi, ti: (0, 0))],
        out_specs=tile(0),
        out_shape=jax.ShapeDtypeStruct((b, l, d), F32),
        compiler_params=_cparams("parallel", "arbitrary"),
        name=name,
    )(z, g_arr, x, mod, w_out.astype(BF16))


def _hyena_layer(x, mod, ng, p, tabs):
    (w_in, short_w, short_b, f_w1, f_b1, f_w_mid, f_b_mid, f_freq, f_w_out, f_decay, bias, w_out) = p
    l = x.shape[1]
    u3, g = _hy_proj(x, mod, ng, w_in, short_w, short_b)
    kf = _hyena_filters(l, (f_w1, f_b1, f_w_mid, f_b_mid, f_freq, f_w_out, f_decay), tabs)
    z1 = _long_conv(u3, u3, kf, 0, bias, tabs, col_u=2, col_x=0)
    z2 = _long_conv(z1, u3, kf, 1, bias, tabs, col_u=0, col_x=1)
    return _gated_out(z2, g, 0, x, mod, w_out, "hyena_out")


def _cf_proj_kernel(x_ref, mod_ref, ng_ref, w_ref, u_o, sg_o):
    e = WIDTH
    h = _prenorm(x_ref[0], ng_ref[...], mod_ref[0, 0:1, :], mod_ref[0, 1:2, :]).astype(BF16)
    a = _dot(h, w_ref[:, 0:e])
    bgate = _dot(h, w_ref[:, e:2 * e])
    g = _dot(h, w_ref[:, 2 * e:3 * e])
    u_o[0] = a * _sigmoid(bgate)
    sg_o[0] = _silu(g)


def _cf_out_kernel(u_ref, up_ref, un_ref, sg_ref, x_ref, mod_ref, dw_ref, db_ref, lg_ref, lb_ref, w_ref,
                   o_ref, ext_ref):
    t = pl.program_id(1)
    nt = pl.num_programs(1)
    tl = u_ref.shape[1]
    hh = CONV_HALO
    ext_ref[0:hh, :] = jnp.where(t > 0, up_ref[0], 0.0)
    ext_ref[hh:hh + tl, :] = u_ref[0]
    ext_ref[hh + tl:hh + tl + hh, :] = jnp.where(t < nt - 1, un_ref[0], 0.0)
    off = hh - (CONV_WIDTH - 1) // 2
    acc = jnp.zeros((tl, WIDTH), F32) + db_ref[...]
    ext = ext_ref[...]
    nrow = ext.shape[0]
    for phase in range(8):
        win = ext if phase == 0 else pltpu.roll(ext, nrow - phase, 0)
        for k in range(CONV_WIDTH):
            if (off + k) % 8 == phase:
                base = off + k - phase
                acc = acc + win[base:base + tl, :] * dw_ref[k:k + 1, :]
    mean = jnp.mean(acc, axis=-1, keepdims=True)
    cen = acc - mean
    var = jnp.mean(cen * cen, axis=-1, keepdims=True)
    y = cen * lax.rsqrt(var + LN_EPS) * lg_ref[...] + lb_ref[...]
    zz = (_silu(y) * sg_ref[0]).astype(BF16)
    o_ref[0] = x_ref[0] + mod_ref[0, 2:3, :] * _dot(zz, w_ref[...])


def _conformer_layer(x, mod, ng, p):
    (w_in, dw_w, dw_b, ln_g, ln_b, w_out) = p
    b, l, d = x.shape
    e = WIDTH
    tl = 256
    tile = pl.BlockSpec((1, tl, e), lambda bi, ti: (bi, ti, 0))
    modspec = pl.BlockSpec((1, 3, d), lambda bi, ti: (bi, 0, 0))
    vec = pl.BlockSpec((1, e), lambda bi, ti: (0, 0))
    u, sg = pl.pallas_call(
        _cf_proj_kernel,
        grid=(b, l // tl),
        in_specs=[tile, modspec, vec, pl.BlockSpec((d, 3 * e), lambda bi, ti: (0, 0))],
        out_specs=[tile, tile],
        out_shape=[jax.ShapeDtypeStruct((b, l, e), F32)] * 2,
        compiler_params=_cparams("parallel", "arbitrary"),
        name="conformer_proj",
    )(x, mod, ng.reshape(1, d), w_in.astype(BF16))

    hh = CONV_HALO
    hb = tl // hh
    lb = l // hh
    return pl.pallas_call(
        _cf_out_kernel,
        grid=(b, l // tl),
        in_specs=[
            tile,
            pl.BlockSpec((1, hh, e), lambda bi, ti: (bi, jnp.maximum(ti * hb - 1, 0), 0)),
            pl.BlockSpec((1, hh, e), lambda bi, ti: (bi, jnp.minimum((ti + 1) * hb, lb - 1), 0)),
            tile, tile, modspec,
            pl.BlockSpec((CONV_WIDTH, e), lambda bi, ti: (0, 0)),
            vec, vec, vec,
            pl.BlockSpec((e, d), lambda bi, ti: (0, 0)),
        ],
        out_specs=tile,
        out_shape=jax.ShapeDtypeStruct((b, l, d), F32),
        scratch_shapes=[pltpu.VMEM((tl + 2 * hh, e), F32)],
        compiler_params=_cparams("parallel", "arbitrary"),
        name="conformer_out",
    )(u, u, u, sg, x, mod, dw_w, dw_b.reshape(1, e), ln_g.reshape(1, e), ln_b.reshape(1, e),
      w_out.astype(BF16))


def _final_norm_kernel(x_ref, g_ref, o_ref):
    x = x_ref[0]
    ms = jnp.mean(x * x, axis=-1, keepdims=True)
    o_ref[0] = x * lax.rsqrt(ms + RMS_EPS) * g_ref[...]


def _final_norm(x, g):
    b, l, d = x.shape
    tl = 512
    tile = pl.BlockSpec((1, tl, d), lambda bi, ti: (bi, ti, 0))
    return pl.pallas_call(
        _final_norm_kernel,
        grid=(b, l // tl),
        in_specs=[tile, pl.BlockSpec((1, d), lambda bi, ti: (0, 0))],
        out_specs=tile,
        out_shape=jax.ShapeDtypeStruct((b, l, d), F32),
        compiler_params=_cparams("parallel", "arbitrary"),
        name="final_norm",
    )(x, g.reshape(1, d))


def _seg_constants():
    seg = np.arange(SEG_GROUP) // HEAD
    bd = (seg[:, None] == seg[None, :]).astype(np.float32)
    col_head = np.arange(SEG_GROUP // HEAD * SCAN_CHUNK) // SCAN_CHUNK
    bd_sj = (col_head[:, None] == seg[None, :]).astype(np.float32)
    blk = np.arange(SEG_GROUP) // SCAN_CHUNK
    bd_inv = (blk[:, None] == blk[None, :]).astype(np.float32)
    return {"bd": jnp.asarray(bd, BF16), "bd_sj": jnp.asarray(bd_sj, BF16), "bd_inv": jnp.asarray(bd_inv, BF16)}


def _trunk(x, mod, norm_g, rw, hy, cf, final_g, consts):
    tabs = None
    for i in range(DEPTH):
        kind, j = i % 3, i // 3
        m = mod[i]
        last = i == DEPTH - 1
        if kind == 0:
            x = _rwkv_layer(x, m, norm_g[i], [p[j] for p in rw], consts, final_g if last else None)
            if last:
                return x
        elif kind == 1:
            if tabs is None:
                tabs = _dft_tables(x.shape[1])
            x = _hyena_layer(x, m, norm_g[i], [p[j] for p in hy], tabs)
        else:
            x = _conformer_layer(x, m, norm_g[i], [p[j] for p in cf])
    return _final_norm(x, final_g)


def kernel(x_prompt, x_sample, c_prompt, c_sample, norm_g, mod_w, mod_b, rw_w_in, rw_mu, rw_w0, rw_w2, rw_a0, rw_a2, rw_k_k, rw_k_a, rw_r_k, rw_gn_w, rw_gn_b, rw_w_out, hy_w_in, hy_short_w, hy_short_b, hy_f_w1, hy_f_b1, hy_f_w_mid, hy_f_b_mid, hy_f_freq, hy_f_w_out, hy_f_decay, hy_bias, hy_w_out, cf_w_in, cf_dw_w, cf_dw_b, cf_ln_g, cf_ln_b, cf_w_out, final_g):
    rw = (rw_w_in, rw_mu, rw_w0, rw_w2, rw_a0, rw_a2, rw_k_k, rw_k_a, rw_r_k, rw_gn_w, rw_gn_b, rw_w_out)
    hy = (hy_w_in, hy_short_w, hy_short_b, hy_f_w1, hy_f_b1, hy_f_w_mid, hy_f_b_mid, hy_f_freq, hy_f_w_out,
          hy_f_decay, hy_bias, hy_w_out)
    cf = (cf_w_in, cf_dw_w, cf_dw_b, cf_ln_g, cf_ln_b, cf_w_out)
    consts = _seg_constants()
    bp = x_prompt.shape[0]
    mod = _modulation(jnp.concatenate([c_prompt, c_sample], axis=0), mod_w, mod_b)
    y_prompt = _trunk(x_prompt, mod[:, :bp], norm_g, rw, hy, cf, final_g, consts)
    y_sample = _trunk(x_sample, mod[:, bp:], norm_g, rw, hy, cf, final_g, consts)
    return (y_prompt, y_sample)
```

```python
import functools
import math

import numpy as np
import jax
import jax.numpy as jnp
from jax import lax
from jax.experimental import pallas as pl
from jax.experimental.pallas import tpu as pltpu

F32 = jnp.float32
BF16 = jnp.bfloat16

D_MODEL = 1024
WIDTH = 1024
DEPTH = 4
HEADS = 16
HEAD = 64
LORA = 64
RMS_EPS = 1e-6
LN_EPS = 1e-5
GN_EPS = 64e-5
POS_BANDS = 16
FILTER_HIDDEN = 64
CONV_WIDTH = 31
CONV_HALO = 16
LANES = 128
SEG_GROUP = 256
FFT_N2 = 64
FFT_UNROLL = 8
VMEM_LIMIT = 56 * 1024 * 1024


def _cparams(*sem):
    return pltpu.CompilerParams(dimension_semantics=sem, vmem_limit_bytes=VMEM_LIMIT)


def _dot(a, b):
    return jnp.dot(a, b, preferred_element_type=F32)


def _split(x):
    hi = x.astype(BF16)
    lo = (x - hi.astype(F32)).astype(BF16)
    return hi, lo


def _dot_x3(a, b):
    ah, al = _split(a)
    bh, bl = _split(b)
    return _dot(ah, bh) + _dot(ah, bl) + _dot(al, bh)


def _dot_presplit(a, bh, bl):
    ah, al = _split(a)
    return _dot(ah, bh) + _dot(ah, bl) + _dot(al, bh)


def _dot_kstacked(a, b_stack):
    ah, al = _split(a)
    return _dot(jnp.concatenate([ah, al, ah], axis=1), b_stack)


def _seg_bcast(t, bd):
    m = t.shape[0]
    outs = []
    for g in range(t.shape[1] // SEG_GROUP):
        th, tl = _split(t[:, g * SEG_GROUP:(g + 1) * SEG_GROUP])
        res = _dot(jnp.concatenate([th, tl], axis=0), bd)
        outs.append(res[:m] + res[m:])
    return jnp.concatenate(outs, axis=1)


def _sigmoid(x):
    return 1.0 / (1.0 + jnp.exp(-x))


def _silu(x):
    return x * _sigmoid(x)


def _prenorm(x, ng, shift, scale):
    ms = jnp.mean(x * x, axis=-1, keepdims=True)
    return (x * lax.rsqrt(ms + RMS_EPS) * ng) * (1.0 + scale) + shift


def _mod_kernel(c_ref, w_ref, b_ref, o_ref):
    o_ref[0] = _dot_x3(_silu(c_ref[...]), w_ref[0]) + b_ref[0]


def _modulation(c_all, mod_w, mod_b):
    bc = c_all.shape[0]
    out = pl.pallas_call(
        _mod_kernel,
        grid=(DEPTH, 3),
        in_specs=[
            pl.BlockSpec((bc, D_MODEL), lambda i, j: (0, 0)),
            pl.BlockSpec((1, D_MODEL, D_MODEL), lambda i, j: (i, 0, j)),
            pl.BlockSpec((1, 1, D_MODEL), lambda i, j: (i, 0, j)),
        ],
        out_specs=pl.BlockSpec((1, bc, D_MODEL), lambda i, j: (i, 0, j)),
        out_shape=jax.ShapeDtypeStruct((DEPTH, bc, 3 * D_MODEL), F32),
        compiler_params=_cparams("arbitrary", "arbitrary"),
        name="modulation",
    )(c_all, mod_w, mod_b.reshape(DEPTH, 1, 3 * D_MODEL))
    return out.reshape(DEPTH, bc, 3, D_MODEL)


RW_TL = 256
SCAN_BATCH = 2
SCAN_CHUNK = 32


def _rw_proj_kernel(x_ref, xp_ref, xn_ref, mod_ref, ng_ref, mu_ref, win_ref,
                    w0_ref, w2_ref, a0_ref, a2_ref,
                    kk_ref, ka_ref, rk_ref, bd_ref,
                    r_o, v_o, nkk_o, g_o, bv_o, w0_o, k0_o, b0_o, w1_o, k1_o, b1_o):
    t = pl.program_id(1)
    nt = pl.num_programs(1)
    tl = x_ref.shape[1]
    e = WIDTH
    shift = mod_ref[0, 0:1, :]
    scale = mod_ref[0, 1:2, :]
    ng = ng_ref[...]
    bd = bd_ref[...]

    h = _prenorm(x_ref[0], ng, shift, scale)
    hp = _prenorm(xp_ref[0], ng, shift, scale)[7:8, :]
    hn = _prenorm(xn_ref[0], ng, shift, scale)[0:1, :]
    hp = jnp.where(t > 0, hp, 0.0)
    hn = jnp.where(t < nt - 1, hn, 0.0)
    rows = lax.broadcasted_iota(jnp.int32, (tl, 1), 0)
    h_up = jnp.where(rows == 0, hp, pltpu.roll(h, 1, 0))
    h_dn = jnp.where(rows == tl - 1, hn, pltpu.roll(h, tl - 1, 0))
    xx = 0.5 * (h_up + h_dn) - h

    def proj(i, c0, c1):
        inp = (h + xx * mu_ref[i:i + 1, :]).astype(BF16)
        return _dot(inp, win_ref[:, c0:c1])

    r = proj(0, 0, e)
    k = proj(1, e, 2 * e)
    v = proj(2, 2 * e, 3 * e)
    g = proj(3, 3 * e, 4 * e)
    wl = proj(4, 4 * e, 4 * e + 2 * LORA)
    al = proj(5, 4 * e + 2 * LORA, 4 * e + 4 * LORA)
    twl = jnp.tanh(wl)

    kk0 = k * kk_ref[...]
    nrm = jnp.sqrt(_seg_bcast(kk0 * kk0, bd))
    kk = kk0 / jnp.maximum(nrm, 1e-12)
    ka = ka_ref[...]

    r_o[0] = r
    v_o[0] = v
    g_o[0] = g
    nkk_o[0] = -kk

    ksum = jnp.zeros_like(k)
    for d, (w_o, k_o, b_o) in enumerate(((w0_o, k0_o, b0_o), (w1_o, k1_o, b1_o))):
        w_raw = w0_ref[d:d + 1, :] + _dot_kstacked(twl, w2_ref[d])
        z = -w_raw
        softplus = jnp.maximum(z, 0.0) + jnp.log(1.0 + jnp.exp(-jnp.abs(z)))
        w_log = -softplus - 0.5
        w_o[0] = -jnp.exp(w_log)
        a = _sigmoid(a0_ref[d:d + 1, :] + _dot_kstacked(al, a2_ref[d]))
        k_d = k * (1.0 + (a - 1.0) * ka)
        k_o[0] = k_d
        b_o[0] = kk * a
        ksum = ksum + k_d

    bonus = _seg_bcast(r * ksum * rk_ref[...], bd)
    bv_o[0] = bonus * v


def _dot_nt(a, b):
    return lax.dot_general(a, b, (((1,), (1,)), ((), ())), preferred_element_type=F32)


def _dot_tn(a, b):
    return lax.dot_general(a, b, (((0,), (0,)), ((), ())), preferred_element_type=F32)


def _rw_chunk_kernel(rf, vf, af, wf, kf, bf, rb, vb, ab, wb, kb, bb, bd_ref, bdsj_ref, bdinv_ref,
                     yf_o, yb_o, s_ref):
    tb = pl.program_id(1)
    nb, c, _ = rf.shape
    grp = SEG_GROUP
    hpg = grp // HEAD
    sw = hpg * c
    pair = grp // sw

    @pl.when(tb == 0)
    def _():
        s_ref[...] = jnp.zeros_like(s_ref)

    row = lax.broadcasted_iota(jnp.int32, (c, sw), 0)
    col = lax.broadcasted_iota(jnp.int32, (c, sw), 1) % c
    eye_inv = (lax.broadcasted_iota(jnp.int32, (c, grp), 0)
               == lax.broadcasted_iota(jnp.int32, (c, grp), 1) % c)
    eye_state = (lax.broadcasted_iota(jnp.int32, (HEAD, grp), 0)
                 == lax.broadcasted_iota(jnp.int32, (HEAD, grp), 1) % HEAD)
    tr = lax.broadcasted_iota(jnp.int32, (c, 3 * c), 0)
    tcq = lax.broadcasted_iota(jnp.int32, (c, 3 * c), 1) % c
    bd = bd_ref[...]
    bd_sj = bdsj_ref[...]
    bd_inv = bdinv_ref[...]
    bd_f32 = bd.astype(F32)

    def blockdiag(x, mask):
        reps = mask.shape[0] // x.shape[0]
        xh, xl = _split(x)
        return (jnp.concatenate([xh] * reps, axis=0) * mask, jnp.concatenate([xl] * reps, axis=0) * mask)

    def pmm(lhs, w, dot=_dot, single=None):
        m = lhs.shape[0]
        lh, ll = _split(lhs)
        stack = [lh, ll] if single is None else [lh, ll, single.astype(BF16)]
        res = dot(jnp.concatenate(stack, axis=0), w[0])
        full = res[:m] + res[m:2 * m] + dot(lh, w[1])
        return full if single is None else (full, res[2 * m:])

    def pmm1(lhs, w):
        return _dot(lhs.astype(BF16), w[0])

    def chain_operands(q, rev, refs):
        r, v, a, lw, k, b = [ref[q] for ref in refs]
        tri = (tcq >= tr if rev else tcq <= tr).astype(BF16)
        p1 = lw.astype(BF16)
        rem = lw - p1.astype(F32)
        p2 = rem.astype(BF16)
        p3 = (rem - p2.astype(F32)).astype(BF16)
        cs = _dot(tri, jnp.concatenate([p1, p2, p3], axis=0))
        total = cs[0:1] if rev else cs[c - 1:c]
        e_neg = jnp.exp(-cs)
        e_bar = jnp.exp(total - cs)
        return dict(at=a * jnp.exp(cs - lw), rt=r * jnp.exp(cs), bt=b * e_neg, kt=k * e_neg,
                    bb=b * e_bar, kb=k * e_bar, v=v, gc=jnp.exp(total),
                    strict=(col > row) if rev else (col < row),
                    incl=(col >= row) if rev else (col <= row))

    chains = [(q, rev, refs, y_o)
              for q in range(nb)
              for rev, refs, y_o in ((False, (rf, vf, af, wf, kf, bf), yf_o), (True, (rb, vb, ab, wb, kb, bb), yb_o))]
    ops = [chain_operands(q, rev, refs) for q, rev, refs, _ in chains]
    probs = [(ch, g) for ch in range(len(chains)) for g in range(WIDTH // grp)]
    pick = lambda name: [ops[ch][name][:, g * grp:(g + 1) * grp] for ch, g in probs]
    mask = lambda name: [ops[ch][name] for ch, _ in probs]
    at, rt, bt, kt, bb, kb, vv, gc = (pick(n) for n in ("at", "rt", "bt", "kt", "bb", "kb", "v", "gc"))
    strict, incl = mask("strict"), mask("incl")

    sc_b = [pmm(x, blockdiag(w, bd_sj), _dot_nt, single=y) for x, y, w in zip(at, rt, bt)]
    sc_k = [pmm(x, blockdiag(w, bd_sj), _dot_nt, single=y) for x, y, w in zip(at, rt, kt)]
    mab = [jnp.where(m, s[0], 0.0) for m, s in zip(strict, sc_b)]
    mak = [jnp.where(m, s[0], 0.0) for m, s in zip(strict, sc_k)]
    nrb = [jnp.where(m, s[1], 0.0) for m, s in zip(incl, sc_b)]
    nrk = [jnp.where(m, s[1], 0.0) for m, s in zip(incl, sc_k)]
    wide = [jnp.concatenate(mab[i:i + pair], axis=1) for i in range(0, len(probs), pair)]
    inv = [jnp.where(eye_inv, 1.0, m) for m in wide]
    pw = [pmm(m, blockdiag(m, bd_inv)) for m in wide]
    doublings = int(math.log2(c)) - 1
    for it in range(doublings):
        if it == doublings - 1:
            inv = [i + pmm(i, blockdiag(p, bd_inv)) for i, p in zip(inv, pw)]
        else:
            both = [pmm(jnp.concatenate([p, i], axis=0), blockdiag(p, bd_inv)) for i, p in zip(inv, pw)]
            pw = [x[:c] for x in both]
            inv = [i + x[c:] for i, x in zip(inv, both)]
    inv = [w[:, j * sw:(j + 1) * sw] for w in inv for j in range(pair)]
    mv = [pmm(x, blockdiag(w, bd_sj), single=y) for x, y, w in zip(mak, nrk, vv)]
    dg = [jnp.where(eye_state, x, 0.0) for x in gc]

    xy = [pmm(jnp.concatenate([a_, d_], axis=0), blockdiag(s_ref[ch, g], bd), single=r_)
          for a_, r_, d_, (ch, g) in zip(at, rt, dg, probs)]
    x = [t[0] for t in xy]
    p = [pmm(i, blockdiag(x_[:c] + z[0], bd_sj)) for i, x_, z in zip(inv, x, mv)]
    for t, z, n, p_, (ch, g) in zip(xy, mv, nrb, p, probs):
        q, _, _, y_o = chains[ch]
        y_o[q, :, g * grp:(g + 1) * grp] = t[1] + z[1] + pmm1(n, blockdiag(p_, bd_sj))
    for x_, p_, v_, b_, k_, (ch, g) in zip(x, p, vv, bb, kb, probs):
        bk_h, bk_l = _split(jnp.concatenate([b_, k_], axis=0))
        pv_h, pv_l = _split(jnp.concatenate([p_, v_], axis=0))
        full = _dot_tn(jnp.concatenate([bk_h, bk_h, bk_l], axis=0),
                       jnp.concatenate([pv_h, pv_l, pv_h], axis=0)) * bd_f32
        upd = full[0:HEAD]
        for hh in range(1, hpg):
            upd = upd + full[hh * HEAD:(hh + 1) * HEAD]
        s_ref[ch, g] = x_[c:] + upd


def _rw_out_kernel(final, yf_ref, yb_ref, bv_ref, g_ref, x_ref, mod_ref, gw_ref, gb_ref, wout_ref, bd_ref,
                   fg_ref, o_ref):
    bd = bd_ref[...]
    y = yf_ref[0] + yb_ref[0]
    mean = _seg_bcast(y, bd) * (1.0 / HEAD)
    yc = y - mean
    var = _seg_bcast(yc * yc, bd) * (1.0 / HEAD)
    yn = yc * lax.rsqrt(var + GN_EPS) * gw_ref[...] + gb_ref[...]
    yo = (yn + bv_ref[0]) * _silu(g_ref[0])
    o = _dot(yo.astype(BF16), wout_ref[...])
    res = x_ref[0] + mod_ref[0, 2:3, :] * o
    if final:
        ms = jnp.mean(res * res, axis=-1, keepdims=True)
        res = res * lax.rsqrt(ms + RMS_EPS) * fg_ref[...]
    o_ref[0] = res


def _pad_lora(w):
    z = jnp.zeros_like(w[0])
    return jnp.stack([jnp.concatenate([w[0], z], axis=0), jnp.concatenate([z, w[1]], axis=0)], axis=0)


def _split_host(w):
    hi = w.astype(BF16)
    lo = (w - hi.astype(F32)).astype(BF16)
    return hi, lo


def _rwkv_layer(x, mod, ng, p, consts, final_g=None):
    (w_in, mu, w0, w2, a0, a2, k_k, k_a, r_k, gn_w, gn_b, w_out) = p
    b, l, d = x.shape
    e = WIDTH
    tl = RW_TL
    nt = l // tl
    bd = consts["bd"]
    def lora_stack(w):
        hi, lo = _split_host(_pad_lora(w))
        return jnp.concatenate([hi, hi, lo], axis=1)

    w2s, a2s = lora_stack(w2), lora_stack(a2)
    ncols = w_in.shape[1]

    full = lambda shape: pl.BlockSpec(shape, lambda bi, ti: (0,) * len(shape))
    tile = pl.BlockSpec((1, tl, e), lambda bi, ti: (bi, ti, 0))
    hb = tl // 8
    lb = l // 8
    outs = pl.pallas_call(
        _rw_proj_kernel,
        grid=(b, nt),
        in_specs=[
            tile,
            pl.BlockSpec((1, 8, d), lambda bi, ti: (bi, jnp.maximum(ti * hb - 1, 0), 0)),
            pl.BlockSpec((1, 8, d), lambda bi, ti: (bi, jnp.minimum((ti + 1) * hb, lb - 1), 0)),
            pl.BlockSpec((1, 3, d), lambda bi, ti: (bi, 0, 0)),
            full((1, d)), full((6, d)), full((d, ncols)),
            full((2, e)), full((2, 6 * LORA, e)),
            full((2, e)), full((2, 6 * LORA, e)),
            full((1, e)), full((1, e)), full((1, e)),
            full((SEG_GROUP, SEG_GROUP)),
        ],
        out_specs=[tile] * 11,
        out_shape=[jax.ShapeDtypeStruct((b, l, e), F32)] * 11,
        compiler_params=_cparams("parallel", "arbitrary"),
        name="rwkv_proj",
    )(x, x, x, mod, ng.reshape(1, d), mu, w_in.astype(BF16),
      w0, w2s, a0, a2s,
      k_k.reshape(1, e), k_a.reshape(1, e), r_k.reshape(1, e), bd)
    r, v, nkk, g, bv, wd0, kd0, bd0, wd1, kd1, bd1 = outs

    tc = SCAN_CHUNK
    ntc = l // tc
    nb = SCAN_BATCH
    fwd = pl.BlockSpec((nb, tc, e), lambda bi, ti: (bi, ti, 0))
    bwd = pl.BlockSpec((nb, tc, e), lambda bi, ti: (bi, ntc - 1 - ti, 0))
    yf, yb = pl.pallas_call(
        _rw_chunk_kernel,
        grid=(b // nb, ntc),
        in_specs=[fwd] * 6 + [bwd] * 6 + [full(consts[n].shape) for n in ("bd", "bd_sj", "bd_inv")],
        out_specs=[fwd, bwd],
        out_shape=[jax.ShapeDtypeStruct((b, l, e), F32)] * 2,
        scratch_shapes=[pltpu.VMEM((2 * nb, e // SEG_GROUP, HEAD, SEG_GROUP), F32)],
        compiler_params=_cparams("parallel", "arbitrary"),
        name="rwkv_scan",
    )(r, v, nkk, wd0, kd0, bd0, r, v, nkk, wd1, kd1, bd1, bd, consts["bd_sj"], consts["bd_inv"])

    tlo = 256
    tile_o = pl.BlockSpec((1, tlo, e), lambda bi, ti: (bi, ti, 0))
    final = final_g is not None
    fg = (final_g if final else jnp.ones((d,), F32)).reshape(1, d)
    return pl.pallas_call(
        functools.partial(_rw_out_kernel, final),
        grid=(b, l // tlo),
        in_specs=[tile_o] * 5 + [
            pl.BlockSpec((1, 3, d), lambda bi, ti: (bi, 0, 0)),
            full((1, e)), full((1, e)), full((e, d)), full((SEG_GROUP, SEG_GROUP)), full((1, d)),
        ],
        out_specs=tile_o,
        out_shape=jax.ShapeDtypeStruct((b, l, d), F32),
        compiler_params=_cparams("parallel", "arbitrary"),
        name="rwkv_out",
    )(yf, yb, bv, g, x, mod, gn_w.reshape(1, e), gn_b.reshape(1, e), w_out.astype(BF16), bd, fg)


def _hy_proj_kernel(x_ref, xp_ref, xn_ref, mod_ref, ng_ref, w_ref, sw_ref, sb_ref, u_o, g_o):
    t = pl.program_id(1)
    nt = pl.num_programs(1)
    tl = x_ref.shape[1]
    nu = u_o.shape[2]
    ng, shift, scale = ng_ref[...], mod_ref[0, 0:1, :], mod_ref[0, 1:2, :]
    rows_in = jnp.concatenate([x_ref[0], xp_ref[0], xn_ref[0]], axis=0)
    proj_all = _dot(_prenorm(rows_in, ng, shift, scale).astype(BF16), w_ref[...])
    proj = proj_all[:tl]
    prev = jnp.where(t > 0, proj_all[tl + 7:tl + 8, 0:nu], 0.0)
    nxt = jnp.where(t < nt - 1, proj_all[tl + 8:tl + 9, 0:nu], 0.0)
    u = proj[:, 0:nu]
    rows = lax.broadcasted_iota(jnp.int32, (tl, 1), 0)
    up = jnp.where(rows == 0, prev, pltpu.roll(u, 1, 0))
    dn = jnp.where(rows == tl - 1, nxt, pltpu.roll(u, tl - 1, 0))
    u_o[0] = up * sw_ref[0:1, :] + u * sw_ref[1:2, :] + dn * sw_ref[2:3, :] + sb_ref[...]
    g_o[0] = proj[:, nu:]


def _hy_proj(x, mod, ng, w, short_w, short_b):
    b, l, d = x.shape
    e = WIDTH
    nu = short_w.shape[1]
    tl = 256
    hb, lb = tl // 8, l // 8
    full = lambda shape: pl.BlockSpec(shape, lambda bi, ti: (0,) * len(shape))
    return pl.pallas_call(
        _hy_proj_kernel,
        grid=(b, l // tl),
        in_specs=[
            pl.BlockSpec((1, tl, d), lambda bi, ti: (bi, ti, 0)),
            pl.BlockSpec((1, 8, d), lambda bi, ti: (bi, jnp.maximum(ti * hb - 1, 0), 0)),
            pl.BlockSpec((1, 8, d), lambda bi, ti: (bi, jnp.minimum((ti + 1) * hb, lb - 1), 0)),
            pl.BlockSpec((1, 3, d), lambda bi, ti: (bi, 0, 0)),
            full((1, d)), full((d, nu + e)), full((3, nu)), full((1, nu)),
        ],
        out_specs=[pl.BlockSpec((1, tl, nu), lambda bi, ti: (bi, ti, 0)),
                   pl.BlockSpec((1, tl, e), lambda bi, ti: (bi, ti, 0))],
        out_shape=[jax.ShapeDtypeStruct((b, l, nu), F32), jax.ShapeDtypeStruct((b, l, e), F32)],
        compiler_params=_cparams("parallel", "arbitrary"),
        name="hyena_proj",
    )(x, x, x, mod, ng.reshape(1, d), w.astype(BF16), short_w, short_b.reshape(1, nu))


def _cmm(lhs, xre, xim):
    r = lhs.shape[0] // 2
    w = xre.shape[1]
    x = xre if xim is None else jnp.concatenate([xre, xim], axis=1)
    xh, xl = _split(x)
    out = _dot(lhs, jnp.concatenate([xh, xl, xh], axis=0))
    re_x, im_x = out[:r], out[r:]
    if xim is None:
        return re_x, im_x
    return re_x[:, :w] - im_x[:, w:], im_x[:, :w] + re_x[:, w:]


def _kstack(mre, mim, k):
    m = jnp.concatenate([mre, mim], axis=0)
    hi, lo = _split(m)
    if m.shape[1] == k:
        return jnp.concatenate([hi, hi, lo], axis=1)
    lane = lax.broadcasted_iota(jnp.int32, m.shape, 1)
    return jnp.where(lane < 2 * k, hi, lo)


def _fft_fwd_outer(src_re, src_im, dst_re, dst_im, fa_ref, n1, n2, k1, scale=None):
    fa = fa_ref[...]

    def body(i, carry):
        xre = src_re[pl.ds(i, k1, stride=n2), :]
        xim = None if src_im is None else src_im[pl.ds(i, k1, stride=n2), :]
        if scale is not None:
            xre = xre * scale
        are, aim = _cmm(fa, xre, xim)
        dst_re[pl.ds(i, n1, stride=n2), :] = are
        dst_im[pl.ds(i, n1, stride=n2), :] = aim
        return carry

    lax.fori_loop(0, n2, body, 0, unroll=FFT_UNROLL)


def _twiddled_inner(c2_ref, s2_ref, twr_ref, twi_ref, k):
    tre = twr_ref[pl.ds(k, 1), :]
    tim = twi_ref[pl.ds(k, 1), :]
    c2 = c2_ref[...]
    s2 = s2_ref[...]
    return _kstack(c2 * tre - s2 * tim, c2 * tim + s2 * tre, FFT_N2)


def _lconv_kernel(u_ref, x_ref, kf_ref, bias_ref, fa_ref, c2_ref, s2_ref, twr_ref, twi_ref, fc_ref,
                  c1_ref, s1_ref, twtr_ref, twti_ref, o_ref, wr_ref, wi_ref):
    l = u_ref.shape[1]
    n2 = FFT_N2
    n1 = 2 * l // n2
    half = n1 // 2

    _fft_fwd_outer(u_ref.at[0], u_ref.at[1], wr_ref, wi_ref, fa_ref, n1, n2, half)

    fc = fc_ref[...]

    def slab(k, carry):
        rows = pl.ds(pl.multiple_of(k * n2, n2), n2)
        g = _twiddled_inner(c2_ref, s2_ref, twr_ref, twi_ref, k)
        xre, xim = _cmm(g, wr_ref[rows, :], wi_ref[rows, :])
        kre = kf_ref[0, 0, rows, :]
        kim = kf_ref[0, 1, rows, :]
        yre = xre * kre - xim * kim
        yim = xre * kim + xim * kre
        are, aim = _cmm(fc, yre, yim)
        wr_ref[rows, :] = are
        wi_ref[rows, :] = aim
        return carry

    lax.fori_loop(0, n1, slab, 0, unroll=FFT_UNROLL)

    bias = bias_ref[0]

    def outer(i, carry):
        tre = twtr_ref[pl.ds(i, 1), :]
        tim = twti_ref[pl.ds(i, 1), :]
        c1 = c1_ref[...]
        s1 = s1_ref[...]
        qs = _kstack(c1 * tre - s1 * tim, -(c1 * tim + s1 * tre), n1)
        yre, yim = _cmm(qs, wr_ref[pl.ds(i, n1, stride=n2), :], wi_ref[pl.ds(i, n1, stride=n2), :])
        rows = pl.ds(i, half, stride=n2)
        u0 = u_ref[0, rows, :]
        u1 = u_ref[1, rows, :]
        o_ref[0, rows, :] = x_ref[0, rows, :] * (yre + u0 * bias)
        o_ref[1, rows, :] = x_ref[1, rows, :] * (yim + u1 * bias)
        return carry

    lax.fori_loop(0, n2, outer, 0, unroll=FFT_UNROLL)


FILTER_CHUNK = 512


def _filter_pos(base, chunk, l):
    m = base + lax.broadcasted_iota(jnp.int32, (chunk, 1), 0)
    is_f = m < l
    pos = jnp.where(is_f, m, 2 * l - m).astype(F32)
    return m, is_f, pos


def _filter_mlp_kernel(l, w1_ref, b1_ref, wm_ref, bm_ref, fr_ref, bands_ref, z_ref):
    chunk = z_ref.shape[0]
    _, _, pos = _filter_pos(pl.program_id(0) * chunk, chunk, l)
    lane = lax.broadcasted_iota(jnp.int32, (1, LANES), 1)
    t = pos / max(l - 1, 1)
    ang = (2.0 * math.pi / l) * pos * bands_ref[...]
    feat = jnp.where(lane == 0, t,
                     jnp.where(lane <= POS_BANDS, jnp.cos(ang),
                               jnp.where(lane <= 2 * POS_BANDS, -jnp.sin(ang), 0.0)))
    z = jnp.sin(fr_ref[0:1, :] * (_dot_x3(feat, w1_ref[...]) + b1_ref[...]))
    for i in range(2):
        z = jnp.sin(fr_ref[i + 1:i + 2, :] * (_dot_x3(z, wm_ref[i]) + bm_ref[i:i + 1, :]))
    z_ref[...] = z


def _filter_kernel(z_ref, wof_ref, wob_ref, dec_ref, fa_ref, c2_ref, s2_ref, twr_ref, twi_ref, o_ref):
    n = o_ref.shape[2]
    l = n // 2
    n2 = FFT_N2
    n1 = n // n2
    chunk = min(FILTER_CHUNK, n)
    nchunk = n // chunk
    wof = _split(wof_ref[...])
    wob = _split(wob_ref[...])
    dec_f = jnp.abs(dec_ref[0, 0:1, :])
    dec_b = jnp.abs(dec_ref[0, 1:2, :])

    def gen(c, asum):
        base = pl.multiple_of(c * chunk, chunk)
        m, is_f, pos = _filter_pos(base, chunk, l)
        t = pos / max(l - 1, 1)
        z = z_ref[pl.ds(base, chunk), :]
        hf = _dot_presplit(z, *wof) * jnp.exp(-t * dec_f)
        hb = _dot_presplit(z, *wob) * jnp.exp(-t * dec_b)
        k2 = jnp.where(is_f, hf, jnp.where(m == l, 0.0, hb))
        o_ref[0, 0, pl.ds(base, chunk), :] = k2
        return asum + jnp.sum(jnp.abs(k2), axis=0, keepdims=True)

    asum = lax.fori_loop(0, nchunk, gen, jnp.zeros((1, LANES), F32))
    inv = 1.0 / asum

    re_ref = o_ref.at[0, 0]
    im_ref = o_ref.at[0, 1]
    _fft_fwd_outer(re_ref, None, re_ref, im_ref, fa_ref, n1, n2, n1, scale=inv)

    def slab(k, carry):
        rows = pl.ds(pl.multiple_of(k * n2, n2), n2)
        g = _twiddled_inner(c2_ref, s2_ref, twr_ref, twi_ref, k)
        xre, xim = _cmm(g, re_ref[rows, :], im_ref[rows, :])
        re_ref[rows, :] = xre
        im_ref[rows, :] = xim
        return carry

    lax.fori_loop(0, n1, slab, 0, unroll=FFT_UNROLL)


def _dft_tables(l):
    n = 2 * l
    n2 = FFT_N2
    n1 = n // n2
    half = n1 // 2

    def cs(rows, cols, period):
        ang = 2.0 * np.pi * ((np.arange(rows)[:, None] * np.arange(cols)[None, :]) % period) / period
        return np.cos(ang), -np.sin(ang)

    def split_np(m):
        m32 = jnp.asarray(m, F32)
        hi = m32.astype(BF16)
        lo = (m32 - hi.astype(F32)).astype(BF16)
        return hi, lo

    def kstack_np(c, s):
        ch, cl = split_np(c)
        sh, sl = split_np(s)
        return jnp.concatenate([jnp.concatenate([ch, ch, cl], axis=1),
                                jnp.concatenate([sh, sh, sl], axis=1)], axis=0)

    def wide(m):
        return jnp.asarray(m if m.shape[1] % LANES == 0 else np.tile(m, (1, 3)), F32)

    c1, s1 = cs(n1, n1, n1)
    c2, s2 = cs(n2, n2, n2)
    tw_c, tw_s = cs(n1, n2, n)
    tabs = {}
    tabs["fa_half"] = kstack_np(c1[:, :half], s1[:, :half])
    tabs["fa_full"] = kstack_np(c1, s1)
    tabs["fc"] = kstack_np(c2, -s2)
    tabs["c2"] = wide(c2)
    tabs["s2"] = wide(s2)
    tabs["twr"] = wide(tw_c)
    tabs["twi"] = wide(tw_s)
    tabs["c1h"] = wide(c1[:half] / n)
    tabs["s1h"] = wide(s1[:half] / n)
    tabs["twtr"] = wide(tw_c.T)
    tabs["twti"] = wide(tw_s.T)
    return tabs


def _const_spec(a, nd_grid):
    shape = a.shape
    if nd_grid == 2:
        return pl.BlockSpec(shape, lambda i, j: (0,) * len(shape))
    return pl.BlockSpec(shape, lambda i: (0,) * len(shape))


def _hyena_filters(l, p, tabs):
    (f_w1, f_b1, f_w_mid, f_b_mid, f_freq, f_w_out, f_decay) = p
    n = 2 * l
    e = WIDTH
    nblk = e // LANES
    w1p = jnp.zeros((LANES, FILTER_HIDDEN), F32).at[:f_w1.shape[0]].set(f_w1)
    bands = np.linspace(1e-4, POS_BANDS - 1, POS_BANDS, dtype=np.float32)
    bl = np.zeros((1, LANES), np.float32)
    bl[0, 1:1 + POS_BANDS] = bands
    bl[0, 1 + POS_BANDS:1 + 2 * POS_BANDS] = bands
    consts = [tabs["fa_full"], tabs["c2"], tabs["s2"], tabs["twr"], tabs["twi"]]
    chunk = min(FILTER_CHUNK, n)
    mlp_in = [w1p, f_b1.reshape(1, FILTER_HIDDEN), f_w_mid, f_b_mid, f_freq, jnp.asarray(bl)]
    z = pl.pallas_call(
        functools.partial(_filter_mlp_kernel, l),
        grid=(n // chunk,),
        in_specs=[_const_spec(a, 1) for a in mlp_in],
        out_specs=pl.BlockSpec((chunk, FILTER_HIDDEN), lambda c: (c, 0)),
        out_shape=jax.ShapeDtypeStruct((n, FILTER_HIDDEN), F32),
        compiler_params=_cparams("parallel"),
        name="hyena_filter_mlp",
    )(*mlp_in)
    return pl.pallas_call(
        _filter_kernel,
        grid=(2, nblk),
        in_specs=[
            pl.BlockSpec((n, FILTER_HIDDEN), lambda o, j: (0, 0)),
            pl.BlockSpec((FILTER_HIDDEN, LANES), lambda o, j: (0, o * 2 * nblk + j)),
            pl.BlockSpec((FILTER_HIDDEN, LANES), lambda o, j: (0, o * 2 * nblk + nblk + j)),
            pl.BlockSpec((1, 2, LANES), lambda o, j: (o, 0, j)),
        ] + [_const_spec(c, 2) for c in consts],
        out_specs=pl.BlockSpec((1, 2, n, LANES), lambda o, j: (o, 0, 0, j)),
        out_shape=jax.ShapeDtypeStruct((2, 2, n, e), F32),
        compiler_params=_cparams("parallel", "arbitrary"),
        name="hyena_filter",
    )(z, f_w_out, f_w_out, f_decay.reshape(2, 2, e), *consts)


def _long_conv(u, xg, kf, order, bias, tabs, col_u, col_x):
    b, l, _ = u.shape
    e = WIDTH
    n = 2 * l
    nblk = e // LANES
    consts = [tabs["fa_half"], tabs["c2"], tabs["s2"], tabs["twr"], tabs["twi"],
              tabs["fc"], tabs["c1h"], tabs["s1h"], tabs["twtr"], tabs["twti"]]
    return pl.pallas_call(
        _lconv_kernel,
        grid=(nblk, b // 2),
        in_specs=[
            pl.BlockSpec((2, l, LANES), lambda j, pi: (pi, 0, col_u * nblk + j)),
            pl.BlockSpec((2, l, LANES), lambda j, pi: (pi, 0, col_x * nblk + j)),
            pl.BlockSpec((1, 2, n, LANES), lambda j, pi: (order, 0, 0, j)),
            pl.BlockSpec((1, 1, LANES), lambda j, pi: (order, 0, j)),
        ] + [_const_spec(c, 2) for c in consts],
        out_specs=pl.BlockSpec((2, l, LANES), lambda j, pi: (pi, 0, j)),
        out_shape=jax.ShapeDtypeStruct((b, l, e), F32),
        scratch_shapes=[pltpu.VMEM((n, LANES), F32), pltpu.VMEM((n, LANES), F32)],
        compiler_params=_cparams("parallel", "arbitrary"),
        name="hyena_long_conv",
    )(u, xg, kf, bias.reshape(2, 1, e), *consts)


def _gated_out_kernel(z_ref, g_ref, x_ref, mod_ref, w_ref, o_ref):
    zz = (z_ref[0] * _silu(g_ref[0])).astype(BF16)
    o_ref[0] = x_ref[0] + mod_ref[0, 2:3, :] * _dot(zz, w_ref[...])


def _gated_out(z, g_arr, g_col, x, mod, w_out, name):
    b, l, d = x.shape
    e = WIDTH
    tl = 256
    tile = lambda col: pl.BlockSpec((1, tl, e), lambda bi, ti: (bi, ti, col))
    return pl.pallas_call(
        _gated_out_kernel,
        grid=(b, l // tl),
        in_specs=[tile(0), tile(g_col), tile(0),
                  pl.BlockSpec((1, 3, d), lambda bi, ti: (bi, 0, 0)),
                  pl.BlockSpec((e, d), lambda bi, ti: (0, 0))],
        out_specs=tile(0),
        out_shape=jax.ShapeDtypeStruct((b, l, d), F32),
        compiler_params=_cparams("parallel", "arbitrary"),
        name=name,
    )(z, g_arr, x, mod, w_out.astype(BF16))


def _hyena_layer(x, mod, ng, p, tabs):
    (w_in, short_w, short_b, f_w1, f_b1, f_w_mid, f_b_mid, f_freq, f_w_out, f_decay, bias, w_out) = p
    l = x.shape[1]
    u3, g = _hy_proj(x, mod, ng, w_in, short_w, short_b)
    kf = _hyena_filters(l, (f_w1, f_b1, f_w_mid, f_b_mid, f_freq, f_w_out, f_decay), tabs)
    z1 = _long_conv(u3, u3, kf, 0, bias, tabs, col_u=2, col_x=0)
    z2 = _long_conv(z1, u3, kf, 1, bias, tabs, col_u=0, col_x=1)
    return _gated_out(z2, g, 0, x, mod, w_out, "hyena_out")


def _cf_proj_kernel(x_ref, mod_ref, ng_ref, w_ref, u_o, sg_o):
    e = WIDTH
    h = _prenorm(x_ref[0], ng_ref[...], mod_ref[0, 0:1, :], mod_ref[0, 1:2, :]).astype(BF16)
    a = _dot(h, w_ref[:, 0:e])
    bgate = _dot(h, w_ref[:, e:2 * e])
    g = _dot(h, w_ref[:, 2 * e:3 * e])
    u_o[0] = a * _sigmoid(bgate)
    sg_o[0] = _silu(g)


def _cf_out_kernel(u_ref, up_ref, un_ref, sg_ref, x_ref, mod_ref, dw_ref, db_ref, lg_ref, lb_ref, w_ref,
                   o_ref, ext_ref):
    t = pl.program_id(1)
    nt = pl.num_programs(1)
    tl = u_ref.shape[1]
    hh = CONV_HALO
    ext_ref[0:hh, :] = jnp.where(t > 0, up_ref[0], 0.0)
    ext_ref[hh:hh + tl, :] = u_ref[0]
    ext_ref[hh + tl:hh + tl + hh, :] = jnp.where(t < nt - 1, un_ref[0], 0.0)
    off = hh - (CONV_WIDTH - 1) // 2
    acc = jnp.zeros((tl, WIDTH), F32) + db_ref[...]
    ext = ext_ref[...]
    nrow = ext.shape[0]
    for phase in range(8):
        win = ext if phase == 0 else pltpu.roll(ext, nrow - phase, 0)
        for k in range(CONV_WIDTH):
            if (off + k) % 8 == phase:
                base = off + k - phase
                acc = acc + win[base:base + tl, :] * dw_ref[k:k + 1, :]
    mean = jnp.mean(acc, axis=-1, keepdims=True)
    cen = acc - mean
    var = jnp.mean(cen * cen, axis=-1, keepdims=True)
    y = cen * lax.rsqrt(var + LN_EPS) * lg_ref[...] + lb_ref[...]
    zz = (_silu(y) * sg_ref[0]).astype(BF16)
    o_ref[0] = x_ref[0] + mod_ref[0, 2:3, :] * _dot(zz, w_ref[...])


def _conformer_layer(x, mod, ng, p):
    (w_in, dw_w, dw_b, ln_g, ln_b, w_out) = p
    b, l, d = x.shape
    e = WIDTH
    tl = 256
    tile = pl.BlockSpec((1, tl, e), lambda bi, ti: (bi, ti, 0))
    modspec = pl.BlockSpec((1, 3, d), lambda bi, ti: (bi, 0, 0))
    vec = pl.BlockSpec((1, e), lambda bi, ti: (0, 0))
    u, sg = pl.pallas_call(
        _cf_proj_kernel,
        grid=(b, l // tl),
        in_specs=[tile, modspec, vec, pl.BlockSpec((d, 3 * e), lambda bi, ti: (0, 0))],
        out_specs=[tile, tile],
        out_shape=[jax.ShapeDtypeStruct((b, l, e), F32)] * 2,
        compiler_params=_cparams("parallel", "arbitrary"),
        name="conformer_proj",
    )(x, mod, ng.reshape(1, d), w_in.astype(BF16))

    hh = CONV_HALO
    hb = tl // hh
    lb = l // hh
    return pl.pallas_call(
        _cf_out_kernel,
        grid=(b, l // tl),
        in_specs=[
            tile,
            pl.BlockSpec((1, hh, e), lambda bi, ti: (bi, jnp.maximum(ti * hb - 1, 0), 0)),
            pl.BlockSpec((1, hh, e), lambda bi, ti: (bi, jnp.minimum((ti + 1) * hb, lb - 1), 0)),
            tile, tile, modspec,
            pl.BlockSpec((CONV_WIDTH, e), lambda bi, ti: (0, 0)),
            vec, vec, vec,
            pl.BlockSpec((e, d), lambda bi, ti: (0, 0)),
        ],
        out_specs=tile,
        out_shape=jax.ShapeDtypeStruct((b, l, d), F32),
        scratch_shapes=[pltpu.VMEM((tl + 2 * hh, e), F32)],
        compiler_params=_cparams("parallel", "arbitrary"),
        name="conformer_out",
    )(u, u, u, sg, x, mod, dw_w, dw_b.reshape(1, e), ln_g.reshape(1, e), ln_b.reshape(1, e),
      w_out.astype(BF16))


def _final_norm_kernel(x_ref, g_ref, o_ref):
    x = x_ref[0]
    ms = jnp.mean(x * x, axis=-1, keepdims=True)
    o_ref[0] = x * lax.rsqrt(ms + RMS_EPS) * g_ref[...]


def _final_norm(x, g):
    b, l, d = x.shape
    tl = 512
    tile = pl.BlockSpec((1, tl, d), lambda bi, ti: (bi, ti, 0))
    return pl.pallas_call(
        _final_norm_kernel,
        grid=(b, l // tl),
        in_specs=[tile, pl.BlockSpec((1, d), lambda bi, ti: (0, 0))],
        out_specs=tile,
        out_shape=jax.ShapeDtypeStruct((b, l, d), F32),
        compiler_params=_cparams("parallel", "arbitrary"),
        name="final_norm",
    )(x, g.reshape(1, d))


def _seg_constants():
    seg = np.arange(SEG_GROUP) // HEAD
    bd = (seg[:, None] == seg[None, :]).astype(np.float32)
    col_head = np.arange(SEG_GROUP // HEAD * SCAN_CHUNK) // SCAN_CHUNK
    bd_sj = (col_head[:, None] == seg[None, :]).astype(np.float32)
    blk = np.arange(SEG_GROUP) // SCAN_CHUNK
    bd_inv = (blk[:, None] == blk[None, :]).astype(np.float32)
    return {"bd": jnp.asarray(bd, BF16), "bd_sj": jnp.asarray(bd_sj, BF16), "bd_inv": jnp.asarray(bd_inv, BF16)}


def _trunk(x, mod, norm_g, rw, hy, cf, final_g, consts):
    tabs = None
    for i in range(DEPTH):
        kind, j = i % 3, i // 3
        m = mod[i]
        last = i == DEPTH - 1
        if kind == 0:
            x = _rwkv_layer(x, m, norm_g[i], [p[j] for p in rw], consts, final_g if last else None)
            if last:
                return x
        elif kind == 1:
            if tabs is None:
                tabs = _dft_tables(x.shape[1])
            x = _hyena_layer(x, m, norm_g[i], [p[j] for p in hy], tabs)
        else:
            x = _conformer_layer(x, m, norm_g[i], [p[j] for p in cf])
    return _final_norm(x, final_g)


def kernel(x_prompt, x_sample, c_prompt, c_sample, norm_g, mod_w, mod_b, rw_w_in, rw_mu, rw_w0, rw_w2, rw_a0, rw_a2, rw_k_k, rw_k_a, rw_r_k, rw_gn_w, rw_gn_b, rw_w_out, hy_w_in, hy_short_w, hy_short_b, hy_f_w1, hy_f_b1, hy_f_w_mid, hy_f_b_mid, hy_f_freq, hy_f_w_out, hy_f_decay, hy_bias, hy_w_out, cf_w_in, cf_dw_w, cf_dw_b, cf_ln_g, cf_ln_b, cf_w_out, final_g):
    rw = (rw_w_in, rw_mu, rw_w0, rw_w2, rw_a0, rw_a2, rw_k_k, rw_k_a, rw_r_k, rw_gn_w, rw_gn_b, rw_w_out)
    hy = (hy_w_in, hy_short_w, hy_short_b, hy_f_w1, hy_f_b1, hy_f_w_mid, hy_f_b_mid, hy_f_freq, hy_f_w_out,
          hy_f_decay, hy_bias, hy_w_out)
    cf = (cf_w_in, cf_dw_w, cf_dw_b, cf_ln_g, cf_ln_b, cf_w_out)
    consts = _seg_constants()
    bp = x_prompt.shape[0]
    mod = _modulation(jnp.concatenate([c_prompt, c_sample], axis=0), mod_w, mod_b)
    y_prompt = _trunk(x_prompt, mod[:, :bp], norm_g, rw, hy, cf, final_g, consts)
    y_sample = _trunk(x_sample, mod[:, bp:], norm_g, rw, hy, cf, final_g, consts)
    return (y_prompt, y_sample)
```

```python
import functools
import math

import numpy as np
import jax
import jax.numpy as jnp
from jax import lax
from jax.experimental import pallas as pl
from jax.experimental.pallas import tpu as pltpu

F32 = jnp.float32
BF16 = jnp.bfloat16

D_MODEL = 1024
WIDTH = 1024
DEPTH = 4
HEADS = 16
HEAD = 64
LORA = 64
RMS_EPS = 1e-6
LN_EPS = 1e-5
GN_EPS = 64e-5
POS_BANDS = 16
FILTER_HIDDEN = 64
CONV_WIDTH = 31
CONV_HALO = 16
LANES = 128
SEG_GROUP = 256
FFT_N2 = 64
FFT_UNROLL = 8
FFT_SLAB_UNROLL = 16
EPILOGUE_ROWS = 512
VMEM_LIMIT = 56 * 1024 * 1024


def _cparams(*sem):
    return pltpu.CompilerParams(dimension_semantics=sem, vmem_limit_bytes=VMEM_LIMIT)


def _dot(a, b):
    return jnp.dot(a, b, preferred_element_type=F32)


def _split(x):
    hi = x.astype(BF16)
    lo = (x - hi.astype(F32)).astype(BF16)
    return hi, lo


def _dot_x3(a, b):
    ah, al = _split(a)
    bh, bl = _split(b)
    return _dot(ah, bh) + _dot(ah, bl) + _dot(al, bh)


def _dot_presplit(a, bh, bl):
    ah, al = _split(a)
    return _dot(ah, bh) + _dot(ah, bl) + _dot(al, bh)


def _dot_kstacked(a, b_stack):
    ah, al = _split(a)
    return _dot(jnp.concatenate([ah, al, ah], axis=1), b_stack)


def _seg_bcast(t, bd):
    m = t.shape[0]
    outs = []
    for g in range(t.shape[1] // SEG_GROUP):
        th, tl = _split(t[:, g * SEG_GROUP:(g + 1) * SEG_GROUP])
        res = _dot(jnp.concatenate([th, tl], axis=0), bd)
        outs.append(res[:m] + res[m:])
    return jnp.concatenate(outs, axis=1)


def _sigmoid(x):
    return 1.0 / (1.0 + jnp.exp(-x))


def _silu(x):
    return x * _sigmoid(x)


def _prenorm(x, ng, shift, scale):
    ms = jnp.mean(x * x, axis=-1, keepdims=True)
    return (x * lax.rsqrt(ms + RMS_EPS) * ng) * (1.0 + scale) + shift


def _mod_kernel(c_ref, w_ref, b_ref, o_ref):
    o_ref[0] = _dot_x3(_silu(c_ref[...]), w_ref[0]) + b_ref[0]


def _modulation(c_all, mod_w, mod_b):
    bc = c_all.shape[0]
    out = pl.pallas_call(
        _mod_kernel,
        grid=(DEPTH, 3),
        in_specs=[
            pl.BlockSpec((bc, D_MODEL), lambda i, j: (0, 0)),
            pl.BlockSpec((1, D_MODEL, D_MODEL), lambda i, j: (i, 0, j)),
            pl.BlockSpec((1, 1, D_MODEL), lambda i, j: (i, 0, j)),
        ],
        out_specs=pl.BlockSpec((1, bc, D_MODEL), lambda i, j: (i, 0, j)),
        out_shape=jax.ShapeDtypeStruct((DEPTH, bc, 3 * D_MODEL), F32),
        compiler_params=_cparams("arbitrary", "arbitrary"),
        name="modulation",
    )(c_all, mod_w, mod_b.reshape(DEPTH, 1, 3 * D_MODEL))
    return out.reshape(DEPTH, bc, 3, D_MODEL)


RW_TL = 256
SCAN_BATCH = 2
SCAN_CHUNK = 32


def _rw_proj_kernel(x_ref, xp_ref, xn_ref, mod_ref, ng_ref, mu_ref, win_ref,
                    w0_ref, w2_ref, a0_ref, a2_ref,
                    kk_ref, ka_ref, rk_ref, bd_ref,
                    r_o, v_o, nkk_o, g_o, bv_o, w0_o, k0_o, b0_o, w1_o, k1_o, b1_o):
    t = pl.program_id(1)
    nt = pl.num_programs(1)
    tl = x_ref.shape[1]
    e = WIDTH
    shift = mod_ref[0, 0:1, :]
    scale = mod_ref[0, 1:2, :]
    ng = ng_ref[...]
    bd = bd_ref[...]

    h = _prenorm(x_ref[0], ng, shift, scale)
    hp = _prenorm(xp_ref[0], ng, shift, scale)[7:8, :]
    hn = _prenorm(xn_ref[0], ng, shift, scale)[0:1, :]
    hp = jnp.where(t > 0, hp, 0.0)
    hn = jnp.where(t < nt - 1, hn, 0.0)
    rows = lax.broadcasted_iota(jnp.int32, (tl, 1), 0)
    h_up = jnp.where(rows == 0, hp, pltpu.roll(h, 1, 0))
    h_dn = jnp.where(rows == tl - 1, hn, pltpu.roll(h, tl - 1, 0))
    xx = 0.5 * (h_up + h_dn) - h

    def proj(i, c0, c1):
        inp = (h + xx * mu_ref[i:i + 1, :]).astype(BF16)
        return _dot(inp, win_ref[:, c0:c1])

    r = proj(0, 0, e)
    k = proj(1, e, 2 * e)
    v = proj(2, 2 * e, 3 * e)
    g = proj(3, 3 * e, 4 * e)
    wl = proj(4, 4 * e, 4 * e + 2 * LORA)
    al = proj(5, 4 * e + 2 * LORA, 4 * e + 4 * LORA)
    twl = jnp.tanh(wl)

    kk0 = k * kk_ref[...]
    nrm = jnp.sqrt(_seg_bcast(kk0 * kk0, bd))
    kk = kk0 / jnp.maximum(nrm, 1e-12)
    ka = ka_ref[...]

    r_o[0] = r
    v_o[0] = v
    g_o[0] = g
    nkk_o[0] = -kk

    ksum = jnp.zeros_like(k)
    for d, (w_o, k_o, b_o) in enumerate(((w0_o, k0_o, b0_o), (w1_o, k1_o, b1_o))):
        w_raw = w0_ref[d:d + 1, :] + _dot_kstacked(twl, w2_ref[d])
        z = -w_raw
        softplus = jnp.maximum(z, 0.0) + jnp.log(1.0 + jnp.exp(-jnp.abs(z)))
        w_log = -softplus - 0.5
        w_o[0] = -jnp.exp(w_log)
        a = _sigmoid(a0_ref[d:d + 1, :] + _dot_kstacked(al, a2_ref[d]))
        k_d = k * (1.0 + (a - 1.0) * ka)
        k_o[0] = k_d
        b_o[0] = kk * a
        ksum = ksum + k_d

    bonus = _seg_bcast(r * ksum * rk_ref[...], bd)
    bv_o[0] = bonus * v


def _dot_nt(a, b):
    return lax.dot_general(a, b, (((1,), (1,)), ((), ())), preferred_element_type=F32)


def _dot_tn(a, b):
    return lax.dot_general(a, b, (((0,), (0,)), ((), ())), preferred_element_type=F32)


def _rw_chunk_kernel(rf, vf, af, wf, kf, bf, rb, vb, ab, wb, kb, bb, bd_ref, bdsj_ref, bdinv_ref,
                     yf_o, yb_o, s_ref):
    tb = pl.program_id(1)
    nb, c, _ = rf.shape
    grp = SEG_GROUP
    hpg = grp // HEAD
    sw = hpg * c
    pair = grp // sw

    @pl.when(tb == 0)
    def _():
        s_ref[...] = jnp.zeros_like(s_ref)

    row = lax.broadcasted_iota(jnp.int32, (c, sw), 0)
    col = lax.broadcasted_iota(jnp.int32, (c, sw), 1) % c
    eye_inv = (lax.broadcasted_iota(jnp.int32, (c, grp), 0)
               == lax.broadcasted_iota(jnp.int32, (c, grp), 1) % c)
    eye_state = (lax.broadcasted_iota(jnp.int32, (HEAD, grp), 0)
                 == lax.broadcasted_iota(jnp.int32, (HEAD, grp), 1) % HEAD)
    tr = lax.broadcasted_iota(jnp.int32, (c, 3 * c), 0)
    tcq = lax.broadcasted_iota(jnp.int32, (c, 3 * c), 1) % c
    bd = bd_ref[...]
    bd_sj = bdsj_ref[...]
    bd_inv = bdinv_ref[...]
    bd_f32 = bd.astype(F32)

    def blockdiag(x, mask):
        reps = mask.shape[0] // x.shape[0]
        xh, xl = _split(x)
        return (jnp.concatenate([xh] * reps, axis=0) * mask, jnp.concatenate([xl] * reps, axis=0) * mask)

    def pmm(lhs, w, dot=_dot, single=None):
        m = lhs.shape[0]
        lh, ll = _split(lhs)
        stack = [lh, ll] if single is None else [lh, ll, single.astype(BF16)]
        res = dot(jnp.concatenate(stack, axis=0), w[0])
        full = res[:m] + res[m:2 * m] + dot(lh, w[1])
        return full if single is None else (full, res[2 * m:])

    def pmm1(lhs, w):
        return _dot(lhs.astype(BF16), w[0])

    def chain_operands(q, rev, refs):
        r, v, a, lw, k, b = [ref[q] for ref in refs]
        tri = (tcq >= tr if rev else tcq <= tr).astype(BF16)
        p1 = lw.astype(BF16)
        rem = lw - p1.astype(F32)
        p2 = rem.astype(BF16)
        p3 = (rem - p2.astype(F32)).astype(BF16)
        cs = _dot(tri, jnp.concatenate([p1, p2, p3], axis=0))
        total = cs[0:1] if rev else cs[c - 1:c]
        e_neg = jnp.exp(-cs)
        e_bar = jnp.exp(total - cs)
        return dict(at=a * jnp.exp(cs - lw), rt=r * jnp.exp(cs), bt=b * e_neg, kt=k * e_neg,
                    bb=b * e_bar, kb=k * e_bar, v=v, gc=jnp.exp(total),
                    strict=(col > row) if rev else (col < row),
                    incl=(col >= row) if rev else (col <= row))

    chains = [(q, rev, refs, y_o)
              for q in range(nb)
              for rev, refs, y_o in ((False, (rf, vf, af, wf, kf, bf), yf_o), (True, (rb, vb, ab, wb, kb, bb), yb_o))]
    ops = [chain_operands(q, rev, refs) for q, rev, refs, _ in chains]
    probs = [(ch, g) for ch in range(len(chains)) for g in range(WIDTH // grp)]
    pick = lambda name: [ops[ch][name][:, g * grp:(g + 1) * grp] for ch, g in probs]
    mask = lambda name: [ops[ch][name] for ch, _ in probs]
    at, rt, bt, kt, bb, kb, vv, gc = (pick(n) for n in ("at", "rt", "bt", "kt", "bb", "kb", "v", "gc"))
    strict, incl = mask("strict"), mask("incl")

    sc_b = [pmm(x, blockdiag(w, bd_sj), _dot_nt, single=y) for x, y, w in zip(at, rt, bt)]
    sc_k = [pmm(x, blockdiag(w, bd_sj), _dot_nt, single=y) for x, y, w in zip(at, rt, kt)]
    mab = [jnp.where(m, s[0], 0.0) for m, s in zip(strict, sc_b)]
    mak = [jnp.where(m, s[0], 0.0) for m, s in zip(strict, sc_k)]
    nrb = [jnp.where(m, s[1], 0.0) for m, s in zip(incl, sc_b)]
    nrk = [jnp.where(m, s[1], 0.0) for m, s in zip(incl, sc_k)]
    wide = [jnp.concatenate(mab[i:i + pair], axis=1) for i in range(0, len(probs), pair)]
    inv = [jnp.where(eye_inv, 1.0, m) for m in wide]
    pw = [pmm(m, blockdiag(m, bd_inv)) for m in wide]
    doublings = int(math.log2(c)) - 1
    for it in range(doublings):
        if it == doublings - 1:
            inv = [i + pmm(i, blockdiag(p, bd_inv)) for i, p in zip(inv, pw)]
        else:
            both = [pmm(jnp.concatenate([p, i], axis=0), blockdiag(p, bd_inv)) for i, p in zip(inv, pw)]
            pw = [x[:c] for x in both]
            inv = [i + x[c:] for i, x in zip(inv, both)]
    inv = [w[:, j * sw:(j + 1) * sw] for w in inv for j in range(pair)]
    mv = [pmm(x, blockdiag(w, bd_sj), single=y) for x, y, w in zip(mak, nrk, vv)]
    dg = [jnp.where(eye_state, x, 0.0) for x in gc]

    xy = [pmm(jnp.concatenate([a_, d_], axis=0), blockdiag(s_ref[ch, g], bd), single=r_)
          for a_, r_, d_, (ch, g) in zip(at, rt, dg, probs)]
    x = [t[0] for t in xy]
    p = [pmm(i, blockdiag(x_[:c] + z[0], bd_sj)) for i, x_, z in zip(inv, x, mv)]
    for t, z, n, p_, (ch, g) in zip(xy, mv, nrb, p, probs):
        q, _, _, y_o = chains[ch]
        y_o[q, :, g * grp:(g + 1) * grp] = t[1] + z[1] + pmm1(n, blockdiag(p_, bd_sj))
    for x_, p_, v_, b_, k_, (ch, g) in zip(x, p, vv, bb, kb, probs):
        bk_h, bk_l = _split(jnp.concatenate([b_, k_], axis=0))
        pv_h, pv_l = _split(jnp.concatenate([p_, v_], axis=0))
        full = _dot_tn(jnp.concatenate([bk_h, bk_h, bk_l], axis=0),
                       jnp.concatenate([pv_h, pv_l, pv_h], axis=0)) * bd_f32
        upd = full[0:HEAD]
        for hh in range(1, hpg):
            upd = upd + full[hh * HEAD:(hh + 1) * HEAD]
        s_ref[ch, g] = x_[c:] + upd


def _rw_out_kernel(final, yf_ref, yb_ref, bv_ref, g_ref, x_ref, mod_ref, gw_ref, gb_ref, wout_ref, bd_ref,
                   fg_ref, o_ref):
    bd = bd_ref[...]
    y = yf_ref[0] + yb_ref[0]
    mean = _seg_bcast(y, bd) * (1.0 / HEAD)
    yc = y - mean
    var = _seg_bcast(yc * yc, bd) * (1.0 / HEAD)
    yn = yc * lax.rsqrt(var + GN_EPS) * gw_ref[...] + gb_ref[...]
    yo = (yn + bv_ref[0]) * _silu(g_ref[0])
    o = _dot(yo.astype(BF16), wout_ref[...])
    res = x_ref[0] + mod_ref[0, 2:3, :] * o
    if final:
        ms = jnp.mean(res * res, axis=-1, keepdims=True)
        res = res * lax.rsqrt(ms + RMS_EPS) * fg_ref[...]
    o_ref[0] = res


def _pad_lora(w):
    z = jnp.zeros_like(w[0])
    return jnp.stack([jnp.concatenate([w[0], z], axis=0), jnp.concatenate([z, w[1]], axis=0)], axis=0)


def _split_host(w):
    hi = w.astype(BF16)
    lo = (w - hi.astype(F32)).astype(BF16)
    return hi, lo


def _rwkv_layer(x, mod, ng, p, consts, final_g=None):
    (w_in, mu, w0, w2, a0, a2, k_k, k_a, r_k, gn_w, gn_b, w_out) = p
    b, l, d = x.shape
    e = WIDTH
    tl = RW_TL
    nt = l // tl
    bd = consts["bd"]
    def lora_stack(w):
        hi, lo = _split_host(_pad_lora(w))
        return jnp.concatenate([hi, hi, lo], axis=1)

    w2s, a2s = lora_stack(w2), lora_stack(a2)
    ncols = w_in.shape[1]

    full = lambda shape: pl.BlockSpec(shape, lambda bi, ti: (0,) * len(shape))
    tile = pl.BlockSpec((1, tl, e), lambda bi, ti: (bi, ti, 0))
    hb = tl // 8
    lb = l // 8
    outs = pl.pallas_call(
        _rw_proj_kernel,
        grid=(b, nt),
        in_specs=[
            tile,
            pl.BlockSpec((1, 8, d), lambda bi, ti: (bi, jnp.maximum(ti * hb - 1, 0), 0)),
            pl.BlockSpec((1, 8, d), lambda bi, ti: (bi, jnp.minimum((ti + 1) * hb, lb - 1), 0)),
            pl.BlockSpec((1, 3, d), lambda bi, ti: (bi, 0, 0)),
            full((1, d)), full((6, d)), full((d, ncols)),
            full((2, e)), full((2, 6 * LORA, e)),
            full((2, e)), full((2, 6 * LORA, e)),
            full((1, e)), full((1, e)), full((1, e)),
            full((SEG_GROUP, SEG_GROUP)),
        ],
        out_specs=[tile] * 11,
        out_shape=[jax.ShapeDtypeStruct((b, l, e), F32)] * 11,
        compiler_params=_cparams("parallel", "arbitrary"),
        name="rwkv_proj",
    )(x, x, x, mod, ng.reshape(1, d), mu, w_in.astype(BF16),
      w0, w2s, a0, a2s,
      k_k.reshape(1, e), k_a.reshape(1, e), r_k.reshape(1, e), bd)
    r, v, nkk, g, bv, wd0, kd0, bd0, wd1, kd1, bd1 = outs

    tc = SCAN_CHUNK
    ntc = l // tc
    nb = SCAN_BATCH
    fwd = pl.BlockSpec((nb, tc, e), lambda bi, ti: (bi, ti, 0))
    bwd = pl.BlockSpec((nb, tc, e), lambda bi, ti: (bi, ntc - 1 - ti, 0))
    yf, yb = pl.pallas_call(
        _rw_chunk_kernel,
        grid=(b // nb, ntc),
        in_specs=[fwd] * 6 + [bwd] * 6 + [full(consts[n].shape) for n in ("bd", "bd_sj", "bd_inv")],
        out_specs=[fwd, bwd],
        out_shape=[jax.ShapeDtypeStruct((b, l, e), F32)] * 2,
        scratch_shapes=[pltpu.VMEM((2 * nb, e // SEG_GROUP, HEAD, SEG_GROUP), F32)],
        compiler_params=_cparams("parallel", "arbitrary"),
        name="rwkv_scan",
    )(r, v, nkk, wd0, kd0, bd0, r, v, nkk, wd1, kd1, bd1, bd, consts["bd_sj"], consts["bd_inv"])

    tlo = 256
    tile_o = pl.BlockSpec((1, tlo, e), lambda bi, ti: (bi, ti, 0))
    final = final_g is not None
    fg = (final_g if final else jnp.ones((d,), F32)).reshape(1, d)
    return pl.pallas_call(
        functools.partial(_rw_out_kernel, final),
        grid=(b, l // tlo),
        in_specs=[tile_o] * 5 + [
            pl.BlockSpec((1, 3, d), lambda bi, ti: (bi, 0, 0)),
            full((1, e)), full((1, e)), full((e, d)), full((SEG_GROUP, SEG_GROUP)), full((1, d)),
        ],
        out_specs=tile_o,
        out_shape=jax.ShapeDtypeStruct((b, l, d), F32),
        compiler_params=_cparams("parallel", "arbitrary"),
        name="rwkv_out",
    )(yf, yb, bv, g, x, mod, gn_w.reshape(1, e), gn_b.reshape(1, e), w_out.astype(BF16), bd, fg)


def _hy_proj_kernel(x_ref, xp_ref, xn_ref, mod_ref, ng_ref, w_ref, sw_ref, sb_ref, u_o, g_o):
    t = pl.program_id(1)
    nt = pl.num_programs(1)
    tl = x_ref.shape[1]
    nu = u_o.shape[2]
    ng, shift, scale = ng_ref[...], mod_ref[0, 0:1, :], mod_ref[0, 1:2, :]
    rows_in = jnp.concatenate([x_ref[0], xp_ref[0], xn_ref[0]], axis=0)
    proj_all = _dot(_prenorm(rows_in, ng, shift, scale).astype(BF16), w_ref[...])
    proj = proj_all[:tl]
    prev = jnp.where(t > 0, proj_all[tl + 7:tl + 8, 0:nu], 0.0)
    nxt = jnp.where(t < nt - 1, proj_all[tl + 8:tl + 9, 0:nu], 0.0)
    u = proj[:, 0:nu]
    rows = lax.broadcasted_iota(jnp.int32, (tl, 1), 0)
    up = jnp.where(rows == 0, prev, pltpu.roll(u, 1, 0))
    dn = jnp.where(rows == tl - 1, nxt, pltpu.roll(u, tl - 1, 0))
    u_o[0] = up * sw_ref[0:1, :] + u * sw_ref[1:2, :] + dn * sw_ref[2:3, :] + sb_ref[...]
    g_o[0] = proj[:, nu:]


def _hy_proj(x, mod, ng, w, short_w, short_b):
    b, l, d = x.shape
    e = WIDTH
    nu = short_w.shape[1]
    tl = 256
    hb, lb = tl // 8, l // 8
    full = lambda shape: pl.BlockSpec(shape, lambda bi, ti: (0,) * len(shape))
    return pl.pallas_call(
        _hy_proj_kernel,
        grid=(b, l // tl),
        in_specs=[
            pl.BlockSpec((1, tl, d), lambda bi, ti: (bi, ti, 0)),
            pl.BlockSpec((1, 8, d), lambda bi, ti: (bi, jnp.maximum(ti * hb - 1, 0), 0)),
            pl.BlockSpec((1, 8, d), lambda bi, ti: (bi, jnp.minimum((ti + 1) * hb, lb - 1), 0)),
            pl.BlockSpec((1, 3, d), lambda bi, ti: (bi, 0, 0)),
            full((1, d)), full((d, nu + e)), full((3, nu)), full((1, nu)),
        ],
        out_specs=[pl.BlockSpec((1, tl, nu), lambda bi, ti: (bi, ti, 0)),
                   pl.BlockSpec((1, tl, e), lambda bi, ti: (bi, ti, 0))],
        out_shape=[jax.ShapeDtypeStruct((b, l, nu), F32), jax.ShapeDtypeStruct((b, l, e), F32)],
        compiler_params=_cparams("parallel", "arbitrary"),
        name="hyena_proj",
    )(x, x, x, mod, ng.reshape(1, d), w.astype(BF16), short_w, short_b.reshape(1, nu))


def _cmm(lhs, xre, xim):
    r = lhs.shape[0] // 2
    w = xre.shape[1]
    x = xre if xim is None else jnp.concatenate([xre, xim], axis=1)
    xh, xl = _split(x)
    out = _dot(lhs, jnp.concatenate([xh, xl, xh], axis=0))
    re_x, im_x = out[:r], out[r:]
    if xim is None:
        return re_x, im_x
    return re_x[:, :w] - im_x[:, w:], im_x[:, :w] + re_x[:, w:]


def _kstack(mre, mim, k):
    m = jnp.concatenate([mre, mim], axis=0)
    hi, lo = _split(m)
    if m.shape[1] == k:
        return jnp.concatenate([hi, hi, lo], axis=1)
    lane = lax.broadcasted_iota(jnp.int32, m.shape, 1)
    return jnp.where(lane < 2 * k, hi, lo)


def _fft_fwd_outer(src_re, src_im, dst_re, dst_im, fa_ref, n1, n2, k1, scale=None):
    fa = fa_ref[...]

    def body(i, carry):
        xre = src_re[pl.ds(i, k1, stride=n2), :]
        xim = None if src_im is None else src_im[pl.ds(i, k1, stride=n2), :]
        if scale is not None:
            xre = xre * scale
        are, aim = _cmm(fa, xre, xim)
        dst_re[pl.ds(i, n1, stride=n2), :] = are
        dst_im[pl.ds(i, n1, stride=n2), :] = aim
        return carry

    lax.fori_loop(0, n2, body, 0, unroll=FFT_UNROLL)


def _twiddled_inner(c2_ref, s2_ref, twr_ref, twi_ref, k):
    tre = twr_ref[pl.ds(k, 1), :]
    tim = twi_ref[pl.ds(k, 1), :]
    c2 = c2_ref[...]
    s2 = s2_ref[...]
    return _kstack(c2 * tre - s2 * tim, c2 * tim + s2 * tre, FFT_N2)


def _lconv_kernel(u_ref, x_ref, kf_ref, bias_ref, fa_ref, c2_ref, s2_ref, twr_ref, twi_ref, fc_ref,
                  c1_ref, s1_ref, twtr_ref, twti_ref, o_ref, wr_ref, wi_ref):
    l = u_ref.shape[1]
    n2 = FFT_N2
    n1 = 2 * l // n2
    half = n1 // 2

    _fft_fwd_outer(u_ref.at[0], u_ref.at[1], wr_ref, wi_ref, fa_ref, n1, n2, half)

    fc = fc_ref[...]

    def slab(k, carry):
        rows = pl.ds(pl.multiple_of(k * n2, n2), n2)
        g = _twiddled_inner(c2_ref, s2_ref, twr_ref, twi_ref, k)
        xre, xim = _cmm(g, wr_ref[rows, :], wi_ref[rows, :])
        kre = kf_ref[0, 0, rows, :]
        kim = kf_ref[0, 1, rows, :]
        yre = xre * kre - xim * kim
        yim = xre * kim + xim * kre
        are, aim = _cmm(fc, yre, yim)
        wr_ref[rows, :] = are
        wi_ref[rows, :] = aim
        return carry

    lax.fori_loop(0, n1, slab, 0, unroll=FFT_SLAB_UNROLL)

    def outer(i, carry):
        tre = twtr_ref[pl.ds(i, 1), :]
        tim = twti_ref[pl.ds(i, 1), :]
        c1 = c1_ref[...]
        s1 = s1_ref[...]
        qs = _kstack(c1 * tre - s1 * tim, -(c1 * tim + s1 * tre), n1)
        yre, yim = _cmm(qs, wr_ref[pl.ds(i, n1, stride=n2), :], wi_ref[pl.ds(i, n1, stride=n2), :])
        rows = pl.ds(i, half, stride=n2)
        o_ref[0, rows, :] = yre
        o_ref[1, rows, :] = yim
        return carry

    lax.fori_loop(0, n2, outer, 0, unroll=FFT_UNROLL)

    bias = bias_ref[0]
    step = min(EPILOGUE_ROWS, l)

    def gate(i, carry):
        rows = pl.ds(pl.multiple_of(i * step, step), step)
        for q in range(2):
            o_ref[q, rows, :] = x_ref[q, rows, :] * (o_ref[q, rows, :] + u_ref[q, rows, :] * bias)
        return carry

    lax.fori_loop(0, l // step, gate, 0)


FILTER_CHUNK = 512


def _filter_pos(base, chunk, l):
    m = base + lax.broadcasted_iota(jnp.int32, (chunk, 1), 0)
    is_f = m < l
    pos = jnp.where(is_f, m, 2 * l - m).astype(F32)
    return m, is_f, pos


def _filter_mlp_kernel(l, w1_ref, b1_ref, wm_ref, bm_ref, fr_ref, bands_ref, z_ref):
    chunk = z_ref.shape[0]
    _, _, pos = _filter_pos(pl.program_id(0) * chunk, chunk, l)
    lane = lax.broadcasted_iota(jnp.int32, (1, LANES), 1)
    t = pos / max(l - 1, 1)
    ang = (2.0 * math.pi / l) * pos * bands_ref[...]
    feat = jnp.where(lane == 0, t,
                     jnp.where(lane <= POS_BANDS, jnp.cos(ang),
                               jnp.where(lane <= 2 * POS_BANDS, -jnp.sin(ang), 0.0)))
    z = jnp.sin(fr_ref[0:1, :] * (_dot_x3(feat, w1_ref[...]) + b1_ref[...]))
    for i in range(2):
        z = jnp.sin(fr_ref[i + 1:i + 2, :] * (_dot_x3(z, wm_ref[i]) + bm_ref[i:i + 1, :]))
    z_ref[...] = z


def _filter_kernel(z_ref, wof_ref, wob_ref, dec_ref, fa_ref, c2_ref, s2_ref, twr_ref, twi_ref, o_ref):
    n = o_ref.shape[2]
    l = n // 2
    n2 = FFT_N2
    n1 = n // n2
    chunk = min(FILTER_CHUNK, n)
    nchunk = n // chunk
    wof = _split(wof_ref[...])
    wob = _split(wob_ref[...])
    dec_f = jnp.abs(dec_ref[0, 0:1, :])
    dec_b = jnp.abs(dec_ref[0, 1:2, :])

    def gen(c, asum):
        base = pl.multiple_of(c * chunk, chunk)
        m, is_f, pos = _filter_pos(base, chunk, l)
        t = pos / max(l - 1, 1)
        z = z_ref[pl.ds(base, chunk), :]
        hf = _dot_presplit(z, *wof) * jnp.exp(-t * dec_f)
        hb = _dot_presplit(z, *wob) * jnp.exp(-t * dec_b)
        k2 = jnp.where(is_f, hf, jnp.where(m == l, 0.0, hb))
        o_ref[0, 0, pl.ds(base, chunk), :] = k2
        return asum + jnp.sum(jnp.abs(k2), axis=0, keepdims=True)

    asum = lax.fori_loop(0, nchunk, gen, jnp.zeros((1, LANES), F32))
    inv = 1.0 / asum

    re_ref = o_ref.at[0, 0]
    im_ref = o_ref.at[0, 1]
    _fft_fwd_outer(re_ref, None, re_ref, im_ref, fa_ref, n1, n2, n1, scale=inv)

    def slab(k, carry):
        rows = pl.ds(pl.multiple_of(k * n2, n2), n2)
        g = _twiddled_inner(c2_ref, s2_ref, twr_ref, twi_ref, k)
        xre, xim = _cmm(g, re_ref[rows, :], im_ref[rows, :])
        re_ref[rows, :] = xre
        im_ref[rows, :] = xim
        return carry

    lax.fori_loop(0, n1, slab, 0, unroll=FFT_SLAB_UNROLL)


def _dft_tables(l):
    n = 2 * l
    n2 = FFT_N2
    n1 = n // n2
    half = n1 // 2

    def cs(rows, cols, period):
        ang = 2.0 * np.pi * ((np.arange(rows)[:, None] * np.arange(cols)[None, :]) % period) / period
        return np.cos(ang), -np.sin(ang)

    def split_np(m):
        m32 = jnp.asarray(m, F32)
        hi = m32.astype(BF16)
        lo = (m32 - hi.astype(F32)).astype(BF16)
        return hi, lo

    def kstack_np(c, s):
        ch, cl = split_np(c)
        sh, sl = split_np(s)
        return jnp.concatenate([jnp.concatenate([ch, ch, cl], axis=1),
                                jnp.concatenate([sh, sh, sl], axis=1)], axis=0)

    def wide(m):
        return jnp.asarray(m if m.shape[1] % LANES == 0 else np.tile(m, (1, 3)), F32)

    c1, s1 = cs(n1, n1, n1)
    c2, s2 = cs(n2, n2, n2)
    tw_c, tw_s = cs(n1, n2, n)
    tabs = {}
    tabs["fa_half"] = kstack_np(c1[:, :half], s1[:, :half])
    tabs["fa_full"] = kstack_np(c1, s1)
    tabs["fc"] = kstack_np(c2, -s2)
    tabs["c2"] = wide(c2)
    tabs["s2"] = wide(s2)
    tabs["twr"] = wide(tw_c)
    tabs["twi"] = wide(tw_s)
    tabs["c1h"] = wide(c1[:half] / n)
    tabs["s1h"] = wide(s1[:half] / n)
    tabs["twtr"] = wide(tw_c.T)
    tabs["twti"] = wide(tw_s.T)
    return tabs


def _const_spec(a, nd_grid):
    shape = a.shape
    if nd_grid == 2:
        return pl.BlockSpec(shape, lambda i, j: (0,) * len(shape))
    return pl.BlockSpec(shape, lambda i: (0,) * len(shape))


def _hyena_filters(l, p, tabs):
    (f_w1, f_b1, f_w_mid, f_b_mid, f_freq, f_w_out, f_decay) = p
    n = 2 * l
    e = WIDTH
    nblk = e // LANES
    w1p = jnp.zeros((LANES, FILTER_HIDDEN), F32).at[:f_w1.shape[0]].set(f_w1)
    bands = np.linspace(1e-4, POS_BANDS - 1, POS_BANDS, dtype=np.float32)
    bl = np.zeros((1, LANES), np.float32)
    bl[0, 1:1 + POS_BANDS] = bands
    bl[0, 1 + POS_BANDS:1 + 2 * POS_BANDS] = bands
    consts = [tabs["fa_full"], tabs["c2"], tabs["s2"], tabs["twr"], tabs["twi"]]
    chunk = min(FILTER_CHUNK, n)
    mlp_in = [w1p, f_b1.reshape(1, FILTER_HIDDEN), f_w_mid, f_b_mid, f_freq, jnp.asarray(bl)]
    z = pl.pallas_call(
        functools.partial(_filter_mlp_kernel, l),
        grid=(n // chunk,),
        in_specs=[_const_spec(a, 1) for a in mlp_in],
        out_specs=pl.BlockSpec((chunk, FILTER_HIDDEN), lambda c: (c, 0)),
        out_shape=jax.ShapeDtypeStruct((n, FILTER_HIDDEN), F32),
        compiler_params=_cparams("parallel"),
        name="hyena_filter_mlp",
    )(*mlp_in)
    return pl.pallas_call(
        _filter_kernel,
        grid=(2, nblk),
        in_specs=[
            pl.BlockSpec((n, FILTER_HIDDEN), lambda o, j: (0, 0)),
            pl.BlockSpec((FILTER_HIDDEN, LANES), lambda o, j: (0, o * 2 * nblk + j)),
            pl.BlockSpec((FILTER_HIDDEN, LANES), lambda o, j: (0, o * 2 * nblk + nblk + j)),
            pl.BlockSpec((1, 2, LANES), lambda o, j: (o, 0, j)),
        ] + [_const_spec(c, 2) for c in consts],
        out_specs=pl.BlockSpec((1, 2, n, LANES), lambda o, j: (o, 0, 0, j)),
        out_shape=jax.ShapeDtypeStruct((2, 2, n, e), F32),
        compiler_params=_cparams("parallel", "arbitrary"),
        name="hyena_filter",
    )(z, f_w_out, f_w_out, f_decay.reshape(2, 2, e), *consts)


def _long_conv(u, xg, kf, order, bias, tabs, col_u, col_x):
    b, l, _ = u.shape
    e = WIDTH
    n = 2 * l
    nblk = e // LANES
    consts = [tabs["fa_half"], tabs["c2"], tabs["s2"], tabs["twr"], tabs["twi"],
              tabs["fc"], tabs["c1h"], tabs["s1h"], tabs["twtr"], tabs["twti"]]
    return pl.pallas_call(
        _lconv_kernel,
        grid=(nblk, b // 2),
        in_specs=[
            pl.BlockSpec((2, l, LANES), lambda j, pi: (pi, 0, col_u * nblk + j)),
            pl.BlockSpec((2, l, LANES), lambda j, pi: (pi, 0, col_x * nblk + j)),
            pl.BlockSpec((1, 2, n, LANES), lambda j, pi: (order, 0, 0, j)),
            pl.BlockSpec((1, 1, LANES), lambda j, pi: (order, 0, j)),
        ] + [_const_spec(c, 2) for c in consts],
        out_specs=pl.BlockSpec((2, l, LANES), lambda j, pi: (pi, 0, j)),
        out_shape=jax.ShapeDtypeStruct((b, l, e), F32),
        scratch_shapes=[pltpu.VMEM((n, LANES), F32), pltpu.VMEM((n, LANES), F32)],
        compiler_params=_cparams("parallel", "arbitrary"),
        name="hyena_long_conv",
    )(u, xg, kf, bias.reshape(2, 1, e), *consts)


def _gated_out_kernel(z_ref, g_ref, x_ref, mod_ref, w_ref, o_ref):
    zz = (z_ref[0] * _silu(g_ref[0])).astype(BF16)
    o_ref[0] = x_ref[0] + mod_ref[0, 2:3, :] * _dot(zz, w_ref[...])


def _gated_out(z, g_arr, g_col, x, mod, w_out, name):
    b, l, d = x.shape
    e = WIDTH
    tl = 256
    tile = lambda col: pl.BlockSpec((1, tl, e), lambda bi, ti: (bi, ti, col))
    return pl.pallas_call(
        _gated_out_kernel,
        grid=(b, l // tl),
        in_specs=[tile(0), tile(g_col), tile(0),
                  pl.BlockSpec((1, 3, d), lambda bi, ti: (bi, 0, 0)),
                  pl.BlockSpec((e, d), lambda bi, ti: (0, 0))],
        out_specs=tile(0),
        out_shape=jax.ShapeDtypeStruct((b, l, d), F32),
        compiler_params=_cparams("parallel", "arbitrary"),
        name=name,
    )(z, g_arr, x, mod, w_out.astype(BF16))


def _hyena_layer(x, mod, ng, p, tabs):
    (w_in, short_w, short_b, f_w1, f_b1, f_w_mid, f_b_mid, f_freq, f_w_out, f_decay, bias, w_out) = p
    l = x.shape[1]
    u3, g = _hy_proj(x, mod, ng, w_in, short_w, short_b)
    kf = _hyena_filters(l, (f_w1, f_b1, f_w_mid, f_b_mid, f_freq, f_w_out, f_decay), tabs)
    z1 = _long_conv(u3, u3, kf, 0, bias, tabs, col_u=2, col_x=0)
    z2 = _long_conv(z1, u3, kf, 1, bias, tabs, col_u=0, col_x=1)
    return _gated_out(z2, g, 0, x, mod, w_out, "hyena_out")


def _cf_proj_kernel(x_ref, mod_ref, ng_ref, w_ref, u_o, sg_o):
    e = WIDTH
    h = _prenorm(x_ref[0], ng_ref[...], mod_ref[0, 0:1, :], mod_ref[0, 1:2, :]).astype(BF16)
    a = _dot(h, w_ref[:, 0:e])
    bgate = _dot(h, w_ref[:, e:2 * e])
    g = _dot(h, w_ref[:, 2 * e:3 * e])
    u_o[0] = a * _sigmoid(bgate)
    sg_o[0] = _silu(g)


def _cf_out_kernel(u_ref, up_ref, un_ref, sg_ref, x_ref, mod_ref, dw_ref, db_ref, lg_ref, lb_ref, w_ref,
                   o_ref, ext_ref):
    t = pl.program_id(1)
    nt = pl.num_programs(1)
    tl = u_ref.shape[1]
    hh = CONV_HALO
    ext_ref[0:hh, :] = jnp.where(t > 0, up_ref[0], 0.0)
    ext_ref[hh:hh + tl, :] = u_ref[0]
    ext_ref[hh + tl:hh + tl + hh, :] = jnp.where(t < nt - 1, un_ref[0], 0.0)
    off = hh - (CONV_WIDTH - 1) // 2
    acc = jnp.zeros((tl, WIDTH), F32) + db_ref[...]
    ext = ext_ref[...]
    nrow = ext.shape[0]
    for phase in range(8):
        win = ext if phase == 0 else pltpu.roll(ext, nrow - phase, 0)
        for k in range(CONV_WIDTH):
            if (off + k) % 8 == phase:
                base = off + k - phase
                acc = acc + win[base:base + tl, :] * dw_ref[k:k + 1, :]
    mean = jnp.mean(acc, axis=-1, keepdims=True)
    cen = acc - mean
    var = jnp.mean(cen * cen, axis=-1, keepdims=True)
    y = cen * lax.rsqrt(var + LN_EPS) * lg_ref[...] + lb_ref[...]
    zz = (_silu(y) * sg_ref[0]).astype(BF16)
    o_ref[0] = x_ref[0] + mod_ref[0, 2:3, :] * _dot(zz, w_ref[...])


def _conformer_layer(x, mod, ng, p):
    (w_in, dw_w, dw_b, ln_g, ln_b, w_out) = p
    b, l, d = x.shape
    e = WIDTH
    tl = 256
    tile = pl.BlockSpec((1, tl, e), lambda bi, ti: (bi, ti, 0))
    modspec = pl.BlockSpec((1, 3, d), lambda bi, ti: (bi, 0, 0))
    vec = pl.BlockSpec((1, e), lambda bi, ti: (0, 0))
    u, sg = pl.pallas_call(
        _cf_proj_kernel,
        grid=(b, l // tl),
        in_specs=[tile, modspec, vec, pl.BlockSpec((d, 3 * e), lambda bi, ti: (0, 0))],
        out_specs=[tile, tile],
        out_shape=[jax.ShapeDtypeStruct((b, l, e), F32)] * 2,
        compiler_params=_cparams("parallel", "arbitrary"),
        name="conformer_proj",
    )(x, mod, ng.reshape(1, d), w_in.astype(BF16))

    hh = CONV_HALO
    hb = tl // hh
    lb = l // hh
    return pl.pallas_call(
        _cf_out_kernel,
        grid=(b, l // tl),
        in_specs=[
            tile,
            pl.BlockSpec((1, hh, e), lambda bi, ti: (bi, jnp.maximum(ti * hb - 1, 0), 0)),
            pl.BlockSpec((1, hh, e), lambda bi, ti: (bi, jnp.minimum((ti + 1) * hb, lb - 1), 0)),
            tile, tile, modspec,
            pl.BlockSpec((CONV_WIDTH, e), lambda bi, ti: (0, 0)),
            vec, vec, vec,
            pl.BlockSpec((e, d), lambda bi, ti: (0, 0)),
        ],
        out_specs=tile,
        out_shape=jax.ShapeDtypeStruct((b, l, d), F32),
        scratch_shapes=[pltpu.VMEM((tl + 2 * hh, e), F32)],
        compiler_params=_cparams("parallel", "arbitrary"),
        name="conformer_out",
    )(u, u, u, sg, x, mod, dw_w, dw_b.reshape(1, e), ln_g.reshape(1, e), ln_b.reshape(1, e),
      w_out.astype(BF16))


def _final_norm_kernel(x_ref, g_ref, o_ref):
    x = x_ref[0]
    ms = jnp.mean(x * x, axis=-1, keepdims=True)
    o_ref[0] = x * lax.rsqrt(ms + RMS_EPS) * g_ref[...]


def _final_norm(x, g):
    b, l, d = x.shape
    tl = 512
    tile = pl.BlockSpec((1, tl, d), lambda bi, ti: (bi, ti, 0))
    return pl.pallas_call(
        _final_norm_kernel,
        grid=(b, l // tl),
        in_specs=[tile, pl.BlockSpec((1, d), lambda bi, ti: (0, 0))],
        out_specs=tile,
        out_shape=jax.ShapeDtypeStruct((b, l, d), F32),
        compiler_params=_cparams("parallel", "arbitrary"),
        name="final_norm",
    )(x, g.reshape(1, d))


def _seg_constants():
    seg = np.arange(SEG_GROUP) // HEAD
    bd = (seg[:, None] == seg[None, :]).astype(np.float32)
    col_head = np.arange(SEG_GROUP // HEAD * SCAN_CHUNK) // SCAN_CHUNK
    bd_sj = (col_head[:, None] == seg[None, :]).astype(np.float32)
    blk = np.arange(SEG_GROUP) // SCAN_CHUNK
    bd_inv = (blk[:, None] == blk[None, :]).astype(np.float32)
    return {"bd": jnp.asarray(bd, BF16), "bd_sj": jnp.asarray(bd_sj, BF16), "bd_inv": jnp.asarray(bd_inv, BF16)}


def _trunk(x, mod, norm_g, rw, hy, cf, final_g, consts):
    tabs = None
    for i in range(DEPTH):
        kind, j = i % 3, i // 3
        m = mod[i]
        last = i == DEPTH - 1
        if kind == 0:
            x = _rwkv_layer(x, m, norm_g[i], [p[j] for p in rw], consts, final_g if last else None)
            if last:
                return x
        elif kind == 1:
            if tabs is None:
                tabs = _dft_tables(x.shape[1])
            x = _hyena_layer(x, m, norm_g[i], [p[j] for p in hy], tabs)
        else:
            x = _conformer_layer(x, m, norm_g[i], [p[j] for p in cf])
    return _final_norm(x, final_g)


def kernel(x_prompt, x_sample, c_prompt, c_sample, norm_g, mod_w, mod_b, rw_w_in, rw_mu, rw_w0, rw_w2, rw_a0, rw_a2, rw_k_k, rw_k_a, rw_r_k, rw_gn_w, rw_gn_b, rw_w_out, hy_w_in, hy_short_w, hy_short_b, hy_f_w1, hy_f_b1, hy_f_w_mid, hy_f_b_mid, hy_f_freq, hy_f_w_out, hy_f_decay, hy_bias, hy_w_out, cf_w_in, cf_dw_w, cf_dw_b, cf_ln_g, cf_ln_b, cf_w_out, final_g):
    rw = (rw_w_in, rw_mu, rw_w0, rw_w2, rw_a0, rw_a2, rw_k_k, rw_k_a, rw_r_k, rw_gn_w, rw_gn_b, rw_w_out)
    hy = (hy_w_in, hy_short_w, hy_short_b, hy_f_w1, hy_f_b1, hy_f_w_mid, hy_f_b_mid, hy_f_freq, hy_f_w_out,
          hy_f_decay, hy_bias, hy_w_out)
    cf = (cf_w_in, cf_dw_w, cf_dw_b, cf_ln_g, cf_ln_b, cf_w_out)
    consts = _seg_constants()
    bp = x_prompt.shape[0]
    mod = _modulation(jnp.concatenate([c_prompt, c_sample], axis=0), mod_w, mod_b)
    y_prompt = _trunk(x_prompt, mod[:, :bp], norm_g, rw, hy, cf, final_g, consts)
    y_sample = _trunk(x_sample, mod[:, bp:], norm_g, rw, hy, cf, final_g, consts)
    return (y_prompt, y_sample)
```

```python
import functools
import math

import numpy as np
import jax
import jax.numpy as jnp
from jax import lax
from jax.experimental import pallas as pl
from jax.experimental.pallas import tpu as pltpu

F32 = jnp.float32
BF16 = jnp.bfloat16

D_MODEL = 1024
WIDTH = 1024
DEPTH = 4
HEADS = 16
HEAD = 64
LORA = 64
RMS_EPS = 1e-6
LN_EPS = 1e-5
GN_EPS = 64e-5
POS_BANDS = 16
FILTER_HIDDEN = 64
CONV_WIDTH = 31
CONV_HALO = 16
LANES = 128
SEG_GROUP = 256
FFT_N2 = 64
FFT_UNROLL = 8
FFT_SLAB_UNROLL = 16
EPILOGUE_ROWS = 512
VMEM_LIMIT = 56 * 1024 * 1024


def _cparams(*sem):
    return pltpu.CompilerParams(dimension_semantics=sem, vmem_limit_bytes=VMEM_LIMIT)


def _dot(a, b):
    return jnp.dot(a, b, preferred_element_type=F32)


def _split(x):
    hi = x.astype(BF16)
    lo = (x - hi.astype(F32)).astype(BF16)
    return hi, lo


def _dot_x3(a, b):
    ah, al = _split(a)
    bh, bl = _split(b)
    return _dot(ah, bh) + _dot(ah, bl) + _dot(al, bh)


def _dot_presplit(a, bh, bl):
    ah, al = _split(a)
    return _dot(ah, bh) + _dot(ah, bl) + _dot(al, bh)


def _dot_kstacked(a, b_stack):
    ah, al = _split(a)
    return _dot(jnp.concatenate([ah, al, ah], axis=1), b_stack)


def _seg_bcast(t, bd, single_pass=False):
    m = t.shape[0]
    outs = []
    for g in range(t.shape[1] // SEG_GROUP):
        tg = t[:, g * SEG_GROUP:(g + 1) * SEG_GROUP]
        if single_pass:
            outs.append(_dot(tg.astype(BF16), bd))
        else:
            th, tl = _split(tg)
            res = _dot(jnp.concatenate([th, tl], axis=0), bd)
            outs.append(res[:m] + res[m:])
    return jnp.concatenate(outs, axis=1)


def _sigmoid(x):
    return 1.0 / (1.0 + jnp.exp(-x))


def _silu(x):
    return x * _sigmoid(x)


def _prenorm(x, ng, shift, scale):
    ms = jnp.mean(x * x, axis=-1, keepdims=True)
    return (x * lax.rsqrt(ms + RMS_EPS) * ng) * (1.0 + scale) + shift


def _mod_kernel(c_ref, w_ref, b_ref, o_ref):
    o_ref[0] = _dot_x3(_silu(c_ref[...]), w_ref[0]) + b_ref[0]


def _modulation(c_all, mod_w, mod_b):
    bc = c_all.shape[0]
    out = pl.pallas_call(
        _mod_kernel,
        grid=(DEPTH, 3),
        in_specs=[
            pl.BlockSpec((bc, D_MODEL), lambda i, j: (0, 0)),
            pl.BlockSpec((1, D_MODEL, D_MODEL), lambda i, j: (i, 0, j)),
            pl.BlockSpec((1, 1, D_MODEL), lambda i, j: (i, 0, j)),
        ],
        out_specs=pl.BlockSpec((1, bc, D_MODEL), lambda i, j: (i, 0, j)),
        out_shape=jax.ShapeDtypeStruct((DEPTH, bc, 3 * D_MODEL), F32),
        compiler_params=_cparams("arbitrary", "arbitrary"),
        name="modulation",
    )(c_all, mod_w, mod_b.reshape(DEPTH, 1, 3 * D_MODEL))
    return out.reshape(DEPTH, bc, 3, D_MODEL)


RW_TL = 256
SCAN_BATCH = 2
SCAN_CHUNK = 32


def _rw_proj_kernel(x_ref, xp_ref, xn_ref, mod_ref, ng_ref, mu_ref, win_ref,
                    w0_ref, w2_ref, a0_ref, a2_ref,
                    kk_ref, ka_ref, rk_ref, bd_ref,
                    r_o, v_o, nkk_o, g_o, bv_o, w0_o, k0_o, b0_o, w1_o, k1_o, b1_o):
    t = pl.program_id(1)
    nt = pl.num_programs(1)
    tl = x_ref.shape[1]
    e = WIDTH
    shift = mod_ref[0, 0:1, :]
    scale = mod_ref[0, 1:2, :]
    ng = ng_ref[...]
    bd = bd_ref[...]

    h = _prenorm(x_ref[0], ng, shift, scale)
    hp = _prenorm(xp_ref[0], ng, shift, scale)[7:8, :]
    hn = _prenorm(xn_ref[0], ng, shift, scale)[0:1, :]
    hp = jnp.where(t > 0, hp, 0.0)
    hn = jnp.where(t < nt - 1, hn, 0.0)
    rows = lax.broadcasted_iota(jnp.int32, (tl, 1), 0)
    h_up = jnp.where(rows == 0, hp, pltpu.roll(h, 1, 0))
    h_dn = jnp.where(rows == tl - 1, hn, pltpu.roll(h, tl - 1, 0))
    xx = 0.5 * (h_up + h_dn) - h

    def proj(i, c0, c1):
        inp = (h + xx * mu_ref[i:i + 1, :]).astype(BF16)
        return _dot(inp, win_ref[:, c0:c1])

    r = proj(0, 0, e)
    k = proj(1, e, 2 * e)
    v = proj(2, 2 * e, 3 * e)
    g = proj(3, 3 * e, 4 * e)
    wl = proj(4, 4 * e, 4 * e + 2 * LORA)
    al = proj(5, 4 * e + 2 * LORA, 4 * e + 4 * LORA)
    twl = jnp.tanh(wl)

    kk0 = k * kk_ref[...]
    nrm = jnp.sqrt(_seg_bcast(kk0 * kk0, bd))
    kk = kk0 / jnp.maximum(nrm, 1e-12)
    ka = ka_ref[...]

    r_o[0] = r
    v_o[0] = v
    g_o[0] = g
    nkk_o[0] = -kk

    ksum = jnp.zeros_like(k)
    for d, (w_o, k_o, b_o) in enumerate(((w0_o, k0_o, b0_o), (w1_o, k1_o, b1_o))):
        w_raw = w0_ref[d:d + 1, :] + _dot_kstacked(twl, w2_ref[d])
        w_o[0] = -math.exp(-0.5) * _sigmoid(w_raw)
        a = _sigmoid(a0_ref[d:d + 1, :] + _dot_kstacked(al, a2_ref[d]))
        k_d = k * (1.0 + (a - 1.0) * ka)
        k_o[0] = k_d
        b_o[0] = kk * a
        ksum = ksum + k_d

    bonus = _seg_bcast(r * ksum * rk_ref[...], bd, single_pass=True)
    bv_o[0] = bonus * v


def _dot_nt(a, b):
    return lax.dot_general(a, b, (((1,), (1,)), ((), ())), preferred_element_type=F32)


def _dot_tn(a, b):
    return lax.dot_general(a, b, (((0,), (0,)), ((), ())), preferred_element_type=F32)


def _rw_chunk_kernel(rf, vf, af, wf, kf, bf, rb, vb, ab, wb, kb, bb, bd_ref, bdsj_ref, bdinv_ref,
                     yf_o, yb_o, s_ref):
    tb = pl.program_id(1)
    nb, c, _ = rf.shape
    grp = SEG_GROUP
    hpg = grp // HEAD
    sw = hpg * c
    pair = grp // sw

    @pl.when(tb == 0)
    def _():
        s_ref[...] = jnp.zeros_like(s_ref)

    row = lax.broadcasted_iota(jnp.int32, (c, sw), 0)
    col = lax.broadcasted_iota(jnp.int32, (c, sw), 1) % c
    eye_inv = (lax.broadcasted_iota(jnp.int32, (c, grp), 0)
               == lax.broadcasted_iota(jnp.int32, (c, grp), 1) % c)
    eye_state = (lax.broadcasted_iota(jnp.int32, (HEAD, grp), 0)
                 == lax.broadcasted_iota(jnp.int32, (HEAD, grp), 1) % HEAD)
    tr = lax.broadcasted_iota(jnp.int32, (c, 3 * c), 0)
    tcq = lax.broadcasted_iota(jnp.int32, (c, 3 * c), 1) % c
    bd = bd_ref[...]
    bd_sj = bdsj_ref[...]
    bd_inv = bdinv_ref[...]
    bd_f32 = bd.astype(F32)

    def blockdiag(x, mask):
        reps = mask.shape[0] // x.shape[0]
        xh, xl = _split(x)
        return (jnp.concatenate([xh] * reps, axis=0) * mask, jnp.concatenate([xl] * reps, axis=0) * mask)

    def pmm(lhs, w, dot=_dot, single=None):
        m = lhs.shape[0]
        lh, ll = _split(lhs)
        stack = [lh, ll] if single is None else [lh, ll, single.astype(BF16)]
        res = dot(jnp.concatenate(stack, axis=0), w[0])
        full = res[:m] + res[m:2 * m] + dot(lh, w[1])
        return full if single is None else (full, res[2 * m:])

    def pmm1(lhs, w):
        return _dot(lhs.astype(BF16), w[0])

    def chain_operands(q, rev, refs):
        r, v, a, lw, k, b = [ref[q] for ref in refs]
        tri = (tcq >= tr if rev else tcq <= tr).astype(BF16)
        p1 = lw.astype(BF16)
        rem = lw - p1.astype(F32)
        p2 = rem.astype(BF16)
        p3 = (rem - p2.astype(F32)).astype(BF16)
        cs = _dot(tri, jnp.concatenate([p1, p2, p3], axis=0))
        total = cs[0:1] if rev else cs[c - 1:c]
        e_neg = jnp.exp(-cs)
        e_bar = jnp.exp(total - cs)
        return dict(at=a * jnp.exp(cs - lw), rt=r * jnp.exp(cs), bt=b * e_neg, kt=k * e_neg,
                    bb=b * e_bar, kb=k * e_bar, v=v, gc=jnp.exp(total),
                    strict=(col > row) if rev else (col < row),
                    incl=(col >= row) if rev else (col <= row))

    chains = [(q, rev, refs, y_o)
              for q in range(nb)
              for rev, refs, y_o in ((False, (rf, vf, af, wf, kf, bf), yf_o), (True, (rb, vb, ab, wb, kb, bb), yb_o))]
    ops = [chain_operands(q, rev, refs) for q, rev, refs, _ in chains]
    probs = [(ch, g) for ch in range(len(chains)) for g in range(WIDTH // grp)]
    pick = lambda name: [ops[ch][name][:, g * grp:(g + 1) * grp] for ch, g in probs]
    mask = lambda name: [ops[ch][name] for ch, _ in probs]
    at, rt, bt, kt, bb, kb, vv, gc = (pick(n) for n in ("at", "rt", "bt", "kt", "bb", "kb", "v", "gc"))
    strict, incl = mask("strict"), mask("incl")

    sc_b = [pmm(x, blockdiag(w, bd_sj), _dot_nt, single=y) for x, y, w in zip(at, rt, bt)]
    sc_k = [pmm(x, blockdiag(w, bd_sj), _dot_nt, single=y) for x, y, w in zip(at, rt, kt)]
    mab = [jnp.where(m, s[0], 0.0) for m, s in zip(strict, sc_b)]
    mak = [jnp.where(m, s[0], 0.0) for m, s in zip(strict, sc_k)]
    nrb = [jnp.where(m, s[1], 0.0) for m, s in zip(incl, sc_b)]
    nrk = [jnp.where(m, s[1], 0.0) for m, s in zip(incl, sc_k)]
    wide = [jnp.concatenate(mab[i:i + pair], axis=1) for i in range(0, len(probs), pair)]
    inv = [jnp.where(eye_inv, 1.0, m) for m in wide]
    pw = [pmm(m, blockdiag(m, bd_inv)) for m in wide]
    doublings = int(math.log2(c)) - 1
    for it in range(doublings):
        if it == doublings - 1:
            inv = [i + pmm(i, blockdiag(p, bd_inv)) for i, p in zip(inv, pw)]
        else:
            both = [pmm(jnp.concatenate([p, i], axis=0), blockdiag(p, bd_inv)) for i, p in zip(inv, pw)]
            pw = [x[:c] for x in both]
            inv = [i + x[c:] for i, x in zip(inv, both)]
    inv = [w[:, j * sw:(j + 1) * sw] for w in inv for j in range(pair)]
    mv = [pmm(x, blockdiag(w, bd_sj), single=y) for x, y, w in zip(mak, nrk, vv)]
    dg = [jnp.where(eye_state, x, 0.0) for x in gc]

    xy = [pmm(jnp.concatenate([a_, d_], axis=0), blockdiag(s_ref[ch, g], bd), single=r_)
          for a_, r_, d_, (ch, g) in zip(at, rt, dg, probs)]
    x = [t[0] for t in xy]
    p = [pmm(i, blockdiag(x_[:c] + z[0], bd_sj)) for i, x_, z in zip(inv, x, mv)]
    for t, z, n, p_, (ch, g) in zip(xy, mv, nrb, p, probs):
        q, _, _, y_o = chains[ch]
        y_o[q, :, g * grp:(g + 1) * grp] = t[1] + z[1] + pmm1(n, blockdiag(p_, bd_sj))
    for x_, p_, v_, b_, k_, (ch, g) in zip(x, p, vv, bb, kb, probs):
        bk_h, bk_l = _split(jnp.concatenate([b_, k_], axis=0))
        pv_h, pv_l = _split(jnp.concatenate([p_, v_], axis=0))
        full = _dot_tn(jnp.concatenate([bk_h, bk_h, bk_l], axis=0),
                       jnp.concatenate([pv_h, pv_l, pv_h], axis=0)) * bd_f32
        upd = full[0:HEAD]
        for hh in range(1, hpg):
            upd = upd + full[hh * HEAD:(hh + 1) * HEAD]
        s_ref[ch, g] = x_[c:] + upd


def _rw_out_kernel(final, yf_ref, yb_ref, bv_ref, g_ref, x_ref, mod_ref, gw_ref, gb_ref, wout_ref, bd_ref,
                   fg_ref, o_ref):
    bd = bd_ref[...]
    y = yf_ref[0] + yb_ref[0]
    mean = _seg_bcast(y, bd) * (1.0 / HEAD)
    yc = y - mean
    var = _seg_bcast(yc * yc, bd) * (1.0 / HEAD)
    yn = yc * lax.rsqrt(var + GN_EPS) * gw_ref[...] + gb_ref[...]
    yo = (yn + bv_ref[0]) * _silu(g_ref[0])
    o = _dot(yo.astype(BF16), wout_ref[...])
    res = x_ref[0] + mod_ref[0, 2:3, :] * o
    if final:
        ms = jnp.mean(res * res, axis=-1, keepdims=True)
        res = res * lax.rsqrt(ms + RMS_EPS) * fg_ref[...]
    o_ref[0] = res


def _pad_lora(w):
    z = jnp.zeros_like(w[0])
    return jnp.stack([jnp.concatenate([w[0], z], axis=0), jnp.concatenate([z, w[1]], axis=0)], axis=0)


def _split_host(w):
    hi = w.astype(BF16)
    lo = (w - hi.astype(F32)).astype(BF16)
    return hi, lo


def _rwkv_layer(x, mod, ng, p, consts, final_g=None):
    (w_in, mu, w0, w2, a0, a2, k_k, k_a, r_k, gn_w, gn_b, w_out) = p
    b, l, d = x.shape
    e = WIDTH
    tl = RW_TL
    nt = l // tl
    bd = consts["bd"]
    def lora_stack(w):
        hi, lo = _split_host(_pad_lora(w))
        return jnp.concatenate([hi, hi, lo], axis=1)

    w2s, a2s = lora_stack(w2), lora_stack(a2)
    ncols = w_in.shape[1]

    full = lambda shape: pl.BlockSpec(shape, lambda bi, ti: (0,) * len(shape))
    tile = pl.BlockSpec((1, tl, e), lambda bi, ti: (bi, ti, 0))
    hb = tl // 8
    lb = l // 8
    outs = pl.pallas_call(
        _rw_proj_kernel,
        grid=(b, nt),
        in_specs=[
            tile,
            pl.BlockSpec((1, 8, d), lambda bi, ti: (bi, jnp.maximum(ti * hb - 1, 0), 0)),
            pl.BlockSpec((1, 8, d), lambda bi, ti: (bi, jnp.minimum((ti + 1) * hb, lb - 1), 0)),
            pl.BlockSpec((1, 3, d), lambda bi, ti: (bi, 0, 0)),
            full((1, d)), full((6, d)), full((d, ncols)),
            full((2, e)), full((2, 6 * LORA, e)),
            full((2, e)), full((2, 6 * LORA, e)),
            full((1, e)), full((1, e)), full((1, e)),
            full((SEG_GROUP, SEG_GROUP)),
        ],
        out_specs=[tile] * 11,
        out_shape=[jax.ShapeDtypeStruct((b, l, e), F32)] * 11,
        compiler_params=_cparams("parallel", "arbitrary"),
        name="rwkv_proj",
    )(x, x, x, mod, ng.reshape(1, d), mu, w_in.astype(BF16),
      w0, w2s, a0, a2s,
      k_k.reshape(1, e), k_a.reshape(1, e), r_k.reshape(1, e), bd)
    r, v, nkk, g, bv, wd0, kd0, bd0, wd1, kd1, bd1 = outs

    tc = SCAN_CHUNK
    ntc = l // tc
    nb = SCAN_BATCH
    fwd = pl.BlockSpec((nb, tc, e), lambda bi, ti: (bi, ti, 0))
    bwd = pl.BlockSpec((nb, tc, e), lambda bi, ti: (bi, ntc - 1 - ti, 0))
    yf, yb = pl.pallas_call(
        _rw_chunk_kernel,
        grid=(b // nb, ntc),
        in_specs=[fwd] * 6 + [bwd] * 6 + [full(consts[n].shape) for n in ("bd", "bd_sj", "bd_inv")],
        out_specs=[fwd, bwd],
        out_shape=[jax.ShapeDtypeStruct((b, l, e), F32)] * 2,
        scratch_shapes=[pltpu.VMEM((2 * nb, e // SEG_GROUP, HEAD, SEG_GROUP), F32)],
        compiler_params=_cparams("parallel", "arbitrary"),
        name="rwkv_scan",
    )(r, v, nkk, wd0, kd0, bd0, r, v, nkk, wd1, kd1, bd1, bd, consts["bd_sj"], consts["bd_inv"])

    tlo = 256
    tile_o = pl.BlockSpec((1, tlo, e), lambda bi, ti: (bi, ti, 0))
    final = final_g is not None
    fg = (final_g if final else jnp.ones((d,), F32)).reshape(1, d)
    return pl.pallas_call(
        functools.partial(_rw_out_kernel, final),
        grid=(b, l // tlo),
        in_specs=[tile_o] * 5 + [
            pl.BlockSpec((1, 3, d), lambda bi, ti: (bi, 0, 0)),
            full((1, e)), full((1, e)), full((e, d)), full((SEG_GROUP, SEG_GROUP)), full((1, d)),
        ],
        out_specs=tile_o,
        out_shape=jax.ShapeDtypeStruct((b, l, d), F32),
        compiler_params=_cparams("parallel", "arbitrary"),
        name="rwkv_out",
    )(yf, yb, bv, g, x, mod, gn_w.reshape(1, e), gn_b.reshape(1, e), w_out.astype(BF16), bd, fg)


def _hy_proj_kernel(x_ref, xp_ref, xn_ref, mod_ref, ng_ref, w_ref, sw_ref, sb_ref, u_o, g_o):
    t = pl.program_id(1)
    nt = pl.num_programs(1)
    tl = x_ref.shape[1]
    nu = u_o.shape[2]
    ng, shift, scale = ng_ref[...], mod_ref[0, 0:1, :], mod_ref[0, 1:2, :]
    rows_in = jnp.concatenate([x_ref[0], xp_ref[0], xn_ref[0]], axis=0)
    proj_all = _dot(_prenorm(rows_in, ng, shift, scale).astype(BF16), w_ref[...])
    proj = proj_all[:tl]
    prev = jnp.where(t > 0, proj_all[tl + 7:tl + 8, 0:nu], 0.0)
    nxt = jnp.where(t < nt - 1, proj_all[tl + 8:tl + 9, 0:nu], 0.0)
    u = proj[:, 0:nu]
    rows = lax.broadcasted_iota(jnp.int32, (tl, 1), 0)
    up = jnp.where(rows == 0, prev, pltpu.roll(u, 1, 0))
    dn = jnp.where(rows == tl - 1, nxt, pltpu.roll(u, tl - 1, 0))
    u_o[0] = up * sw_ref[0:1, :] + u * sw_ref[1:2, :] + dn * sw_ref[2:3, :] + sb_ref[...]
    g_o[0] = proj[:, nu:]


def _hy_proj(x, mod, ng, w, short_w, short_b):
    b, l, d = x.shape
    e = WIDTH
    nu = short_w.shape[1]
    tl = 256
    hb, lb = tl // 8, l // 8
    full = lambda shape: pl.BlockSpec(shape, lambda bi, ti: (0,) * len(shape))
    return pl.pallas_call(
        _hy_proj_kernel,
        grid=(b, l // tl),
        in_specs=[
            pl.BlockSpec((1, tl, d), lambda bi, ti: (bi, ti, 0)),
            pl.BlockSpec((1, 8, d), lambda bi, ti: (bi, jnp.maximum(ti * hb - 1, 0), 0)),
            pl.BlockSpec((1, 8, d), lambda bi, ti: (bi, jnp.minimum((ti + 1) * hb, lb - 1), 0)),
            pl.BlockSpec((1, 3, d), lambda bi, ti: (bi, 0, 0)),
            full((1, d)), full((d, nu + e)), full((3, nu)), full((1, nu)),
        ],
        out_specs=[pl.BlockSpec((1, tl, nu), lambda bi, ti: (bi, ti, 0)),
                   pl.BlockSpec((1, tl, e), lambda bi, ti: (bi, ti, 0))],
        out_shape=[jax.ShapeDtypeStruct((b, l, nu), F32), jax.ShapeDtypeStruct((b, l, e), F32)],
        compiler_params=_cparams("parallel", "arbitrary"),
        name="hyena_proj",
    )(x, x, x, mod, ng.reshape(1, d), w.astype(BF16), short_w, short_b.reshape(1, nu))


def _cmm(lhs, xre, xim):
    r = lhs.shape[0] // 2
    w = xre.shape[1]
    x = xre if xim is None else jnp.concatenate([xre, xim], axis=1)
    xh, xl = _split(x)
    out = _dot(lhs, jnp.concatenate([xh, xl, xh], axis=0))
    re_x, im_x = out[:r], out[r:]
    if xim is None:
        return re_x, im_x
    return re_x[:, :w] - im_x[:, w:], im_x[:, :w] + re_x[:, w:]


def _kstack(mre, mim, k):
    m = jnp.concatenate([mre, mim], axis=0)
    hi, lo = _split(m)
    if m.shape[1] == k:
        return jnp.concatenate([hi, hi, lo], axis=1)
    lane = lax.broadcasted_iota(jnp.int32, m.shape, 1)
    return jnp.where(lane < 2 * k, hi, lo)


def _fft_fwd_outer(src_re, src_im, dst_re, dst_im, fa_ref, n1, n2, k1, scale=None):
    fa = fa_ref[...]

    def body(i, carry):
        xre = src_re[pl.ds(i, k1, stride=n2), :]
        xim = None if src_im is None else src_im[pl.ds(i, k1, stride=n2), :]
        if scale is not None:
            xre = xre * scale
        are, aim = _cmm(fa, xre, xim)
        dst_re[pl.ds(i, n1, stride=n2), :] = are
        dst_im[pl.ds(i, n1, stride=n2), :] = aim
        return carry

    lax.fori_loop(0, n2, body, 0, unroll=FFT_UNROLL)


def _twiddled_inner(c2_ref, s2_ref, twr_ref, twi_ref, k):
    tre = twr_ref[pl.ds(k, 1), :]
    tim = twi_ref[pl.ds(k, 1), :]
    c2 = c2_ref[...]
    s2 = s2_ref[...]
    return _kstack(c2 * tre - s2 * tim, c2 * tim + s2 * tre, FFT_N2)


def _lconv_kernel(u_ref, x_ref, kf_ref, bias_ref, fa_ref, c2_ref, s2_ref, twr_ref, twi_ref, fc_ref,
                  c1_ref, s1_ref, twtr_ref, twti_ref, o_ref, wr_ref, wi_ref):
    l = u_ref.shape[1]
    n2 = FFT_N2
    n1 = 2 * l // n2
    half = n1 // 2

    _fft_fwd_outer(u_ref.at[0], u_ref.at[1], wr_ref, wi_ref, fa_ref, n1, n2, half)

    fc = fc_ref[...]

    def slab(k, carry):
        rows = pl.ds(pl.multiple_of(k * n2, n2), n2)
        g = _twiddled_inner(c2_ref, s2_ref, twr_ref, twi_ref, k)
        xre, xim = _cmm(g, wr_ref[rows, :], wi_ref[rows, :])
        kre = kf_ref[0, 0, rows, :]
        kim = kf_ref[0, 1, rows, :]
        yre = xre * kre - xim * kim
        yim = xre * kim + xim * kre
        are, aim = _cmm(fc, yre, yim)
        wr_ref[rows, :] = are
        wi_ref[rows, :] = aim
        return carry

    lax.fori_loop(0, n1, slab, 0, unroll=FFT_SLAB_UNROLL)

    def outer(i, carry):
        tre = twtr_ref[pl.ds(i, 1), :]
        tim = twti_ref[pl.ds(i, 1), :]
        c1 = c1_ref[...]
        s1 = s1_ref[...]
        qs = _kstack(c1 * tre - s1 * tim, -(c1 * tim + s1 * tre), n1)
        yre, yim = _cmm(qs, wr_ref[pl.ds(i, n1, stride=n2), :], wi_ref[pl.ds(i, n1, stride=n2), :])
        rows = pl.ds(i, half, stride=n2)
        o_ref[0, rows, :] = yre
        o_ref[1, rows, :] = yim
        return carry

    lax.fori_loop(0, n2, outer, 0, unroll=FFT_UNROLL)

    bias = bias_ref[0]
    step = min(EPILOGUE_ROWS, l)

    def gate(i, carry):
        rows = pl.ds(pl.multiple_of(i * step, step), step)
        for q in range(2):
            o_ref[q, rows, :] = x_ref[q, rows, :] * (o_ref[q, rows, :] + u_ref[q, rows, :] * bias)
        return carry

    lax.fori_loop(0, l // step, gate, 0)


FILTER_CHUNK = 512


def _filter_pos(base, chunk, l):
    m = base + lax.broadcasted_iota(jnp.int32, (chunk, 1), 0)
    is_f = m < l
    pos = jnp.where(is_f, m, 2 * l - m).astype(F32)
    return m, is_f, pos


def _filter_mlp_kernel(l, w1_ref, b1_ref, wm_ref, bm_ref, fr_ref, bands_ref, z_ref):
    chunk = z_ref.shape[0]
    _, _, pos = _filter_pos(pl.program_id(0) * chunk, chunk, l)
    lane = lax.broadcasted_iota(jnp.int32, (1, LANES), 1)
    t = pos / max(l - 1, 1)
    ang = (2.0 * math.pi / l) * pos * bands_ref[...]
    feat = jnp.where(lane == 0, t,
                     jnp.where(lane <= POS_BANDS, jnp.cos(ang),
                               jnp.where(lane <= 2 * POS_BANDS, -jnp.sin(ang), 0.0)))
    z = jnp.sin(fr_ref[0:1, :] * (_dot_x3(feat, w1_ref[...]) + b1_ref[...]))
    for i in range(2):
        z = jnp.sin(fr_ref[i + 1:i + 2, :] * (_dot_x3(z, wm_ref[i]) + bm_ref[i:i + 1, :]))
    z_ref[...] = z


def _filter_kernel(z_ref, wof_ref, wob_ref, dec_ref, fa_ref, c2_ref, s2_ref, twr_ref, twi_ref, o_ref):
    n = o_ref.shape[2]
    l = n // 2
    n2 = FFT_N2
    n1 = n // n2
    chunk = min(FILTER_CHUNK, n)
    nchunk = n // chunk
    wof = _split(wof_ref[...])
    wob = _split(wob_ref[...])
    dec_f = jnp.abs(dec_ref[0, 0:1, :])
    dec_b = jnp.abs(dec_ref[0, 1:2, :])

    def gen(c, asum):
        base = pl.multiple_of(c * chunk, chunk)
        m, is_f, pos = _filter_pos(base, chunk, l)
        t = pos / max(l - 1, 1)
        z = z_ref[pl.ds(base, chunk), :]
        hf = _dot_presplit(z, *wof) * jnp.exp(-t * dec_f)
        hb = _dot_presplit(z, *wob) * jnp.exp(-t * dec_b)
        k2 = jnp.where(is_f, hf, jnp.where(m == l, 0.0, hb))
        o_ref[0, 0, pl.ds(base, chunk), :] = k2
        return asum + jnp.sum(jnp.abs(k2), axis=0, keepdims=True)

    asum = lax.fori_loop(0, nchunk, gen, jnp.zeros((1, LANES), F32))
    inv = 1.0 / asum

    re_ref = o_ref.at[0, 0]
    im_ref = o_ref.at[0, 1]
    _fft_fwd_outer(re_ref, None, re_ref, im_ref, fa_ref, n1, n2, n1, scale=inv)

    def slab(k, carry):
        rows = pl.ds(pl.multiple_of(k * n2, n2), n2)
        g = _twiddled_inner(c2_ref, s2_ref, twr_ref, twi_ref, k)
        xre, xim = _cmm(g, re_ref[rows, :], im_ref[rows, :])
        re_ref[rows, :] = xre
        im_ref[rows, :] = xim
        return carry

    lax.fori_loop(0, n1, slab, 0, unroll=FFT_SLAB_UNROLL)


def _dft_tables(l):
    n = 2 * l
    n2 = FFT_N2
    n1 = n // n2
    half = n1 // 2

    def cs(rows, cols, period):
        ang = 2.0 * np.pi * ((np.arange(rows)[:, None] * np.arange(cols)[None, :]) % period) / period
        return np.cos(ang), -np.sin(ang)

    def split_np(m):
        m32 = jnp.asarray(m, F32)
        hi = m32.astype(BF16)
        lo = (m32 - hi.astype(F32)).astype(BF16)
        return hi, lo

    def kstack_np(c, s):
        ch, cl = split_np(c)
        sh, sl = split_np(s)
        return jnp.concatenate([jnp.concatenate([ch, ch, cl], axis=1),
                                jnp.concatenate([sh, sh, sl], axis=1)], axis=0)

    def wide(m):
        return jnp.asarray(m if m.shape[1] % LANES == 0 else np.tile(m, (1, 3)), F32)

    c1, s1 = cs(n1, n1, n1)
    c2, s2 = cs(n2, n2, n2)
    tw_c, tw_s = cs(n1, n2, n)
    tabs = {}
    tabs["fa_half"] = kstack_np(c1[:, :half], s1[:, :half])
    tabs["fa_full"] = kstack_np(c1, s1)
    tabs["fc"] = kstack_np(c2, -s2)
    tabs["c2"] = wide(c2)
    tabs["s2"] = wide(s2)
    tabs["twr"] = wide(tw_c)
    tabs["twi"] = wide(tw_s)
    tabs["c1h"] = wide(c1[:half] / n)
    tabs["s1h"] = wide(s1[:half] / n)
    tabs["twtr"] = wide(tw_c.T)
    tabs["twti"] = wide(tw_s.T)
    return tabs


def _const_spec(a, nd_grid):
    shape = a.shape
    if nd_grid == 2:
        return pl.BlockSpec(shape, lambda i, j: (0,) * len(shape))
    return pl.BlockSpec(shape, lambda i: (0,) * len(shape))


def _hyena_filters(l, p, tabs):
    (f_w1, f_b1, f_w_mid, f_b_mid, f_freq, f_w_out, f_decay) = p
    n = 2 * l
    e = WIDTH
    nblk = e // LANES
    w1p = jnp.zeros((LANES, FILTER_HIDDEN), F32).at[:f_w1.shape[0]].set(f_w1)
    bands = np.linspace(1e-4, POS_BANDS - 1, POS_BANDS, dtype=np.float32)
    bl = np.zeros((1, LANES), np.float32)
    bl[0, 1:1 + POS_BANDS] = bands
    bl[0, 1 + POS_BANDS:1 + 2 * POS_BANDS] = bands
    consts = [tabs["fa_full"], tabs["c2"], tabs["s2"], tabs["twr"], tabs["twi"]]
    chunk = min(FILTER_CHUNK, n)
    mlp_in = [w1p, f_b1.reshape(1, FILTER_HIDDEN), f_w_mid, f_b_mid, f_freq, jnp.asarray(bl)]
    z = pl.pallas_call(
        functools.partial(_filter_mlp_kernel, l),
        grid=(n // chunk,),
        in_specs=[_const_spec(a, 1) for a in mlp_in],
        out_specs=pl.BlockSpec((chunk, FILTER_HIDDEN), lambda c: (c, 0)),
        out_shape=jax.ShapeDtypeStruct((n, FILTER_HIDDEN), F32),
        compiler_params=_cparams("parallel"),
        name="hyena_filter_mlp",
    )(*mlp_in)
    return pl.pallas_call(
        _filter_kernel,
        grid=(2, nblk),
        in_specs=[
            pl.BlockSpec((n, FILTER_HIDDEN), lambda o, j: (0, 0)),
            pl.BlockSpec((FILTER_HIDDEN, LANES), lambda o, j: (0, o * 2 * nblk + j)),
            pl.BlockSpec((FILTER_HIDDEN, LANES), lambda o, j: (0, o * 2 * nblk + nblk + j)),
            pl.BlockSpec((1, 2, LANES), lambda o, j: (o, 0, j)),
        ] + [_const_spec(c, 2) for c in consts],
        out_specs=pl.BlockSpec((1, 2, n, LANES), lambda o, j: (o, 0, 0, j)),
        out_shape=jax.ShapeDtypeStruct((2, 2, n, e), F32),
        compiler_params=_cparams("parallel", "arbitrary"),
        name="hyena_filter",
    )(z, f_w_out, f_w_out, f_decay.reshape(2, 2, e), *consts)


def _long_conv(u, xg, kf, order, bias, tabs, col_u, col_x):
    b, l, _ = u.shape
    e = WIDTH
    n = 2 * l
    nblk = e // LANES
    consts = [tabs["fa_half"], tabs["c2"], tabs["s2"], tabs["twr"], tabs["twi"],
              tabs["fc"], tabs["c1h"], tabs["s1h"], tabs["twtr"], tabs["twti"]]
    return pl.pallas_call(
        _lconv_kernel,
        grid=(nblk, b // 2),
        in_specs=[
            pl.BlockSpec((2, l, LANES), lambda j, pi: (pi, 0, col_u * nblk + j)),
            pl.BlockSpec((2, l, LANES), lambda j, pi: (pi, 0, col_x * nblk + j)),
            pl.BlockSpec((1, 2, n, LANES), lambda j, pi: (order, 0, 0, j)),
            pl.BlockSpec((1, 1, LANES), lambda j, pi: (order, 0, j)),
        ] + [_const_spec(c, 2) for c in consts],
        out_specs=pl.BlockSpec((2, l, LANES), lambda j, pi: (pi, 0, j)),
        out_shape=jax.ShapeDtypeStruct((b, l, e), F32),
        scratch_shapes=[pltpu.VMEM((n, LANES), F32), pltpu.VMEM((n, LANES), F32)],
        compiler_params=_cparams("parallel", "arbitrary"),
        name="hyena_long_conv",
    )(u, xg, kf, bias.reshape(2, 1, e), *consts)


def _gated_out_kernel(z_ref, g_ref, x_ref, mod_ref, w_ref, o_ref):
    zz = (z_ref[0] * _silu(g_ref[0])).astype(BF16)
    o_ref[0] = x_ref[0] + mod_ref[0, 2:3, :] * _dot(zz, w_ref[...])


def _gated_out(z, g_arr, g_col, x, mod, w_out, name):
    b, l, d = x.shape
    e = WIDTH
    tl = 256
    tile = lambda col: pl.BlockSpec((1, tl, e), lambda bi, ti: (bi, ti, col))
    return pl.pallas_call(
        _gated_out_kernel,
        grid=(b, l // tl),
        in_specs=[tile(0), tile(g_col), tile(0),
                  pl.BlockSpec((1, 3, d), lambda bi, ti: (bi, 0, 0)),
                  pl.BlockSpec((e, d), lambda bi, ti: (0, 0))],
        out_specs=tile(0),
        out_shape=jax.ShapeDtypeStruct((b, l, d), F32),
        compiler_params=_cparams("parallel", "arbitrary"),
        name=name,
    )(z, g_arr, x, mod, w_out.astype(BF16))


def _hyena_layer(x, mod, ng, p, tabs):
    (w_in, short_w, short_b, f_w1, f_b1, f_w_mid, f_b_mid, f_freq, f_w_out, f_decay, bias, w_out) = p
    l = x.shape[1]
    u3, g = _hy_proj(x, mod, ng, w_in, short_w, short_b)
    kf = _hyena_filters(l, (f_w1, f_b1, f_w_mid, f_b_mid, f_freq, f_w_out, f_decay), tabs)
    z1 = _long_conv(u3, u3, kf, 0, bias, tabs, col_u=2, col_x=0)
    z2 = _long_conv(z1, u3, kf, 1, bias, tabs, col_u=0, col_x=1)
    return _gated_out(z2, g, 0, x, mod, w_out, "hyena_out")


def _cf_proj_kernel(x_ref, mod_ref, ng_ref, w_ref, u_o, sg_o):
    e = WIDTH
    h = _prenorm(x_ref[0], ng_ref[...], mod_ref[0, 0:1, :], mod_ref[0, 1:2, :]).astype(BF16)
    a = _dot(h, w_ref[:, 0:e])
    bgate = _dot(h, w_ref[:, e:2 * e])
    g = _dot(h, w_ref[:, 2 * e:3 * e])
    u_o[0] = a * _sigmoid(bgate)
    sg_o[0] = _silu(g)


def _cf_out_kernel(u_ref, up_ref, un_ref, sg_ref, x_ref, mod_ref, dw_ref, db_ref, lg_ref, lb_ref, w_ref,
                   o_ref, ext_ref):
    t = pl.program_id(1)
    nt = pl.num_programs(1)
    tl = u_ref.shape[1]
    hh = CONV_HALO
    ext_ref[0:hh, :] = jnp.where(t > 0, up_ref[0], 0.0)
    ext_ref[hh:hh + tl, :] = u_ref[0]
    ext_ref[hh + tl:hh + tl + hh, :] = jnp.where(t < nt - 1, un_ref[0], 0.0)
    off = hh - (CONV_WIDTH - 1) // 2
    acc = jnp.zeros((tl, WIDTH), F32) + db_ref[...]
    ext = ext_ref[...]
    nrow = ext.shape[0]
    for phase in range(8):
        win = ext if phase == 0 else pltpu.roll(ext, nrow - phase, 0)
        for k in range(CONV_WIDTH):
            if (off + k) % 8 == phase:
                base = off + k - phase
                acc = acc + win[base:base + tl, :] * dw_ref[k:k + 1, :]
    mean = jnp.mean(acc, axis=-1, keepdims=True)
    cen = acc - mean
    var = jnp.mean(cen * cen, axis=-1, keepdims=True)
    y = cen * lax.rsqrt(var + LN_EPS) * lg_ref[...] + lb_ref[...]
    zz = (_silu(y) * sg_ref[0]).astype(BF16)
    o_ref[0] = x_ref[0] + mod_ref[0, 2:3, :] * _dot(zz, w_ref[...])


def _conformer_layer(x, mod, ng, p):
    (w_in, dw_w, dw_b, ln_g, ln_b, w_out) = p
    b, l, d = x.shape
    e = WIDTH
    tl = 256
    tile = pl.BlockSpec((1, tl, e), lambda bi, ti: (bi, ti, 0))
    modspec = pl.BlockSpec((1, 3, d), lambda bi, ti: (bi, 0, 0))
    vec = pl.BlockSpec((1, e), lambda bi, ti: (0, 0))
    u, sg = pl.pallas_call(
        _cf_proj_kernel,
        grid=(b, l // tl),
        in_specs=[tile, modspec, vec, pl.BlockSpec((d, 3 * e), lambda bi, ti: (0, 0))],
        out_specs=[tile, tile],
        out_shape=[jax.ShapeDtypeStruct((b, l, e), F32)] * 2,
        compiler_params=_cparams("parallel", "arbitrary"),
        name="conformer_proj",
    )(x, mod, ng.reshape(1, d), w_in.astype(BF16))

    hh = CONV_HALO
    hb = tl // hh
    lb = l // hh
    return pl.pallas_call(
        _cf_out_kernel,
        grid=(b, l // tl),
        in_specs=[
            tile,
            pl.BlockSpec((1, hh, e), lambda bi, ti: (bi, jnp.maximum(ti * hb - 1, 0), 0)),
            pl.BlockSpec((1, hh, e), lambda bi, ti: (bi, jnp.minimum((ti + 1) * hb, lb - 1), 0)),
            tile, tile, modspec,
            pl.BlockSpec((CONV_WIDTH, e), lambda bi, ti: (0, 0)),
            vec, vec, vec,
            pl.BlockSpec((e, d), lambda bi, ti: (0, 0)),
        ],
        out_specs=tile,
        out_shape=jax.ShapeDtypeStruct((b, l, d), F32),
        scratch_shapes=[pltpu.VMEM((tl + 2 * hh, e), F32)],
        compiler_params=_cparams("parallel", "arbitrary"),
        name="conformer_out",
    )(u, u, u, sg, x, mod, dw_w, dw_b.reshape(1, e), ln_g.reshape(1, e), ln_b.reshape(1, e),
      w_out.astype(BF16))


def _final_norm_kernel(x_ref, g_ref, o_ref):
    x = x_ref[0]
    ms = jnp.mean(x * x, axis=-1, keepdims=True)
    o_ref[0] = x * lax.rsqrt(ms + RMS_EPS) * g_ref[...]


def _final_norm(x, g):
    b, l, d = x.shape
    tl = 512
    tile = pl.BlockSpec((1, tl, d), lambda bi, ti: (bi, ti, 0))
    return pl.pallas_call(
        _final_norm_kernel,
        grid=(b, l // tl),
        in_specs=[tile, pl.BlockSpec((1, d), lambda bi, ti: (0, 0))],
        out_specs=tile,
        out_shape=jax.ShapeDtypeStruct((b, l, d), F32),
        compiler_params=_cparams("parallel", "arbitrary"),
        name="final_norm",
    )(x, g.reshape(1, d))


def _seg_constants():
    seg = np.arange(SEG_GROUP) // HEAD
    bd = (seg[:, None] == seg[None, :]).astype(np.float32)
    col_head = np.arange(SEG_GROUP // HEAD * SCAN_CHUNK) // SCAN_CHUNK
    bd_sj = (col_head[:, None] == seg[None, :]).astype(np.float32)
    blk = np.arange(SEG_GROUP) // SCAN_CHUNK
    bd_inv = (blk[:, None] == blk[None, :]).astype(np.float32)
    return {"bd": jnp.asarray(bd, BF16), "bd_sj": jnp.asarray(bd_sj, BF16), "bd_inv": jnp.asarray(bd_inv, BF16)}


def _trunk(x, mod, norm_g, rw, hy, cf, final_g, consts):
    tabs = None
    for i in range(DEPTH):
        kind, j = i % 3, i // 3
        m = mod[i]
        last = i == DEPTH - 1
        if kind == 0:
            x = _rwkv_layer(x, m, norm_g[i], [p[j] for p in rw], consts, final_g if last else None)
            if last:
                return x
        elif kind == 1:
            if tabs is None:
                tabs = _dft_tables(x.shape[1])
            x = _hyena_layer(x, m, norm_g[i], [p[j] for p in hy], tabs)
        else:
            x = _conformer_layer(x, m, norm_g[i], [p[j] for p in cf])
    return _final_norm(x, final_g)


def kernel(x_prompt, x_sample, c_prompt, c_sample, norm_g, mod_w, mod_b, rw_w_in, rw_mu, rw_w0, rw_w2, rw_a0, rw_a2, rw_k_k, rw_k_a, rw_r_k, rw_gn_w, rw_gn_b, rw_w_out, hy_w_in, hy_short_w, hy_short_b, hy_f_w1, hy_f_b1, hy_f_w_mid, hy_f_b_mid, hy_f_freq, hy_f_w_out, hy_f_decay, hy_bias, hy_w_out, cf_w_in, cf_dw_w, cf_dw_b, cf_ln_g, cf_ln_b, cf_w_out, final_g):
    rw = (rw_w_in, rw_mu, rw_w0, rw_w2, rw_a0, rw_a2, rw_k_k, rw_k_a, rw_r_k, rw_gn_w, rw_gn_b, rw_w_out)
    hy = (hy_w_in, hy_short_w, hy_short_b, hy_f_w1, hy_f_b1, hy_f_w_mid, hy_f_b_mid, hy_f_freq, hy_f_w_out,
          hy_f_decay, hy_bias, hy_w_out)
    cf = (cf_w_in, cf_dw_w, cf_dw_b, cf_ln_g, cf_ln_b, cf_w_out)
    consts = _seg_constants()
    bp = x_prompt.shape[0]
    mod = _modulation(jnp.concatenate([c_prompt, c_sample], axis=0), mod_w, mod_b)
    y_prompt = _trunk(x_prompt, mod[:, :bp], norm_g, rw, hy, cf, final_g, consts)
    y_sample = _trunk(x_sample, mod[:, bp:], norm_g, rw, hy, cf, final_g, consts)
    return (y_prompt, y_sample)
```
